```python
import math
import jax, jax.numpy as jnp
from jax import lax
import numpy as np

D_MODEL = 2048
BATCH = 2
SEQ = 8192
DEPTH = 4

GRID_W = 64
HEAD_DIM = 128
D_MIX = D_MODEL
D_POOL = D_MIX // 4
D_ATTN = D_MIX // 2
D_MLSTM = D_MIX - D_POOL - D_ATTN
POOL_WINDOWS = (2, 4, 8, 16)
N_POOL_GROUPS = len(POOL_WINDOWS)
POOL_GW = D_POOL // N_POOL_GROUPS
N_Q_HEADS = D_ATTN // HEAD_DIM
N_KV_HEADS = max(1, N_Q_HEADS // 4)
KV_DIM = N_KV_HEADS * HEAD_DIM
ROPE_THETA = 10000.0
Q_BLOCK = 128
N_M_HEADS = D_MLSTM // HEAD_DIM
M_CHUNK = 128
M_CONV_W = 5
N_M_GATES = 4 * N_M_HEADS
D_IN = D_POOL + D_ATTN + 2 * KV_DIM + 3 * D_MLSTM + N_M_GATES
D_FF = 11 * D_MODEL // 4
N_EXPERTS = 8
TOP_K = 2
EXPERT_FF = 7 * D_MODEL // 4
N_DENSE = (DEPTH + 1) // 2
N_MOE = DEPTH // 2
EPS = 1e-6
FORGET_BIAS = 3.0

kernel_name = "hymba_pool_gqa_mlstm_moe_encoder"


def rms_norm(x, gain):
    xf = x.astype(jnp.float32)
    y = xf * lax.rsqrt(jnp.mean(xf * xf, axis=-1, keepdims=True) + EPS)
    return y.astype(x.dtype) * gain


def swiglu(h, w1, w3, w2):
    return (jax.nn.silu(h @ w1) * (h @ w3)) @ w2


def rope_1d(x, pos):
    half = x.shape[-1] // 2
    inv = ROPE_THETA ** (-jnp.arange(half, dtype=jnp.float32) / half)
    ang = pos[:, None] * inv[None, :]
    cos = jnp.cos(ang).astype(x.dtype)
    sin = jnp.sin(ang).astype(x.dtype)
    x1, x2 = x[..., :half], x[..., half:]
    return jnp.concatenate([x1 * cos - x2 * sin, x1 * sin + x2 * cos], axis=-1)


def axial_rope(x, row, col):
    d = x.shape[-1] // 2
    return jnp.concatenate([rope_1d(x[..., :d], row), rope_1d(x[..., d:], col)], axis=-1)


def pool_mixer(u, w_pool, scale):
    B, L, _ = u.shape
    uf = u.astype(jnp.float32)
    cs = jnp.concatenate([jnp.zeros((B, 1, D_POOL), jnp.float32), jnp.cumsum(uf, axis=1)], axis=1)
    t = np.arange(L)
    outs = []
    for g, w in enumerate(POOL_WINDOWS):
        sl = slice(g * POOL_GW, (g + 1) * POOL_GW)
        lo = np.clip(t - w // 2, 0, L - 1)
        hi = np.clip(t + w // 2 - 1, 0, L - 1)
        cnt = jnp.asarray((hi - lo + 1).astype(np.float32))
        csg = cs[..., sl]
        mean = (jnp.take(csg, hi + 1, axis=1) - jnp.take(csg, lo, axis=1)) / cnt[None, :, None]
        diff = (mean - uf[..., sl]).astype(u.dtype)
        outs.append(diff @ w_pool[g])
    return jnp.concatenate(outs, axis=-1) * scale


def gqa_attention(q, k, v):
    B, Hq, L, dh = q.shape
    Hkv = k.shape[1]
    G = Hq // Hkv
    NB = L // Q_BLOCK
    qb = jnp.moveaxis(q.reshape(B, Hkv, G, NB, Q_BLOCK, dh), 3, 0)
    sm_scale = 1.0 / math.sqrt(dh)

    def block(qblk):
        s = jnp.einsum('bhgqd,bhkd->bhgqk', qblk, k).astype(jnp.float32) * sm_scale
        p = jax.nn.softmax(s, axis=-1).astype(v.dtype)
        return jnp.einsum('bhgqk,bhkd->bhgqd', p, v)

    o = lax.map(block, qb)
    o = jnp.moveaxis(o, 0, 3).reshape(B, Hq, L, dh)
    return o.transpose(0, 2, 1, 3).reshape(B, L, Hq * dh)


def centred_dwconv(u, w):
    K, C = w.shape
    return lax.conv_general_dilated(u, w[:, None, :].astype(u.dtype), (1,), [(K // 2, K // 2)],
                                    dimension_numbers=('NWC', 'WIO', 'NWC'), feature_group_count=C)


def mlstm_scan_dir(q, k, v, i_pre, f_pre):
    B, H, L, d = q.shape
    NC = L // M_CHUNK
    CH = M_CHUNK
    q = q.reshape(B, H, NC, CH, d)
    k = k.reshape(B, H, NC, CH, d)
    v = v.reshape(B, H, NC, CH, d)
    logf = jax.nn.log_sigmoid(f_pre).reshape(B, H, NC, CH)
    logi = i_pre.reshape(B, H, NC, CH)
    b = jnp.cumsum(logf, axis=-1)
    g = b[..., -1]
    w_st = g[..., None] - b + logi
    m_loc = jnp.max(w_st, axis=-1)
    a_st = jnp.exp(w_st - m_loc[..., None])
    S_c = jnp.einsum('bhcs,bhcsk,bhcsv->bhckv', a_st, k, v)
    n_c = jnp.einsum('bhcs,bhcsk->bhck', a_st, k)

    def step(carry, inp):
        S, n, m = carry
        S_i, n_i, ml, gg = inp
        m_new = jnp.maximum(gg + m, ml)
        decay = jnp.exp(gg + m - m_new)
        add = jnp.exp(ml - m_new)
        S_new = decay[..., None, None] * S + add[..., None, None] * S_i
        n_new = decay[..., None] * n + add[..., None] * n_i
        return (S_new, n_new, m_new), (S, n, m)

    init = (jnp.zeros((B, H, d, d), jnp.float32), jnp.zeros((B, H, d), jnp.float32), jnp.zeros((B, H), jnp.float32))
    xs = (jnp.moveaxis(S_c, 2, 0), jnp.moveaxis(n_c, 2, 0), jnp.moveaxis(m_loc, 2, 0), jnp.moveaxis(g, 2, 0))
    _, (S_p, n_p, m_p) = lax.scan(step, init, xs)
    S_p = jnp.moveaxis(S_p, 0, 2)
    n_p = jnp.moveaxis(n_p, 0, 2)
    m_p = jnp.moveaxis(m_p, 0, 2)
    lower = jnp.tril(jnp.ones((CH, CH), dtype=bool))
    Dm = jnp.where(lower, b[..., :, None] - b[..., None, :] + logi[..., None, :], -jnp.inf)
    m_inter = b + m_p[..., None]
    m_t = jnp.maximum(m_inter, jnp.max(Dm, axis=-1))
    P = jnp.exp(Dm - m_t[..., None])
    sc = jnp.einsum('bhctd,bhcsd->bhcts', q, k) * P
    w_inter = jnp.exp(m_inter - m_t)
    num = jnp.einsum('bhcts,bhcsv->bhctv', sc, v) + w_inter[..., None] * jnp.einsum('bhctk,bhckv->bhctv', q, S_p)
    den = jnp.sum(sc, axis=-1) + w_inter * jnp.einsum('bhctk,bhck->bhct', q, n_p)
    h = num / jnp.maximum(jnp.abs(den), jnp.exp(-m_t))[..., None]
    return h.reshape(B, H, L, d)


def mlstm_mixer(u, v, o_pre, gates, conv_w, wq, wk, gate_b, norm_g):
    B, L, _ = u.shape
    d = HEAD_DIM
    uc = jax.nn.silu(centred_dwconv(u, conv_w)).reshape(B, L, N_M_HEADS, d)
    q = jnp.einsum('blhd,hde->bhle', uc, wq).astype(jnp.float32)
    k = (jnp.einsum('blhd,hde->bhle', uc, wk) * (d ** -0.5)).astype(jnp.float32)
    vh = v.reshape(B, L, N_M_HEADS, d).transpose(0, 2, 1, 3).astype(jnp.float32)
    gt = (gates + gate_b).astype(jnp.float32).reshape(B, L, 4, N_M_HEADS).transpose(2, 0, 3, 1)
    i_f, f_f, i_b, f_b = gt
    h_fwd = mlstm_scan_dir(q, k, vh, i_f, f_f)
    h_bwd = jnp.flip(mlstm_scan_dir(jnp.flip(q, 2), jnp.flip(k, 2), jnp.flip(vh, 2),
                                    jnp.flip(i_b, -1), jnp.flip(f_b, -1)), 2)
    h = rms_norm((h_fwd + h_bwd).transpose(0, 2, 1, 3), norm_g)
    h = h.reshape(B, L, D_MLSTM).astype(u.dtype)
    return jax.nn.sigmoid(o_pre) * h


def moe_ffn(h, router_w, router_b, w1, w3, w2):
    B, L, D = h.shape
    hf = h.reshape(B * L, D)
    logits = (hf @ router_w + router_b).astype(jnp.float32)
    top_v, top_i = lax.top_k(logits, TOP_K)
    probs = jax.nn.softmax(top_v, axis=-1)
    combine = jnp.sum(jax.nn.one_hot(top_i, N_EXPERTS, dtype=jnp.float32) * probs[..., None], axis=1).astype(h.dtype)
    out = jnp.zeros_like(hf)
    for e in range(N_EXPERTS):
        out = out + combine[:, e:e + 1] * swiglu(hf, w1[e], w3[e], w2[e])
    return out.reshape(B, L, D)


def setup_inputs(seed: int = 0) -> dict:
    key = jax.random.key(seed)
    ks = iter(jax.random.split(key, 32))
    D = D_MODEL

    def nrm(shape, s):
        return jax.random.normal(next(ks), shape, jnp.float32) * s

    def gain(shape):
        return 1.0 + nrm(shape, 0.05)

    gate_offset = jnp.tile(jnp.repeat(jnp.array([0.0, FORGET_BIAS], jnp.float32), N_M_HEADS), 2)
    return {
        "x": nrm((BATCH, SEQ, D), 1.0),
        "c": nrm((BATCH, D), 1.0),
        "ada_w": nrm((DEPTH, D, 6 * D), 0.5 * D ** -0.5),
        "ada_b": nrm((DEPTH, 6 * D), 0.02),
        "norm1_g": gain((DEPTH, D)),
        "norm2_g": gain((DEPTH, D)),
        "w_in": nrm((DEPTH, D, D_IN), D ** -0.5),
        "w_out": nrm((DEPTH, D_MIX, D), D_MIX ** -0.5),
        "pool_w": nrm((DEPTH, N_POOL_GROUPS, POOL_GW, POOL_GW), POOL_GW ** -0.5),
        "pool_scale": gain((DEPTH, D_POOL)),
        "q_norm_g": gain((DEPTH, HEAD_DIM)),
        "k_norm_g": gain((DEPTH, HEAD_DIM)),
        "m_conv_w": nrm((DEPTH, M_CONV_W, D_MLSTM), M_CONV_W ** -0.5),
        "m_wq": nrm((DEPTH, N_M_HEADS, HEAD_DIM, HEAD_DIM), HEAD_DIM ** -0.5),
        "m_wk": nrm((DEPTH, N_M_HEADS, HEAD_DIM, HEAD_DIM), HEAD_DIM ** -0.5),
        "m_gate_b": nrm((DEPTH, N_M_GATES), 0.1) + gate_offset,
        "m_norm_g": gain((DEPTH, HEAD_DIM)),
        "ffn_w1": nrm((N_DENSE, D, D_FF), D ** -0.5),
        "ffn_w3": nrm((N_DENSE, D, D_FF), D ** -0.5),
        "ffn_w2": nrm((N_DENSE, D_FF, D), D_FF ** -0.5),
        "router_w": nrm((N_MOE, D, N_EXPERTS), D ** -0.5),
        "router_b": nrm((N_MOE, N_EXPERTS), 0.01),
        "moe_w1": nrm((N_MOE, N_EXPERTS, D, EXPERT_FF), D ** -0.5),
        "moe_w3": nrm((N_MOE, N_EXPERTS, D, EXPERT_FF), D ** -0.5),
        "moe_w2": nrm((N_MOE, N_EXPERTS, EXPERT_FF, D), EXPERT_FF ** -0.5),
        "final_norm_g": gain((D,)),
    }


def reference(x, c, ada_w, ada_b, norm1_g, norm2_g, w_in, w_out, pool_w, pool_scale, q_norm_g, k_norm_g,
              m_conv_w, m_wq, m_wk, m_gate_b, m_norm_g, ffn_w1, ffn_w3, ffn_w2, router_w, router_b,
              moe_w1, moe_w3, moe_w2, final_norm_g):
    B, L, _ = x.shape
    rows = L // GRID_W
    row = jnp.broadcast_to(jnp.arange(rows, dtype=jnp.float32)[:, None], (rows, GRID_W)).reshape(L)
    col = jnp.broadcast_to(jnp.arange(GRID_W, dtype=jnp.float32)[None, :], (rows, GRID_W)).reshape(L)
    c_act = jax.nn.silu(c)
    splits = list(np.cumsum([D_POOL, D_ATTN, KV_DIM, KV_DIM, D_MLSTM, D_MLSTM, D_MLSTM]))
    for l in range(DEPTH):
        mod = c_act @ ada_w[l] + ada_b[l]
        sh1, sc1, g1, sh2, sc2, g2 = jnp.split(mod, 6, axis=-1)
        h = rms_norm(x, norm1_g[l]) * (1.0 + sc1[:, None]) + sh1[:, None]
        z = h @ w_in[l]
        zp, zq, zk, zv, zu, zmv, zo, zg = jnp.split(z, splits, axis=-1)
        y_pool = pool_mixer(zp, pool_w[l], pool_scale[l])
        q = zq.reshape(B, L, N_Q_HEADS, HEAD_DIM).transpose(0, 2, 1, 3)
        k = zk.reshape(B, L, N_KV_HEADS, HEAD_DIM).transpose(0, 2, 1, 3)
        v = zv.reshape(B, L, N_KV_HEADS, HEAD_DIM).transpose(0, 2, 1, 3)
        q = axial_rope(rms_norm(q, q_norm_g[l]), row, col)
        k = axial_rope(rms_norm(k, k_norm_g[l]), row, col)
        y_attn = gqa_attention(q, k, v)
        y_m = mlstm_mixer(zu, zmv, zo, zg, m_conv_w[l], m_wq[l], m_wk[l], m_gate_b[l], m_norm_g[l])
        y = jnp.concatenate([y_pool, y_attn, y_m], axis=-1) @ w_out[l]
        x = x + g1[:, None] * y
        h = rms_norm(x, norm2_g[l]) * (1.0 + sc2[:, None]) + sh2[:, None]
        if l % 2 == 0:
            f = swiglu(h, ffn_w1[l // 2], ffn_w3[l // 2], ffn_w2[l // 2])
        else:
            f = moe_ffn(h, router_w[l // 2], router_b[l // 2], moe_w1[l // 2], moe_w3[l // 2], moe_w2[l // 2])
        x = x + g2[:, None] * f
    return rms_norm(x, final_norm_g)
```

```python
import functools
import math

import numpy as np
import jax
import jax.numpy as jnp
from jax import lax
from jax.experimental import pallas as pl
from jax.experimental.pallas import tpu as pltpu

F32 = jnp.float32
BF16 = jnp.bfloat16

EPS = 1e-6
HEAD_DIM = 128
GRID_W = 64
ROPE_THETA = 10000.0
POOL_WINDOWS = (2, 4, 8, 16)
M_CHUNK = 128
M_CONV_W = 5
N_M_HEADS = 4
N_Q_HEADS = 8
N_KV_HEADS = 2
Q_PER_KV = N_Q_HEADS // N_KV_HEADS
N_EXPERTS = 8
LANES = 128
HALO = 16
VMEM_LIMIT = 48 * 1024 * 1024
LOG2E = math.log2(math.e)


def _cparams(sem):
    return pltpu.CompilerParams(dimension_semantics=sem, vmem_limit_bytes=VMEM_LIMIT)


def _silu(a):
    return a * jax.nn.sigmoid(a)


def _mod_kernel(c_ref, w_ref, b_ref, o_ref):
    ca = _silu(c_ref[...])
    o_ref[0] = jnp.dot(ca.astype(BF16), w_ref[0].astype(BF16), preferred_element_type=F32) + b_ref[0]


def _adaln_mod(c, ada_w, ada_b, tn=768):
    depth, d, n6 = ada_w.shape
    b = c.shape[0]
    cp = jnp.zeros((8, d), F32).at[:b].set(c)
    out = pl.pallas_call(
        _mod_kernel,
        grid=(depth, n6 // tn),
        in_specs=[
            pl.BlockSpec((8, d), lambda l, j: (0, 0)),
            pl.BlockSpec((1, d, tn), lambda l, j: (l, 0, j)),
            pl.BlockSpec((1, 1, tn), lambda l, j: (l, 0, j)),
        ],
        out_specs=pl.BlockSpec((1, 8, tn), lambda l, j: (l, 0, j)),
        out_shape=jax.ShapeDtypeStruct((depth, 8, n6), F32),
        compiler_params=_cparams(("arbitrary", "arbitrary")),
        name="adaln_mod",
    )(cp, ada_w, ada_b.reshape(depth, 1, n6))
    return out[:, :b]


def _norm_kernel(*refs, modulate):
    if modulate:
        x_ref, g_ref, sc_ref, sh_ref, o_ref = refs
    else:
        x_ref, g_ref, o_ref = refs
    x = x_ref[...]
    ms = jnp.mean(x * x, axis=-1, keepdims=True)
    y = x * lax.rsqrt(ms + EPS) * g_ref[...]
    if modulate:
        y = y * (1.0 + sc_ref[0]) + sh_ref[0]
    o_ref[...] = y.astype(o_ref.dtype)


def _norm(x, g, sc=None, sh=None, *, rows_per_batch, out_dtype, tm=256):
    n, d = x.shape
    tpb = rows_per_batch // tm
    modulate = sc is not None
    in_specs = [pl.BlockSpec((tm, d), lambda i: (i, 0)), pl.BlockSpec((1, d), lambda i: (0, 0))]
    args = [x, g.reshape(1, d)]
    if modulate:
        in_specs += [pl.BlockSpec((1, 1, d), lambda i: (i // tpb, 0, 0))] * 2
        args += [sc, sh]
    return pl.pallas_call(
        functools.partial(_norm_kernel, modulate=modulate),
        grid=(n // tm,),
        in_specs=in_specs,
        out_specs=pl.BlockSpec((tm, d), lambda i: (i, 0)),
        out_shape=jax.ShapeDtypeStruct((n, d), out_dtype),
        compiler_params=_cparams(("arbitrary",)),
        name="rms_norm",
    )(*args)


def _mm_kernel(*refs, nk, resid):
    if resid:
        a_ref, w_ref, x_ref, g_ref, o_ref, *scr = refs
    else:
        a_ref, w_ref, o_ref, *scr = refs

    def finish(acc):
        if resid:
            o_ref[...] = x_ref[...] + g_ref[0] * acc
        else:
            o_ref[...] = acc.astype(o_ref.dtype)

    prod = jnp.dot(a_ref[...], w_ref[...], preferred_element_type=F32)
    if nk == 1:
        finish(prod)
    else:
        acc_ref = scr[0]
        k = pl.program_id(2)

        @pl.when(k == 0)
        def _():
            acc_ref[...] = prod

        @pl.when(k > 0)
        def _():
            acc_ref[...] += prod

        @pl.when(k == nk - 1)
        def _():
            finish(acc_ref[...])


def _mm(a, w, *, tm, tn, tk, out_dtype=F32, x=None, g=None, rows_per_batch=None):
    m, kdim = a.shape
    n = w.shape[1]
    nk = kdim // tk
    resid = x is not None
    in_specs = [pl.BlockSpec((tm, tk), lambda i, j, k: (i, k)), pl.BlockSpec((tk, tn), lambda i, j, k: (k, j))]
    args = [a, w]
    if resid:
        tpb = rows_per_batch // tm
        in_specs += [pl.BlockSpec((tm, tn), lambda i, j, k: (i, j)),
                     pl.BlockSpec((1, 1, tn), lambda i, j, k: (i // tpb, 0, j))]
        args += [x, g]
    return pl.pallas_call(
        functools.partial(_mm_kernel, nk=nk, resid=resid),
        grid=(m // tm, n // tn, nk),
        in_specs=in_specs,
        out_specs=pl.BlockSpec((tm, tn), lambda i, j, k: (i, j)),
        out_shape=jax.ShapeDtypeStruct((m, n), out_dtype),
        scratch_shapes=[pltpu.VMEM((tm, tn), F32)] if nk > 1 else [],
        compiler_params=_cparams(("arbitrary", "arbitrary", "arbitrary")),
        name="matmul_resid" if resid else "matmul",
    )(*args)


def _up_kernel(*refs, scaled):
    if scaled:
        h_ref, w1_ref, w3_ref, c_ref, o_ref = refs
    else:
        h_ref, w1_ref, w3_ref, o_ref = refs
    h = h_ref[...]
    a1 = jnp.dot(h, w1_ref[...], preferred_element_type=F32)
    a3 = jnp.dot(h, w3_ref[...], preferred_element_type=F32)
    r = _silu(a1) * a3
    if scaled:
        r = r * c_ref[...]
    o_ref[...] = r.astype(o_ref.dtype)


def _swiglu_up(h, w1, w3, comb=None, *, tm, tn):
    n, d = h.shape
    e, _, f = w1.shape
    nj = f // tn
    scaled = comb is not None
    wspec = pl.BlockSpec((None, d, tn), lambda i, jj: (jj // nj, 0, jj % nj))
    in_specs = [pl.BlockSpec((tm, d), lambda i, jj: (i, 0)), wspec, wspec]
    args = [h, w1, w3]
    if scaled:
        in_specs.append(pl.BlockSpec((None, tm, 1), lambda i, jj: (jj // nj, i, 0)))
        args.append(comb)
    return pl.pallas_call(
        functools.partial(_up_kernel, scaled=scaled),
        grid=(n // tm, e * nj),
        in_specs=in_specs,
        out_specs=pl.BlockSpec((tm, tn), lambda i, jj: (i, jj)),
        out_shape=jax.ShapeDtypeStruct((n, e * f), BF16),
        compiler_params=_cparams(("arbitrary", "arbitrary")),
        name="swiglu_up",
    )(*args)


def _halo_specs(tl, width, col_block, nl):
    hb = tl // HALO
    nh = nl * hb
    cur = pl.BlockSpec((tl, width), lambda b, i: (b * nl + i, col_block))
    prev = pl.BlockSpec((HALO, width), lambda b, i: (jnp.maximum(b * nh + i * hb - 1, 0), col_block))
    nxt = pl.BlockSpec((HALO, width), lambda b, i: (jnp.minimum(b * nh + (i + 1) * hb, 2 * nh - 1), col_block))
    return prev, cur, nxt


def _with_halo(prev_ref, cur_ref, next_ref, nl):
    i = pl.program_id(1)
    prev = jnp.where(i == 0, 0.0, prev_ref[...].astype(F32))
    nxt = jnp.where(i == nl - 1, 0.0, next_ref[...].astype(F32))
    return jnp.concatenate([prev, cur_ref[...].astype(F32), nxt], axis=0)


def _pool_kernel(prev_ref, cur_ref, next_ref, w_ref, s_ref, o_ref, *, tl, nl, seq):
    u = _with_halo(prev_ref, cur_ref, next_ref, nl)
    rows = tl + 2 * HALO
    t = (pl.program_id(1) * tl + lax.broadcasted_iota(jnp.int32, (tl, LANES), 0)).astype(F32)
    for g, w in enumerate(POOL_WINDOWS):
        half = w // 2
        ug = u[:, g * LANES:(g + 1) * LANES]
        s = ug + pltpu.roll(ug, 1, 0)
        sh = 1
        while sh < half:
            s = pltpu.roll(s, sh, 0) + pltpu.roll(s, rows - sh, 0)
            sh *= 2
        cnt = jnp.minimum(t + (half - 1), seq - 1.0) - jnp.maximum(t - half, 0.0) + 1.0
        mean = s[HALO:HALO + tl] / cnt
        diff = mean - ug[HALO:HALO + tl]
        y = jnp.dot(diff.astype(BF16), w_ref[g], preferred_element_type=F32)
        o_ref[:, g * LANES:(g + 1) * LANES] = (y * s_ref[:, g * LANES:(g + 1) * LANES]).astype(o_ref.dtype)


def _pool_mixer(z, w_pool, scale, *, batch, seq, tl=512):
    n = z.shape[0]
    nl = seq // tl
    width = len(POOL_WINDOWS) * LANES
    prev, cur, nxt = _halo_specs(tl, width, 0, nl)
    return pl.pallas_call(
        functools.partial(_pool_kernel, tl=tl, nl=nl, seq=seq),
        grid=(batch, nl),
        in_specs=[prev, cur, nxt,
                  pl.BlockSpec((len(POOL_WINDOWS), LANES, LANES), lambda b, i: (0, 0, 0)),
                  pl.BlockSpec((1, width), lambda b, i: (0, 0))],
        out_specs=pl.BlockSpec((tl, width), lambda b, i: (b * nl + i, 0)),
        out_shape=jax.ShapeDtypeStruct((n, width), BF16),
        compiler_params=_cparams(("arbitrary", "arbitrary")),
        name="pool_mixer",
    )(z, z, z, w_pool.astype(BF16), scale.reshape(1, width))


def _rope_tables(seq):
    rows = seq // GRID_W
    row = jnp.broadcast_to(jnp.arange(rows, dtype=F32)[:, None], (rows, GRID_W)).reshape(seq)
    col = jnp.broadcast_to(jnp.arange(GRID_W, dtype=F32)[None, :], (rows, GRID_W)).reshape(seq)
    half = HEAD_DIM // 4
    inv = ROPE_THETA ** (-jnp.arange(half, dtype=F32) / half)
    ar = row[:, None] * inv[None, :]
    ac = col[:, None] * inv[None, :]
    cos = jnp.concatenate([jnp.cos(ar), jnp.cos(ar), jnp.cos(ac), jnp.cos(ac)], axis=-1)
    sin = jnp.concatenate([-jnp.sin(ar), jnp.sin(ar), -jnp.sin(ac), jnp.sin(ac)], axis=-1)
    return cos, sin


def _rope_kernel(z_ref, cos_ref, sin_ref, qg_ref, kg_ref, o_ref):
    j = pl.program_id(2)
    cos = cos_ref[...]
    sin = sin_ref[...]
    lane = lax.broadcasted_iota(jnp.int32, cos.shape, 1)
    first = (lane % (HEAD_DIM // 2)) < (HEAD_DIM // 4)

    def norm_rope(xh, g, scale):
        xf = xh.astype(F32)
        ms = jnp.mean(xf * xf, axis=-1, keepdims=True)
        y = xf * lax.rsqrt(ms + EPS) * g
        rot = jnp.where(first, pltpu.roll(y, HEAD_DIM - HEAD_DIM // 4, 1), pltpu.roll(y, HEAD_DIM // 4, 1))
        return ((y * cos + rot * sin) * scale).astype(o_ref.dtype)

    @pl.when(j < 2)
    def _():
        for hh in range(4):
            sl = slice(hh * HEAD_DIM, (hh + 1) * HEAD_DIM)
            o_ref[:, sl] = norm_rope(z_ref[:, sl], qg_ref[...], LOG2E / math.sqrt(HEAD_DIM))

    @pl.when(j == 2)
    def _():
        for hh in range(2):
            sl = slice(hh * HEAD_DIM, (hh + 1) * HEAD_DIM)
            o_ref[:, sl] = norm_rope(z_ref[:, sl], kg_ref[...], 1.0)
        o_ref[:, 2 * HEAD_DIM:] = z_ref[:, 2 * HEAD_DIM:]


def _qk_norm_rope(z, cos, sin, qg, kg, *, batch, seq, tr=512):
    n = z.shape[0]
    nl = seq // tr
    return pl.pallas_call(
        _rope_kernel,
        grid=(batch, nl, 3),
        in_specs=[pl.BlockSpec((tr, 512), lambda b, i, j: (b * nl + i, 1 + j)),
                  pl.BlockSpec((tr, HEAD_DIM), lambda b, i, j: (i, 0)),
                  pl.BlockSpec((tr, HEAD_DIM), lambda b, i, j: (i, 0)),
                  pl.BlockSpec((1, HEAD_DIM), lambda b, i, j: (0, 0)),
                  pl.BlockSpec((1, HEAD_DIM), lambda b, i, j: (0, 0))],
        out_specs=pl.BlockSpec((tr, 512), lambda b, i, j: (b * nl + i, j)),
        out_shape=jax.ShapeDtypeStruct((n, 1536), BF16),
        compiler_params=_cparams(("arbitrary", "arbitrary", "arbitrary")),
        name="qk_norm_rope",
    )(z, cos, sin, qg.reshape(1, HEAD_DIM), kg.reshape(1, HEAD_DIM))


def _attn_kernel(q_ref, k_ref, v_ref, o_ref, m_scr, l_scr, acc_scr, *, nk, tk):
    ki = pl.program_id(3)

    @pl.when(ki == 0)
    def _():
        m_scr[...] = jnp.full(m_scr.shape, -jnp.inf, F32)
        l_scr[...] = jnp.zeros(l_scr.shape, F32)
        acc_scr[...] = jnp.zeros(acc_scr.shape, F32)

    k = k_ref[...]
    v = v_ref[...]
    for g in range(Q_PER_KV):
        q = q_ref[:, g * HEAD_DIM:(g + 1) * HEAD_DIM]
        s = lax.dot_general(q, k, (((1,), (1,)), ((), ())), preferred_element_type=F32)
        m_prev = m_scr[g]
        m_next = jnp.maximum(m_prev, jnp.max(s, axis=1, keepdims=True))
        p = jnp.exp2(s - pltpu.repeat(m_next, tk // LANES, 1))
        alpha = jnp.exp2(m_prev - m_next)
        l_scr[g] = alpha * l_scr[g] + jnp.sum(p, axis=1, keepdims=True)
        acc_scr[g] = alpha * acc_scr[g] + jnp.dot(p.astype(BF16), v, preferred_element_type=F32)
        m_scr[g] = m_next

    @pl.when(ki == nk - 1)
    def _():
        for g in range(Q_PER_KV):
            o_ref[:, g * HEAD_DIM:(g + 1) * HEAD_DIM] = (acc_scr[g] / l_scr[g]).astype(o_ref.dtype)


def _attention(qkv, *, batch, seq, tq=512, tk=512):
    n = qkv.shape[0]
    nq, nk = seq // tq, seq // tk
    qw = Q_PER_KV * HEAD_DIM
    kcol = N_Q_HEADS
    vcol = N_Q_HEADS + N_KV_HEADS
    return pl.pallas_call(
        functools.partial(_attn_kernel, nk=nk, tk=tk),
        grid=(batch, N_KV_HEADS, nq, nk),
        in_specs=[pl.BlockSpec((tq, qw), lambda b, h, qi, ki: (b * nq + qi, h)),
                  pl.BlockSpec((tk, HEAD_DIM), lambda b, h, qi, ki: (b * nk + ki, kcol + h)),
                  pl.BlockSpec((tk, HEAD_DIM), lambda b, h, qi, ki: (b * nk + ki, vcol + h))],
        out_specs=pl.BlockSpec((tq, qw), lambda b, h, qi, ki: (b * nq + qi, h)),
        out_shape=jax.ShapeDtypeStruct((n, N_Q_HEADS * HEAD_DIM), BF16),
        scratch_shapes=[pltpu.VMEM((Q_PER_KV, tq, HEAD_DIM), F32)] * 3,
        compiler_params=_cparams(("arbitrary", "arbitrary", "arbitrary", "arbitrary")),
        name="flash_attention",
    )(qkv, qkv, qkv)


def _mconv_kernel(prev_ref, cur_ref, next_ref, cw_ref, wq_ref, wk_ref, q_ref, k_ref, *, tl, nl):
    u = _with_halo(prev_ref, cur_ref, next_ref, nl)
    rows = tl + 2 * HALO
    acc = None
    for kk in range(M_CONV_W):
        sh = (M_CONV_W // 2 - kk) % rows
        tap = (pltpu.roll(u, sh, 0) if sh else u) * cw_ref[kk:kk + 1, :]
        acc = tap if acc is None else acc + tap
    uc = _silu(acc[HALO:HALO + tl]).astype(BF16)
    for h in range(N_M_HEADS):
        sl = slice(h * HEAD_DIM, (h + 1) * HEAD_DIM)
        q_ref[:, sl] = jnp.dot(uc[:, sl], wq_ref[h], preferred_element_type=F32).astype(q_ref.dtype)
        kh = jnp.dot(uc[:, sl], wk_ref[h], preferred_element_type=F32) * (HEAD_DIM ** -0.5)
        k_ref[:, sl] = kh.astype(k_ref.dtype)


def _mlstm_qk(z, conv_w, wq, wk, *, batch, seq, tl=512):
    n = z.shape[0]
    nl = seq // tl
    width = N_M_HEADS * HEAD_DIM
    prev, cur, nxt = _halo_specs(tl, width, 4, nl)
    wspec = pl.BlockSpec((N_M_HEADS, HEAD_DIM, HEAD_DIM), lambda b, i: (0, 0, 0))
    ospec = pl.BlockSpec((tl, width), lambda b, i: (b * nl + i, 0))
    return pl.pallas_call(
        functools.partial(_mconv_kernel, tl=tl, nl=nl),
        grid=(batch, nl),
        in_specs=[prev, cur, nxt, pl.BlockSpec((M_CONV_W, width), lambda b, i: (0, 0)), wspec, wspec],
        out_specs=[ospec, ospec],
        out_shape=[jax.ShapeDtypeStruct((n, width), BF16)] * 2,
        compiler_params=_cparams(("arbitrary", "arbitrary")),
        name="mlstm_conv_qk",
    )(z, z, z, conv_w, wq.astype(BF16), wk.astype(BF16))


def _split3(x):
    x1 = x.astype(BF16)
    r = x - x1.astype(F32)
    x2 = r.astype(BF16)
    x3 = (r - x2.astype(F32)).astype(BF16)
    return x1, x2, x3


def _mscan_kernel(qf_ref, kf_ref, vf_ref, gf_ref, gtf_ref, qb_ref, kb_ref, vb_ref, gb_ref, gtb_ref,
                  bias_ref, biast_ref, hf_ref, hb_ref, s_scr, n_scr, m_scr):
    c = pl.program_id(1)

    @pl.when(c == 0)
    def _():
        s_scr[...] = jnp.zeros(s_scr.shape, F32)
        n_scr[...] = jnp.zeros(n_scr.shape, F32)
        m_scr[...] = jnp.zeros(m_scr.shape, F32)

    ch = M_CHUNK
    ti = lax.broadcasted_iota(jnp.int32, (ch, ch), 0)
    si = lax.broadcasted_iota(jnp.int32, (ch, ch), 1)
    nh = N_M_HEADS

    for rev in (False, True):
        q_ref, k_ref, v_ref, g_ref, gt_ref, h_ref = (
            (qb_ref, kb_ref, vb_ref, gb_ref, gtb_ref, hb_ref) if rev else
            (qf_ref, kf_ref, vf_ref, gf_ref, gtf_ref, hf_ref))
        causal = (si >= ti) if rev else (si <= ti)
        tri = jnp.where(causal, 1.0, 0.0).astype(BF16)
        tri_t = jnp.where((ti >= si) if rev else (ti <= si), 1.0, 0.0).astype(BF16)
        gates = g_ref[...] + bias_ref[...]
        gates_t = gt_ref[...] + biast_ref[...]
        lf = jax.nn.log_sigmoid(gates)
        lf_t = jax.nn.log_sigmoid(gates_t)
        cum = sum(jnp.dot(tri, part, preferred_element_type=F32) for part in _split3(lf))
        cum_t = sum(jnp.dot(part, tri_t, preferred_element_type=F32) for part in _split3(lf_t))
        last = 0 if rev else ch - 1
        for h in range(nh):
            idx = (nh if rev else 0) + h
            icol = (2 * nh if rev else 0) + h
            fcol = icol + nh
            sl = slice(h * HEAD_DIM, (h + 1) * HEAD_DIM)
            q = q_ref[:, sl]
            k = k_ref[:, sl]
            v = v_ref[:, sl]
            b_col = cum[:, fcol:fcol + 1]
            b_row = cum_t[fcol:fcol + 1, :]
            li_col = gates[:, icol:icol + 1]
            li_row = gates_t[icol:icol + 1, :]
            gtot = cum[last:last + 1, fcol:fcol + 1]
            m_prev = m_scr[idx][:, :1]
            n_prev = n_scr[idx]
            s_prev = s_scr[idx]

            dm = jnp.where(causal, b_col - b_row + li_row, -jnp.inf)
            m_inter = b_col + m_prev
            m_t = jnp.maximum(m_inter, jnp.max(dm, axis=1, keepdims=True))
            p = jnp.exp(dm - m_t)
            qk = lax.dot_general(q, k, (((1,), (1,)), ((), ())), preferred_element_type=F32)
            sc = qk * p
            w_inter = jnp.exp(m_inter - m_t)
            num = (jnp.dot(sc.astype(BF16), v, preferred_element_type=F32)
                   + w_inter * jnp.dot(q, s_prev.astype(BF16), preferred_element_type=F32))
            den = (jnp.sum(sc, axis=1, keepdims=True)
                   + w_inter * jnp.sum(q.astype(F32) * n_prev, axis=1, keepdims=True))
            h_ref[:, sl] = num / jnp.maximum(jnp.abs(den), jnp.exp(-m_t))

            w_st = gtot - b_col + li_col
            m_loc = jnp.max(w_st, axis=0, keepdims=True)
            ak = jnp.exp(w_st - m_loc) * k.astype(F32)
            s_c = lax.dot_general(ak.astype(BF16), v, (((0,), (0,)), ((), ())), preferred_element_type=F32)
            n_c = jnp.sum(ak, axis=0, keepdims=True)
            m_new = jnp.maximum(gtot + m_prev, m_loc)
            decay = jnp.exp(gtot + m_prev - m_new)
            add = jnp.exp(m_loc - m_new)
            s_scr[idx] = decay * s_prev + add * s_c
            n_scr[idx] = decay * n_prev + add * n_c
            m_scr[idx] = jnp.broadcast_to(m_new, (1, HEAD_DIM))


def _mlstm_scan(qm, km, z, gates, gates_t, gate_b, *, batch, seq):
    n = qm.shape[0]
    nc = seq // M_CHUNK
    width = N_M_HEADS * HEAD_DIM
    ng = 4 * N_M_HEADS
    fwd = lambda b, c: (b * nc + c, 0)
    bwd = lambda b, c: (b * nc + nc - 1 - c, 0)
    fwd_v = lambda b, c: (b * nc + c, 5)
    bwd_v = lambda b, c: (b * nc + nc - 1 - c, 5)
    fwd_t = lambda b, c: (0, b * nc + c)
    bwd_t = lambda b, c: (0, b * nc + nc - 1 - c)
    blk = (M_CHUNK, width)
    bias = jnp.zeros((1, LANES), F32).at[0, :ng].set(gate_b)
    bias_t = jnp.broadcast_to(gate_b[:, None], (ng, M_CHUNK))
    const = lambda b, c: (0, 0)
    in_specs = [
        pl.BlockSpec(blk, fwd), pl.BlockSpec(blk, fwd), pl.BlockSpec(blk, fwd_v),
        pl.BlockSpec((M_CHUNK, LANES), fwd), pl.BlockSpec((ng, M_CHUNK), fwd_t),
        pl.BlockSpec(blk, bwd), pl.BlockSpec(blk, bwd), pl.BlockSpec(blk, bwd_v),
        pl.BlockSpec((M_CHUNK, LANES), bwd), pl.BlockSpec((ng, M_CHUNK), bwd_t),
        pl.BlockSpec((1, LANES), const), pl.BlockSpec((ng, M_CHUNK), const),
    ]
    return pl.pallas_call(
        _mscan_kernel,
        grid=(batch, nc),
        in_specs=in_specs,
        out_specs=[pl.BlockSpec(blk, fwd), pl.BlockSpec(blk, bwd)],
        out_shape=[jax.ShapeDtypeStruct((n, width), F32)] * 2,
        scratch_shapes=[pltpu.VMEM((2 * N_M_HEADS, HEAD_DIM, HEAD_DIM), F32),
                        pltpu.VMEM((2 * N_M_HEADS, 1, HEAD_DIM), F32),
                        pltpu.VMEM((2 * N_M_HEADS, 1, HEAD_DIM), F32)],
        compiler_params=_cparams(("arbitrary", "arbitrary")),
        name="mlstm_scan",
    )(qm, km, z, gates, gates_t, qm, km, z, gates, gates_t, bias, bias_t)


def _mout_kernel(hf_ref, hb_ref, o_ref, g_ref, y_ref):
    for h in range(N_M_HEADS):
        sl = slice(h * HEAD_DIM, (h + 1) * HEAD_DIM)
        x = hf_ref[:, sl] + hb_ref[:, sl]
        ms = jnp.mean(x * x, axis=-1, keepdims=True)
        hn = x * lax.rsqrt(ms + EPS) * g_ref[...]
        y_ref[:, sl] = (jax.nn.sigmoid(o_ref[:, sl].astype(F32)) * hn).astype(y_ref.dtype)


def _mlstm_out(hf, hb, z, norm_g, *, tm=512):
    n, width = hf.shape
    spec = pl.BlockSpec((tm, width), lambda i: (i, 0))
    return pl.pallas_call(
        _mout_kernel,
        grid=(n // tm,),
        in_specs=[spec, spec, pl.BlockSpec((tm, width), lambda i: (i, 6)),
                  pl.BlockSpec((1, HEAD_DIM), lambda i: (0, 0))],
        out_specs=spec,
        out_shape=jax.ShapeDtypeStruct((n, width), BF16),
        compiler_params=_cparams(("arbitrary",)),
        name="mlstm_out",
    )(hf, hb, z, norm_g.reshape(1, HEAD_DIM))


def _router_kernel(l_ref, b_ref, o_ref):
    lane = lax.broadcasted_iota(jnp.int32, l_ref.shape, 1).astype(F32)
    logit = jnp.where(lane < N_EXPERTS, l_ref[...] + b_ref[...], -jnp.inf)
    m1 = jnp.max(logit, axis=1, keepdims=True)
    i1 = jnp.min(jnp.where(logit == m1, lane, float(LANES)), axis=1, keepdims=True)
    rest = jnp.where(lane == i1, -jnp.inf, logit)
    m2 = jnp.max(rest, axis=1, keepdims=True)
    i2 = jnp.min(jnp.where(rest == m2, lane, float(LANES)), axis=1, keepdims=True)
    e = jnp.exp(m2 - m1)
    p1 = 1.0 / (1.0 + e)
    p2 = e / (1.0 + e)
    o_ref[...] = jnp.where(lane == i1, p1, 0.0) + jnp.where(lane == i2, p2, 0.0)


def _router(logits, bias, *, tm=1024):
    n = logits.shape[0]
    spec = pl.BlockSpec((tm, LANES), lambda i: (i, 0))
    bpad = jnp.zeros((1, LANES), F32).at[0, :N_EXPERTS].set(bias)
    return pl.pallas_call(
        _router_kernel,
        grid=(n // tm,),
        in_specs=[spec, pl.BlockSpec((1, LANES), lambda i: (0, 0))],
        out_specs=spec,
        out_shape=jax.ShapeDtypeStruct((n, LANES), F32),
        compiler_params=_cparams(("arbitrary",)),
        name="moe_router",
    )(logits, bpad)


def _pad_cols(w, width):
    return jnp.zeros((w.shape[0], width), w.dtype).at[:, :w.shape[1]].set(w)


def kernel(x, c, ada_w, ada_b, norm1_g, norm2_g, w_in, w_out, pool_w, pool_scale, q_norm_g, k_norm_g,
           m_conv_w, m_wq, m_wk, m_gate_b, m_norm_g, ffn_w1, ffn_w3, ffn_w2, router_w, router_b,
           moe_w1, moe_w3, moe_w2, final_norm_g):
    batch, seq, d = x.shape
    depth = ada_w.shape[0]
    n = batch * seq
    ng = 4 * N_M_HEADS
    d_main = w_in.shape[2] - ng
    xf = x.reshape(n, d)
    mod = _adaln_mod(c, ada_w, ada_b)
    cos, sin = _rope_tables(seq)
    dims = dict(batch=batch, seq=seq)

    for l in range(depth):
        sh1, sc1, g1, sh2, sc2, g2 = [mod[l, :, i * d:(i + 1) * d].reshape(batch, 1, d) for i in range(6)]
        h = _norm(xf, norm1_g[l], sc1, sh1, rows_per_batch=seq, out_dtype=BF16)
        z = _mm(h, w_in[l, :, :d_main].astype(BF16), tm=1024, tn=512, tk=d, out_dtype=BF16)
        gates = _mm(h, _pad_cols(w_in[l, :, d_main:], LANES).astype(BF16), tm=1024, tn=LANES, tk=d)
        y_pool = _pool_mixer(z, pool_w[l], pool_scale[l], **dims)
        qkv = _qk_norm_rope(z, cos, sin, q_norm_g[l], k_norm_g[l], **dims)
        y_attn = _attention(qkv, **dims)
        qm, km = _mlstm_qk(z, m_conv_w[l], m_wq[l], m_wk[l], **dims)
        hf, hb = _mlstm_scan(qm, km, z, gates, gates[:, :ng].T, m_gate_b[l], **dims)
        y_m = _mlstm_out(hf, hb, z, m_norm_g[l])
        y = jnp.concatenate([y_pool, y_attn, y_m], axis=-1)
        xf = _mm(y, w_out[l].astype(BF16), tm=1024, tn=1024, tk=d, x=xf, g=g1, rows_per_batch=seq)
        h2 = _norm(xf, norm2_g[l], sc2, sh2, rows_per_batch=seq, out_dtype=BF16)
        if l % 2 == 0:
            i = l // 2
            a = _swiglu_up(h2, ffn_w1[i][None].astype(BF16), ffn_w3[i][None].astype(BF16), tm=1024, tn=512)
            w2 = ffn_w2[i].astype(BF16)
        else:
            i = l // 2
            logits = _mm(h2, _pad_cols(router_w[i], LANES).astype(BF16), tm=1024, tn=LANES, tk=d)
            comb = _router(logits, router_b[i])
            comb_e = comb[:, :N_EXPERTS].T.reshape(N_EXPERTS, n, 1)
            a = _swiglu_up(h2, moe_w1[i].astype(BF16), moe_w3[i].astype(BF16), comb_e, tm=1024, tn=512)
            w2 = moe_w2[i].reshape(-1, d).astype(BF16)
        xf = _mm(a, w2, tm=1024, tn=1024, tk=512, x=xf, g=g2, rows_per_batch=seq)

    out = _norm(xf, final_norm_g, rows_per_batch=seq, out_dtype=F32)
    return out.reshape(batch, seq, d)
```

```python
import functools
import math

import numpy as np
import jax
import jax.numpy as jnp
from jax import lax
from jax.experimental import pallas as pl
from jax.experimental.pallas import tpu as pltpu

F32 = jnp.float32
BF16 = jnp.bfloat16

EPS = 1e-6
HEAD_DIM = 128
GRID_W = 64
ROPE_THETA = 10000.0
POOL_WINDOWS = (2, 4, 8, 16)
M_CHUNK = 128
M_CONV_W = 5
N_M_HEADS = 4
N_Q_HEADS = 8
N_KV_HEADS = 2
Q_PER_KV = N_Q_HEADS // N_KV_HEADS
N_EXPERTS = 8
LANES = 128
HALO = 16
VMEM_LIMIT = 48 * 1024 * 1024
LOG2E = math.log2(math.e)


def _cparams(sem):
    return pltpu.CompilerParams(dimension_semantics=sem, vmem_limit_bytes=VMEM_LIMIT)


def _silu(a):
    return a * jax.nn.sigmoid(a)


def _mod_kernel(c_ref, w_ref, b_ref, o_ref):
    ca = _silu(c_ref[...])
    o_ref[0] = jnp.dot(ca.astype(BF16), w_ref[0].astype(BF16), preferred_element_type=F32) + b_ref[0]


def _adaln_mod(c, ada_w, ada_b, tn=768):
    depth, d, n6 = ada_w.shape
    b = c.shape[0]
    cp = jnp.zeros((8, d), F32).at[:b].set(c)
    out = pl.pallas_call(
        _mod_kernel,
        grid=(depth, n6 // tn),
        in_specs=[
            pl.BlockSpec((8, d), lambda l, j: (0, 0)),
            pl.BlockSpec((1, d, tn), lambda l, j: (l, 0, j)),
            pl.BlockSpec((1, 1, tn), lambda l, j: (l, 0, j)),
        ],
        out_specs=pl.BlockSpec((1, 8, tn), lambda l, j: (l, 0, j)),
        out_shape=jax.ShapeDtypeStruct((depth, 8, n6), F32),
        compiler_params=_cparams(("arbitrary", "arbitrary")),
        name="adaln_mod",
    )(cp, ada_w, ada_b.reshape(depth, 1, n6))
    return out[:, :b]


def _norm_kernel(*refs, modulate):
    if modulate:
        x_ref, g_ref, sc_ref, sh_ref, o_ref = refs
    else:
        x_ref, g_ref, o_ref = refs
    x = x_ref[...]
    ms = jnp.mean(x * x, axis=-1, keepdims=True)
    y = x * lax.rsqrt(ms + EPS) * g_ref[...]
    if modulate:
        y = y * (1.0 + sc_ref[0]) + sh_ref[0]
    o_ref[...] = y.astype(o_ref.dtype)


def _norm(x, g, sc=None, sh=None, *, rows_per_batch, out_dtype, tm=256):
    n, d = x.shape
    tpb = rows_per_batch // tm
    modulate = sc is not None
    in_specs = [pl.BlockSpec((tm, d), lambda i: (i, 0)), pl.BlockSpec((1, d), lambda i: (0, 0))]
    args = [x, g.reshape(1, d)]
    if modulate:
        in_specs += [pl.BlockSpec((1, 1, d), lambda i: (i // tpb, 0, 0))] * 2
        args += [sc, sh]
    return pl.pallas_call(
        functools.partial(_norm_kernel, modulate=modulate),
        grid=(n // tm,),
        in_specs=in_specs,
        out_specs=pl.BlockSpec((tm, d), lambda i: (i, 0)),
        out_shape=jax.ShapeDtypeStruct((n, d), out_dtype),
        compiler_params=_cparams(("arbitrary",)),
        name="rms_norm",
    )(*args)


def _mm_kernel(*refs, nk, resid):
    if resid:
        a_ref, w_ref, x_ref, g_ref, o_ref, *scr = refs
    else:
        a_ref, w_ref, o_ref, *scr = refs

    def finish(acc):
        if resid:
            o_ref[...] = x_ref[...] + g_ref[0] * acc
        else:
            o_ref[...] = acc.astype(o_ref.dtype)

    prod = jnp.dot(a_ref[...], w_ref[...], preferred_element_type=F32)
    if nk == 1:
        finish(prod)
    else:
        acc_ref = scr[0]
        k = pl.program_id(2)

        @pl.when(k == 0)
        def _():
            acc_ref[...] = prod

        @pl.when(k > 0)
        def _():
            acc_ref[...] += prod

        @pl.when(k == nk - 1)
        def _():
            finish(acc_ref[...])


def _mm(a, w, *, tm, tn, tk, out_dtype=F32, x=None, g=None, rows_per_batch=None):
    m, kdim = a.shape
    n = w.shape[1]
    nk = kdim // tk
    resid = x is not None
    in_specs = [pl.BlockSpec((tm, tk), lambda i, j, k: (i, k)), pl.BlockSpec((tk, tn), lambda i, j, k: (k, j))]
    args = [a, w]
    if resid:
        tpb = rows_per_batch // tm
        in_specs += [pl.BlockSpec((tm, tn), lambda i, j, k: (i, j)),
                     pl.BlockSpec((1, 1, tn), lambda i, j, k: (i // tpb, 0, j))]
        args += [x, g]
    return pl.pallas_call(
        functools.partial(_mm_kernel, nk=nk, resid=resid),
        grid=(m // tm, n // tn, nk),
        in_specs=in_specs,
        out_specs=pl.BlockSpec((tm, tn), lambda i, j, k: (i, j)),
        out_shape=jax.ShapeDtypeStruct((m, n), out_dtype),
        scratch_shapes=[pltpu.VMEM((tm, tn), F32)] if nk > 1 else [],
        compiler_params=_cparams(("arbitrary", "arbitrary", "arbitrary")),
        name="matmul_resid" if resid else "matmul",
    )(*args)


def _up_kernel(*refs, scaled):
    if scaled:
        h_ref, w1_ref, w3_ref, c_ref, o_ref = refs
    else:
        h_ref, w1_ref, w3_ref, o_ref = refs
    h = h_ref[...]
    a1 = jnp.dot(h, w1_ref[...], preferred_element_type=F32)
    a3 = jnp.dot(h, w3_ref[...], preferred_element_type=F32)
    r = _silu(a1) * a3
    if scaled:
        r = r * c_ref[...]
    o_ref[...] = r.astype(o_ref.dtype)


def _swiglu_up(h, w1, w3, comb=None, *, tm, tn):
    n, d = h.shape
    e, _, f = w1.shape
    nj = f // tn
    scaled = comb is not None
    wspec = pl.BlockSpec((None, d, tn), lambda i, jj: (jj // nj, 0, jj % nj))
    in_specs = [pl.BlockSpec((tm, d), lambda i, jj: (i, 0)), wspec, wspec]
    args = [h, w1, w3]
    if scaled:
        in_specs.append(pl.BlockSpec((None, tm, 1), lambda i, jj: (jj // nj, i, 0)))
        args.append(comb)
    return pl.pallas_call(
        functools.partial(_up_kernel, scaled=scaled),
        grid=(n // tm, e * nj),
        in_specs=in_specs,
        out_specs=pl.BlockSpec((tm, tn), lambda i, jj: (i, jj)),
        out_shape=jax.ShapeDtypeStruct((n, e * f), BF16),
        compiler_params=_cparams(("arbitrary", "arbitrary")),
        name="swiglu_up",
    )(*args)


def _halo_specs(tl, width, col_block, nl, batch):
    hb = tl // HALO
    nh = nl * hb
    last = batch * nh - 1
    cur = pl.BlockSpec((tl, width), lambda b, i: (b * nl + i, col_block))
    prev = pl.BlockSpec((HALO, width), lambda b, i: (jnp.maximum(b * nh + i * hb - 1, 0), col_block))
    nxt = pl.BlockSpec((HALO, width), lambda b, i: (jnp.minimum(b * nh + (i + 1) * hb, last), col_block))
    return prev, cur, nxt


def _with_halo(prev_ref, cur_ref, next_ref, nl):
    i = pl.program_id(1)
    prev = jnp.where(i == 0, 0.0, prev_ref[...].astype(F32))
    nxt = jnp.where(i == nl - 1, 0.0, next_ref[...].astype(F32))
    return jnp.concatenate([prev, cur_ref[...].astype(F32), nxt], axis=0)


def _pool_kernel(prev_ref, cur_ref, next_ref, w_ref, s_ref, o_ref, *, tl, nl, seq):
    u = _with_halo(prev_ref, cur_ref, next_ref, nl)
    rows = tl + 2 * HALO
    t = (pl.program_id(1) * tl + lax.broadcasted_iota(jnp.int32, (tl, LANES), 0)).astype(F32)
    for g, w in enumerate(POOL_WINDOWS):
        half = w // 2
        ug = u[:, g * LANES:(g + 1) * LANES]
        s = ug + pltpu.roll(ug, 1, 0)
        sh = 1
        while sh < half:
            s = pltpu.roll(s, sh, 0) + pltpu.roll(s, rows - sh, 0)
            sh *= 2
        cnt = jnp.minimum(t + (half - 1), seq - 1.0) - jnp.maximum(t - half, 0.0) + 1.0
        mean = s[HALO:HALO + tl] / cnt
        diff = mean - ug[HALO:HALO + tl]
        y = jnp.dot(diff.astype(BF16), w_ref[g], preferred_element_type=F32)
        o_ref[:, g * LANES:(g + 1) * LANES] = (y * s_ref[:, g * LANES:(g + 1) * LANES]).astype(o_ref.dtype)


def _pool_mixer(z, w_pool, scale, *, batch, seq, tl=512):
    n = z.shape[0]
    nl = seq // tl
    width = len(POOL_WINDOWS) * LANES
    prev, cur, nxt = _halo_specs(tl, width, 0, nl, batch)
    return pl.pallas_call(
        functools.partial(_pool_kernel, tl=tl, nl=nl, seq=seq),
        grid=(batch, nl),
        in_specs=[prev, cur, nxt,
                  pl.BlockSpec((len(POOL_WINDOWS), LANES, LANES), lambda b, i: (0, 0, 0)),
                  pl.BlockSpec((1, width), lambda b, i: (0, 0))],
        out_specs=pl.BlockSpec((tl, width), lambda b, i: (b * nl + i, 0)),
        out_shape=jax.ShapeDtypeStruct((n, width), BF16),
        compiler_params=_cparams(("arbitrary", "arbitrary")),
        name="pool_mixer",
    )(z, z, z, w_pool.astype(BF16), scale.reshape(1, width))


def _rope_tables(seq):
    rows = seq // GRID_W
    row = jnp.broadcast_to(jnp.arange(rows, dtype=F32)[:, None], (rows, GRID_W)).reshape(seq)
    col = jnp.broadcast_to(jnp.arange(GRID_W, dtype=F32)[None, :], (rows, GRID_W)).reshape(seq)
    half = HEAD_DIM // 4
    inv = ROPE_THETA ** (-jnp.arange(half, dtype=F32) / half)
    ar = row[:, None] * inv[None, :]
    ac = col[:, None] * inv[None, :]
    cos = jnp.concatenate([jnp.cos(ar), jnp.cos(ar), jnp.cos(ac), jnp.cos(ac)], axis=-1)
    sin = jnp.concatenate([-jnp.sin(ar), jnp.sin(ar), -jnp.sin(ac), jnp.sin(ac)], axis=-1)
    return cos, sin


def _rope_kernel(z_ref, cos_ref, sin_ref, qg_ref, kg_ref, o_ref):
    j = pl.program_id(2)
    cos = cos_ref[...]
    sin = sin_ref[...]
    lane = lax.broadcasted_iota(jnp.int32, cos.shape, 1)
    first = (lane % (HEAD_DIM // 2)) < (HEAD_DIM // 4)

    def norm_rope(xh, g, scale):
        xf = xh.astype(F32)
        ms = jnp.mean(xf * xf, axis=-1, keepdims=True)
        y = xf * lax.rsqrt(ms + EPS) * g
        rot = jnp.where(first, pltpu.roll(y, HEAD_DIM - HEAD_DIM // 4, 1), pltpu.roll(y, HEAD_DIM // 4, 1))
        return ((y * cos + rot * sin) * scale).astype(o_ref.dtype)

    @pl.when(j < 2)
    def _():
        for hh in range(4):
            sl = slice(hh * HEAD_DIM, (hh + 1) * HEAD_DIM)
            o_ref[:, sl] = norm_rope(z_ref[:, sl], qg_ref[...], LOG2E / math.sqrt(HEAD_DIM))

    @pl.when(j == 2)
    def _():
        for hh in range(2):
            sl = slice(hh * HEAD_DIM, (hh + 1) * HEAD_DIM)
            o_ref[:, sl] = norm_rope(z_ref[:, sl], kg_ref[...], 1.0)
        o_ref[:, 2 * HEAD_DIM:] = z_ref[:, 2 * HEAD_DIM:]


def _qk_norm_rope(z, cos, sin, qg, kg, *, batch, seq, tr=512):
    n = z.shape[0]
    nl = seq // tr
    return pl.pallas_call(
        _rope_kernel,
        grid=(batch, nl, 3),
        in_specs=[pl.BlockSpec((tr, 512), lambda b, i, j: (b * nl + i, 1 + j)),
                  pl.BlockSpec((tr, HEAD_DIM), lambda b, i, j: (i, 0)),
                  pl.BlockSpec((tr, HEAD_DIM), lambda b, i, j: (i, 0)),
                  pl.BlockSpec((1, HEAD_DIM), lambda b, i, j: (0, 0)),
                  pl.BlockSpec((1, HEAD_DIM), lambda b, i, j: (0, 0))],
        out_specs=pl.BlockSpec((tr, 512), lambda b, i, j: (b * nl + i, j)),
        out_shape=jax.ShapeDtypeStruct((n, 1536), BF16),
        compiler_params=_cparams(("arbitrary", "arbitrary", "arbitrary")),
        name="qk_norm_rope",
    )(z, cos, sin, qg.reshape(1, HEAD_DIM), kg.reshape(1, HEAD_DIM))


def _attn_kernel(q_ref, k_ref, v_ref, o_ref, m_scr, l_scr, acc_scr, *, nk, tk):
    ki = pl.program_id(3)

    @pl.when(ki == 0)
    def _():
        m_scr[...] = jnp.full(m_scr.shape, -jnp.inf, F32)
        l_scr[...] = jnp.zeros(l_scr.shape, F32)
        acc_scr[...] = jnp.zeros(acc_scr.shape, F32)

    k = k_ref[...]
    v = v_ref[...]
    for g in range(Q_PER_KV):
        q = q_ref[:, g * HEAD_DIM:(g + 1) * HEAD_DIM]
        s = lax.dot_general(q, k, (((1,), (1,)), ((), ())), preferred_element_type=F32)
        m_prev = m_scr[g]
        m_next = jnp.maximum(m_prev, jnp.max(s, axis=1, keepdims=True))
        p = jnp.exp2(s - jnp.concatenate([m_next] * (tk // LANES), axis=1))
        alpha = jnp.exp2(m_prev - m_next)
        l_scr[g] = alpha * l_scr[g] + jnp.sum(p, axis=1, keepdims=True)
        acc_scr[g] = alpha * acc_scr[g] + jnp.dot(p.astype(BF16), v, preferred_element_type=F32)
        m_scr[g] = m_next

    @pl.when(ki == nk - 1)
    def _():
        for g in range(Q_PER_KV):
            o_ref[:, g * HEAD_DIM:(g + 1) * HEAD_DIM] = (acc_scr[g] / l_scr[g]).astype(o_ref.dtype)


def _attention(qkv, *, batch, seq, tq=512, tk=512):
    n = qkv.shape[0]
    nq, nk = seq // tq, seq // tk
    qw = Q_PER_KV * HEAD_DIM
    kcol = N_Q_HEADS
    vcol = N_Q_HEADS + N_KV_HEADS
    return pl.pallas_call(
        functools.partial(_attn_kernel, nk=nk, tk=tk),
        grid=(batch, N_KV_HEADS, nq, nk),
        in_specs=[pl.BlockSpec((tq, qw), lambda b, h, qi, ki: (b * nq + qi, h)),
                  pl.BlockSpec((tk, HEAD_DIM), lambda b, h, qi, ki: (b * nk + ki, kcol + h)),
                  pl.BlockSpec((tk, HEAD_DIM), lambda b, h, qi, ki: (b * nk + ki, vcol + h))],
        out_specs=pl.BlockSpec((tq, qw), lambda b, h, qi, ki: (b * nq + qi, h)),
        out_shape=jax.ShapeDtypeStruct((n, N_Q_HEADS * HEAD_DIM), BF16),
        scratch_shapes=[pltpu.VMEM((Q_PER_KV, tq, HEAD_DIM), F32)] * 3,
        compiler_params=_cparams(("arbitrary", "arbitrary", "arbitrary", "arbitrary")),
        name="flash_attention",
    )(qkv, qkv, qkv)


def _mconv_kernel(prev_ref, cur_ref, next_ref, cw_ref, wq_ref, wk_ref, q_ref, k_ref, *, tl, nl):
    u = _with_halo(prev_ref, cur_ref, next_ref, nl)
    rows = tl + 2 * HALO
    acc = None
    for kk in range(M_CONV_W):
        sh = (M_CONV_W // 2 - kk) % rows
        tap = (pltpu.roll(u, sh, 0) if sh else u) * cw_ref[kk:kk + 1, :]
        acc = tap if acc is None else acc + tap
    uc = _silu(acc[HALO:HALO + tl]).astype(BF16)
    for h in range(N_M_HEADS):
        sl = slice(h * HEAD_DIM, (h + 1) * HEAD_DIM)
        q_ref[:, sl] = jnp.dot(uc[:, sl], wq_ref[h], preferred_element_type=F32).astype(q_ref.dtype)
        kh = jnp.dot(uc[:, sl], wk_ref[h], preferred_element_type=F32) * (HEAD_DIM ** -0.5)
        k_ref[:, sl] = kh.astype(k_ref.dtype)


def _mlstm_qk(z, conv_w, wq, wk, *, batch, seq, tl=512):
    n = z.shape[0]
    nl = seq // tl
    width = N_M_HEADS * HEAD_DIM
    prev, cur, nxt = _halo_specs(tl, width, 4, nl, batch)
    wspec = pl.BlockSpec((N_M_HEADS, HEAD_DIM, HEAD_DIM), lambda b, i: (0, 0, 0))
    ospec = pl.BlockSpec((tl, width), lambda b, i: (b * nl + i, 0))
    return pl.pallas_call(
        functools.partial(_mconv_kernel, tl=tl, nl=nl),
        grid=(batch, nl),
        in_specs=[prev, cur, nxt, pl.BlockSpec((M_CONV_W, width), lambda b, i: (0, 0)), wspec, wspec],
        out_specs=[ospec, ospec],
        out_shape=[jax.ShapeDtypeStruct((n, width), BF16)] * 2,
        compiler_params=_cparams(("arbitrary", "arbitrary")),
        name="mlstm_conv_qk",
    )(z, z, z, conv_w, wq.astype(BF16), wk.astype(BF16))


def _split3(x):
    x1 = x.astype(BF16)
    r = x - x1.astype(F32)
    x2 = r.astype(BF16)
    x3 = (r - x2.astype(F32)).astype(BF16)
    return x1, x2, x3


def _mscan_kernel(qf_ref, kf_ref, vf_ref, gf_ref, gtf_ref, qb_ref, kb_ref, vb_ref, gb_ref, gtb_ref,
                  bias_ref, biast_ref, hf_ref, hb_ref, s_scr, n_scr, m_scr):
    c = pl.program_id(1)

    @pl.when(c == 0)
    def _():
        s_scr[...] = jnp.zeros(s_scr.shape, F32)
        n_scr[...] = jnp.zeros(n_scr.shape, F32)
        m_scr[...] = jnp.zeros(m_scr.shape, F32)

    ch = M_CHUNK
    ti = lax.broadcasted_iota(jnp.int32, (ch, ch), 0)
    si = lax.broadcasted_iota(jnp.int32, (ch, ch), 1)
    nh = N_M_HEADS

    for rev in (False, True):
        q_ref, k_ref, v_ref, g_ref, gt_ref, h_ref = (
            (qb_ref, kb_ref, vb_ref, gb_ref, gtb_ref, hb_ref) if rev else
            (qf_ref, kf_ref, vf_ref, gf_ref, gtf_ref, hf_ref))
        causal = (si >= ti) if rev else (si <= ti)
        tri = jnp.where(causal, 1.0, 0.0).astype(BF16)
        tri_t = jnp.where((ti >= si) if rev else (ti <= si), 1.0, 0.0).astype(BF16)
        gates = g_ref[...] + bias_ref[...]
        gates_t = gt_ref[...] + biast_ref[...]
        lf = jax.nn.log_sigmoid(gates)
        lf_t = jax.nn.log_sigmoid(gates_t)
        cum = sum(jnp.dot(tri, part, preferred_element_type=F32) for part in _split3(lf))
        cum_t = sum(jnp.dot(part, tri_t, preferred_element_type=F32) for part in _split3(lf_t))
        last = 0 if rev else ch - 1
        for h in range(nh):
            idx = (nh if rev else 0) + h
            icol = (2 * nh if rev else 0) + h
            fcol = icol + nh
            sl = slice(h * HEAD_DIM, (h + 1) * HEAD_DIM)
            q = q_ref[:, sl]
            k = k_ref[:, sl]
            v = v_ref[:, sl]
            b_col = cum[:, fcol:fcol + 1]
            b_row = cum_t[fcol:fcol + 1, :]
            li_col = gates[:, icol:icol + 1]
            li_row = gates_t[icol:icol + 1, :]
            gtot = cum[last:last + 1, fcol:fcol + 1]
            m_prev = m_scr[idx][:, :1]
            n_prev = n_scr[idx]
            s_prev = s_scr[idx]

            dm = jnp.where(causal, b_col - b_row + li_row, -jnp.inf)
            m_inter = b_col + m_prev
            m_t = jnp.maximum(m_inter, jnp.max(dm, axis=1, keepdims=True))
            p = jnp.exp(dm - m_t)
            qk = lax.dot_general(q, k, (((1,), (1,)), ((), ())), preferred_element_type=F32)
            sc = qk * p
            w_inter = jnp.exp(m_inter - m_t)
            num = (jnp.dot(sc.astype(BF16), v, preferred_element_type=F32)
                   + w_inter * jnp.dot(q, s_prev.astype(BF16), preferred_element_type=F32))
            den = (jnp.sum(sc, axis=1, keepdims=True)
                   + w_inter * jnp.sum(q.astype(F32) * n_prev, axis=1, keepdims=True))
            h_ref[:, sl] = num / jnp.maximum(jnp.abs(den), jnp.exp(-m_t))

            w_st = gtot - b_col + li_col
            m_loc = jnp.max(w_st, axis=0, keepdims=True)
            ak = jnp.exp(w_st - m_loc) * k.astype(F32)
            s_c = lax.dot_general(ak.astype(BF16), v, (((0,), (0,)), ((), ())), preferred_element_type=F32)
            n_c = jnp.sum(ak, axis=0, keepdims=True)
            m_new = jnp.maximum(gtot + m_prev, m_loc)
            decay = jnp.exp(gtot + m_prev - m_new)
            add = jnp.exp(m_loc - m_new)
            s_scr[idx] = decay * s_prev + add * s_c
            n_scr[idx] = decay * n_prev + add * n_c
            m_scr[idx] = jnp.broadcast_to(m_new, (1, HEAD_DIM))


def _mlstm_scan(qm, km, z, gates, gates_t, gate_b, *, batch, seq):
    n = qm.shape[0]
    nc = seq // M_CHUNK
    width = N_M_HEADS * HEAD_DIM
    ng = 4 * N_M_HEADS
    fwd = lambda b, c: (b * nc + c, 0)
    bwd = lambda b, c: (b * nc + nc - 1 - c, 0)
    fwd_v = lambda b, c: (b * nc + c, 5)
    bwd_v = lambda b, c: (b * nc + nc - 1 - c, 5)
    fwd_t = lambda b, c: (0, b * nc + c)
    bwd_t = lambda b, c: (0, b * nc + nc - 1 - c)
    blk = (M_CHUNK, width)
    bias = jnp.zeros((1, LANES), F32).at[0, :ng].set(gate_b)
    bias_t = jnp.broadcast_to(gate_b[:, None], (ng, M_CHUNK))
    const = lambda b, c: (0, 0)
    in_specs = [
        pl.BlockSpec(blk, fwd), pl.BlockSpec(blk, fwd), pl.BlockSpec(blk, fwd_v),
        pl.BlockSpec((M_CHUNK, LANES), fwd), pl.BlockSpec((ng, M_CHUNK), fwd_t),
        pl.BlockSpec(blk, bwd), pl.BlockSpec(blk, bwd), pl.BlockSpec(blk, bwd_v),
        pl.BlockSpec((M_CHUNK, LANES), bwd), pl.BlockSpec((ng, M_CHUNK), bwd_t),
        pl.BlockSpec((1, LANES), const), pl.BlockSpec((ng, M_CHUNK), const),
    ]
    return pl.pallas_call(
        _mscan_kernel,
        grid=(batch, nc),
        in_specs=in_specs,
        out_specs=[pl.BlockSpec(blk, fwd), pl.BlockSpec(blk, bwd)],
        out_shape=[jax.ShapeDtypeStruct((n, width), F32)] * 2,
        scratch_shapes=[pltpu.VMEM((2 * N_M_HEADS, HEAD_DIM, HEAD_DIM), F32),
                        pltpu.VMEM((2 * N_M_HEADS, 1, HEAD_DIM), F32),
                        pltpu.VMEM((2 * N_M_HEADS, 1, HEAD_DIM), F32)],
        compiler_params=_cparams(("arbitrary", "arbitrary")),
        name="mlstm_scan",
    )(qm, km, z, gates, gates_t, qm, km, z, gates, gates_t, bias, bias_t)


def _mout_kernel(hf_ref, hb_ref, o_ref, g_ref, y_ref):
    for h in range(N_M_HEADS):
        sl = slice(h * HEAD_DIM, (h + 1) * HEAD_DIM)
        x = hf_ref[:, sl] + hb_ref[:, sl]
        ms = jnp.mean(x * x, axis=-1, keepdims=True)
        hn = x * lax.rsqrt(ms + EPS) * g_ref[...]
        y_ref[:, sl] = (jax.nn.sigmoid(o_ref[:, sl].astype(F32)) * hn).astype(y_ref.dtype)


def _mlstm_out(hf, hb, z, norm_g, *, tm=512):
    n, width = hf.shape
    spec = pl.BlockSpec((tm, width), lambda i: (i, 0))
    return pl.pallas_call(
        _mout_kernel,
        grid=(n // tm,),
        in_specs=[spec, spec, pl.BlockSpec((tm, width), lambda i: (i, 6)),
                  pl.BlockSpec((1, HEAD_DIM), lambda i: (0, 0))],
        out_specs=spec,
        out_shape=jax.ShapeDtypeStruct((n, width), BF16),
        compiler_params=_cparams(("arbitrary",)),
        name="mlstm_out",
    )(hf, hb, z, norm_g.reshape(1, HEAD_DIM))


def _top2(logit):
    lane = lax.broadcasted_iota(jnp.int32, logit.shape, 1).astype(F32)
    logit = jnp.where(lane < N_EXPERTS, logit, -jnp.inf)
    m1 = jnp.max(logit, axis=1, keepdims=True)
    i1 = jnp.min(jnp.where(logit == m1, lane, float(LANES)), axis=1, keepdims=True)
    rest = jnp.where(lane == i1, -jnp.inf, logit)
    m2 = jnp.max(rest, axis=1, keepdims=True)
    i2 = jnp.min(jnp.where(rest == m2, lane, float(LANES)), axis=1, keepdims=True)
    e = jnp.exp(m2 - m1)
    p1 = 1.0 / (1.0 + e)
    p2 = e / (1.0 + e)
    comb = jnp.where(lane == i1, p1, 0.0) + jnp.where(lane == i2, p2, 0.0)
    sel = jnp.where((lane == i1) | (lane == i2), 1.0, 0.0)
    return comb, sel


def _pack_halves(y):
    half = y.shape[1] // 2
    lo = pltpu.bitcast(y[:, :half].astype(BF16).astype(F32), jnp.uint32)
    hi = pltpu.bitcast(y[:, half:].astype(BF16).astype(F32), jnp.uint32)
    return hi | (lo >> 16)


def _unpack_halves(w):
    lo = pltpu.bitcast(w << 16, F32).astype(BF16)
    hi = pltpu.bitcast(w & jnp.uint32(0xFFFF0000), F32).astype(BF16)
    return lo, hi


def _norm_route_kernel(x_ref, g_ref, sc_ref, sh_ref, rw_ref, rb_ref, hp_ref, comb_ref, sel_ref):
    x = x_ref[...]
    ms = jnp.mean(x * x, axis=-1, keepdims=True)
    y = x * lax.rsqrt(ms + EPS) * g_ref[...]
    y = y * (1.0 + sc_ref[0]) + sh_ref[0]
    hp_ref[...] = _pack_halves(y)
    logit = jnp.dot(y, rw_ref[...], precision=lax.Precision.HIGHEST, preferred_element_type=F32) + rb_ref[...]
    comb_ref[...], sel_ref[...] = _top2(logit)


def _norm_route(x, g, sc, sh, router_w, router_b, *, rows_per_batch, tm=256):
    n, d = x.shape
    tpb = rows_per_batch // tm
    lspec = pl.BlockSpec((tm, LANES), lambda i: (i, 0))
    rb = jnp.zeros((1, LANES), F32).at[0, :N_EXPERTS].set(router_b)
    return pl.pallas_call(
        _norm_route_kernel,
        grid=(n // tm,),
        in_specs=[pl.BlockSpec((tm, d), lambda i: (i, 0)), pl.BlockSpec((1, d), lambda i: (0, 0)),
                  pl.BlockSpec((1, 1, d), lambda i: (i // tpb, 0, 0)), pl.BlockSpec((1, 1, d), lambda i: (i // tpb, 0, 0)),
                  pl.BlockSpec((d, LANES), lambda i: (0, 0)), pl.BlockSpec((1, LANES), lambda i: (0, 0))],
        out_specs=[pl.BlockSpec((tm, d // 2), lambda i: (i, 0)), lspec, lspec],
        out_shape=[jax.ShapeDtypeStruct((n, d // 2), jnp.uint32), jax.ShapeDtypeStruct((n, LANES), F32),
                   jax.ShapeDtypeStruct((n, LANES), F32)],
        compiler_params=_cparams(("arbitrary",)),
        name="norm_route",
    )(x, g.reshape(1, d), sc, sh, _pad_cols(router_w, LANES), rb)


def _rank_kernel(sel_ref, rank_ref, tot_ref, carry):
    @pl.when(pl.program_id(0) == 0)
    def _():
        carry[...] = jnp.zeros(carry.shape, F32)

    sel = sel_ref[...]
    tb = sel.shape[0]
    r = lax.broadcasted_iota(jnp.int32, (tb, tb), 0)
    c = lax.broadcasted_iota(jnp.int32, (tb, tb), 1)
    tri = jnp.where(r >= c, 1.0, 0.0).astype(BF16)
    incl = jnp.dot(tri, sel.astype(BF16), preferred_element_type=F32)
    rank_ref[...] = incl - sel + carry[...]
    total = carry[...] + incl[tb - 1:tb, :]
    carry[...] = total
    tot_ref[...] = total


def _rank(sel, *, tb=512):
    n = sel.shape[0]
    return pl.pallas_call(
        _rank_kernel,
        grid=(n // tb,),
        in_specs=[pl.BlockSpec((tb, LANES), lambda i: (i, 0))],
        out_specs=[pl.BlockSpec((tb, LANES), lambda i: (i, 0)), pl.BlockSpec((1, LANES), lambda i: (0, 0))],
        out_shape=[jax.ShapeDtypeStruct((n, LANES), F32), jax.ShapeDtypeStruct((1, LANES), F32)],
        scratch_shapes=[pltpu.VMEM((1, LANES), F32)],
        compiler_params=_cparams(("arbitrary",)),
        name="moe_rank",
    )(sel)


def _dest_kernel(rank_ref, sel_ref, comb_ref, off_ref, dest_ref, p_ref):
    lane = lax.broadcasted_iota(jnp.int32, rank_ref.shape, 1).astype(F32)
    sel = sel_ref[...] > 0.5
    pos = rank_ref[...] + off_ref[...]
    la = jnp.min(jnp.where(sel, lane, float(LANES)), axis=1, keepdims=True)
    lb = jnp.max(jnp.where(sel, lane, -1.0), axis=1, keepdims=True)
    pick = lambda l, v: jnp.sum(jnp.where(lane == l, v, 0.0), axis=1, keepdims=True)
    two = lambda a, b: jnp.where(lane == 0.0, a, jnp.where(lane == 1.0, b, 0.0))
    dest_ref[...] = two(pick(la, pos), pick(lb, pos)).astype(jnp.int32)
    p_ref[...] = two(pick(la, comb_ref[...]), pick(lb, comb_ref[...]))


def _dest(rank, sel, comb, off, *, tb=1024):
    n = rank.shape[0]
    spec = pl.BlockSpec((tb, LANES), lambda i: (i, 0))
    return pl.pallas_call(
        _dest_kernel,
        grid=(n // tb,),
        in_specs=[spec, spec, spec, pl.BlockSpec((1, LANES), lambda i: (0, 0))],
        out_specs=[spec, spec],
        out_shape=[jax.ShapeDtypeStruct((n, LANES), jnp.int32), jax.ShapeDtypeStruct((n, LANES), F32)],
        compiler_params=_cparams(("arbitrary",)),
        name="moe_dest",
    )(rank, sel, comb, off)


def _row_copy(src, src_row, dst, dst_row, sem):
    return pltpu.make_async_copy(src.at[pl.ds(src_row, 1), :], dst.at[pl.ds(dst_row, 1), :], sem)


def _dispatch_kernel(dest_ref, hp_ref, xs_in_ref, xs_ref, sem):
    del xs_in_ref
    tb = hp_ref.shape[0]

    def issue(r, carry):
        for s in range(2):
            _row_copy(hp_ref, r, xs_ref, dest_ref[s, r], sem).start()
        return carry

    def drain(r, carry):
        for s in range(2):
            _row_copy(hp_ref, r, xs_ref, dest_ref[s, r], sem).wait()
        return carry

    lax.fori_loop(0, tb, issue, 0)
    lax.fori_loop(0, tb, drain, 0)


def _dispatch(hp, dest3, rows, *, tb):
    n, w = hp.shape
    xs0 = jnp.zeros((rows, w), hp.dtype)
    return pl.pallas_call(
        _dispatch_kernel,
        grid=(n // tb,),
        in_specs=[pl.BlockSpec((None, 2, tb), lambda i: (i, 0, 0), memory_space=pltpu.SMEM),
                  pl.BlockSpec((tb, w), lambda i: (i, 0)),
                  pl.BlockSpec(memory_space=pl.ANY)],
        out_specs=pl.BlockSpec(memory_space=pl.ANY),
        out_shape=jax.ShapeDtypeStruct((rows, w), hp.dtype),
        scratch_shapes=[pltpu.SemaphoreType.DMA(())],
        input_output_aliases={2: 0},
        compiler_params=_cparams(("arbitrary",)),
        name="moe_dispatch",
    )(dest3, hp, xs0)


def _expert_changed(te_ref, t):
    return (t == 0) | (te_ref[t] != te_ref[jnp.maximum(t - 1, 0)])


def _moe_up_kernel(te_ref, nu_ref, xs_ref, w1_ref, w3_ref, o_ref, w1b, w3b):
    t = pl.program_id(1)

    @pl.when(_expert_changed(te_ref, t))
    def _():
        w1b[...] = w1_ref[...].astype(BF16)
        w3b[...] = w3_ref[...].astype(BF16)

    @pl.when(t < nu_ref[0])
    def _():
        lo, hi = _unpack_halves(xs_ref[...])
        half = lo.shape[1]
        a1 = (jnp.dot(lo, w1b[:half], preferred_element_type=F32)
              + jnp.dot(hi, w1b[half:], preferred_element_type=F32))
        a3 = (jnp.dot(lo, w3b[:half], preferred_element_type=F32)
              + jnp.dot(hi, w3b[half:], preferred_element_type=F32))
        o_ref[...] = (_silu(a1) * a3).astype(o_ref.dtype)

    @pl.when(t >= nu_ref[0])
    def _():
        o_ref[...] = jnp.zeros(o_ref.shape, o_ref.dtype)


def _moe_up(xs, w1, w3, te, nu, *, tg, tn):
    rows, half = xs.shape
    e, d, f = w1.shape
    wspec = pl.BlockSpec((None, d, tn), lambda j, t, te, nu: (te[t], 0, j))
    return pl.pallas_call(
        _moe_up_kernel,
        grid_spec=pltpu.PrefetchScalarGridSpec(
            num_scalar_prefetch=2,
            grid=(f // tn, rows // tg),
            in_specs=[pl.BlockSpec((tg, half), lambda j, t, te, nu: (t, 0)), wspec, wspec],
            out_specs=pl.BlockSpec((tg, tn), lambda j, t, te, nu: (t, j)),
            scratch_shapes=[pltpu.VMEM((d, tn), BF16)] * 2),
        out_shape=jax.ShapeDtypeStruct((rows, f), BF16),
        compiler_params=_cparams(("arbitrary", "arbitrary")),
        name="moe_up",
    )(te, nu, xs, w1, w3)


def _moe_down_kernel(te_ref, nu_ref, a_ref, w2_ref, o_ref, w2b):
    t = pl.program_id(1)

    @pl.when(_expert_changed(te_ref, t))
    def _():
        w2b[...] = w2_ref[...].astype(BF16)

    @pl.when(t < nu_ref[0])
    def _():
        o_ref[...] = jnp.dot(a_ref[...], w2b[...], preferred_element_type=F32)

    @pl.when(t >= nu_ref[0])
    def _():
        o_ref[...] = jnp.zeros(o_ref.shape, o_ref.dtype)


def _moe_down(a, w2, te, nu, *, tg, tn):
    rows, f = a.shape
    d = w2.shape[2]
    return pl.pallas_call(
        _moe_down_kernel,
        grid_spec=pltpu.PrefetchScalarGridSpec(
            num_scalar_prefetch=2,
            grid=(d // tn, rows // tg),
            in_specs=[pl.BlockSpec((tg, f), lambda j, t, te, nu: (t, 0)),
                      pl.BlockSpec((None, f, tn), lambda j, t, te, nu: (te[t], 0, j))],
            out_specs=pl.BlockSpec((tg, tn), lambda j, t, te, nu: (t, j)),
            scratch_shapes=[pltpu.VMEM((f, tn), BF16)]),
        out_shape=jax.ShapeDtypeStruct((rows, d), F32),
        compiler_params=_cparams(("arbitrary", "arbitrary")),
        name="moe_down",
    )(te, nu, a, w2)


def _combine_kernel(dest_ref, x_ref, g_ref, p_ref, ys_ref, o_ref, buf, sem):
    tb = x_ref.shape[0]

    def issue(r, carry):
        for s in range(2):
            _row_copy(ys_ref, dest_ref[s, r], buf.at[s], r, sem).start()
        return carry

    def drain(r, carry):
        for s in range(2):
            _row_copy(ys_ref, dest_ref[s, r], buf.at[s], r, sem).wait()
        return carry

    lax.fori_loop(0, tb, issue, 0)
    lax.fori_loop(0, tb, drain, 0)
    p = p_ref[...]
    o_ref[...] = x_ref[...] + g_ref[0] * (p[:, 0:1] * buf[0] + p[:, 1:2] * buf[1])


def _combine(x, g, p, dest3, ys, *, rows_per_batch, tb):
    n, d = x.shape
    tpb = rows_per_batch // tb
    return pl.pallas_call(
        _combine_kernel,
        grid=(n // tb,),
        in_specs=[pl.BlockSpec((None, 2, tb), lambda i: (i, 0, 0), memory_space=pltpu.SMEM),
                  pl.BlockSpec((tb, d), lambda i: (i, 0)),
                  pl.BlockSpec((1, 1, d), lambda i: (i // tpb, 0, 0)),
                  pl.BlockSpec((tb, LANES), lambda i: (i, 0)),
                  pl.BlockSpec(memory_space=pl.ANY)],
        out_specs=pl.BlockSpec((tb, d), lambda i: (i, 0)),
        out_shape=jax.ShapeDtypeStruct((n, d), F32),
        scratch_shapes=[pltpu.VMEM((2, tb, d), F32), pltpu.SemaphoreType.DMA(())],
        compiler_params=_cparams(("arbitrary",)),
        name="moe_combine",
    )(dest3, x, g, p, ys)


def _moe_ffn(x, norm_g, sc, sh, g, router_w, router_b, w1, w3, w2, *, rows_per_batch, tg=512, tb=256, tn=512):
    n, d = x.shape
    e = w1.shape[0]
    t_max = 2 * n // tg + e
    hp, comb, sel = _norm_route(x, norm_g, sc, sh, router_w, router_b, rows_per_batch=rows_per_batch)
    rank, tot = _rank(sel)
    cnt = tot[0, :e].astype(jnp.int32)
    tiles = (cnt + tg - 1) // tg
    tile_end = jnp.cumsum(tiles)
    n_used = tile_end[-1]
    off = jnp.zeros((1, LANES), F32).at[0, :e].set(((tile_end - tiles) * tg).astype(F32))
    te = jnp.sum(jnp.arange(t_max, dtype=jnp.int32)[:, None] >= tile_end[None, :], axis=1).astype(jnp.int32)
    te = jnp.minimum(te, te[jnp.maximum(n_used - 1, 0)])
    nu = n_used.reshape(1).astype(jnp.int32)
    dest, p = _dest(rank, sel, comb, off)
    dest3 = dest[:, :2].T.reshape(2, n // tb, tb).transpose(1, 0, 2)
    xs = _dispatch(hp, dest3, t_max * tg, tb=tb)
    a = _moe_up(xs, w1, w3, te, nu, tg=tg, tn=tn)
    ys = _moe_down(a, w2, te, nu, tg=tg, tn=tn)
    return _combine(x, g, p, dest3, ys, rows_per_batch=rows_per_batch, tb=tb)


def _pad_cols(w, width):
    return jnp.zeros((w.shape[0], width), w.dtype).at[:, :w.shape[1]].set(w)


def kernel(x, c, ada_w, ada_b, norm1_g, norm2_g, w_in, w_out, pool_w, pool_scale, q_norm_g, k_norm_g,
           m_conv_w, m_wq, m_wk, m_gate_b, m_norm_g, ffn_w1, ffn_w3, ffn_w2, router_w, router_b,
           moe_w1, moe_w3, moe_w2, final_norm_g):
    batch, seq, d = x.shape
    depth = ada_w.shape[0]
    n = batch * seq
    ng = 4 * N_M_HEADS
    d_main = w_in.shape[2] - ng
    xf = x.reshape(n, d)
    mod = _adaln_mod(c, ada_w, ada_b)
    cos, sin = _rope_tables(seq)
    dims = dict(batch=batch, seq=seq)

    for l in range(depth):
        sh1, sc1, g1, sh2, sc2, g2 = [mod[l, :, i * d:(i + 1) * d].reshape(batch, 1, d) for i in range(6)]
        h = _norm(xf, norm1_g[l], sc1, sh1, rows_per_batch=seq, out_dtype=BF16)
        z = _mm(h, w_in[l, :, :d_main].astype(BF16), tm=1024, tn=512, tk=d, out_dtype=BF16)
        gates = _mm(h, _pad_cols(w_in[l, :, d_main:], LANES).astype(BF16), tm=1024, tn=LANES, tk=d)
        y_pool = _pool_mixer(z, pool_w[l], pool_scale[l], **dims)
        qkv = _qk_norm_rope(z, cos, sin, q_norm_g[l], k_norm_g[l], **dims)
        y_attn = _attention(qkv, tq=1024, tk=512, **dims)
        qm, km = _mlstm_qk(z, m_conv_w[l], m_wq[l], m_wk[l], **dims)
        hf, hb = _mlstm_scan(qm, km, z, gates, gates[:, :ng].T, m_gate_b[l], **dims)
        y_m = _mlstm_out(hf, hb, z, m_norm_g[l])
        y = jnp.concatenate([y_pool, y_attn, y_m], axis=-1)
        xf = _mm(y, w_out[l].astype(BF16), tm=1024, tn=1024, tk=d, x=xf, g=g1, rows_per_batch=seq)
        i = l // 2
        if l % 2 == 0:
            h2 = _norm(xf, norm2_g[l], sc2, sh2, rows_per_batch=seq, out_dtype=BF16)
            a = _swiglu_up(h2, ffn_w1[i][None].astype(BF16), ffn_w3[i][None].astype(BF16), tm=1024, tn=512)
            xf = _mm(a, ffn_w2[i].astype(BF16), tm=1024, tn=1024, tk=512, x=xf, g=g2, rows_per_batch=seq)
        else:
            xf = _moe_ffn(xf, norm2_g[l], sc2, sh2, g2, router_w[i], router_b[i], moe_w1[i], moe_w3[i], moe_w2[i],
                          rows_per_batch=seq)

    out = _norm(xf, final_norm_g, rows_per_batch=seq, out_dtype=F32)
    return out.reshape(batch, seq, d)
```

```python
import functools
import math

import numpy as np
import jax
import jax.numpy as jnp
from jax import lax
from jax.experimental import pallas as pl
from jax.experimental.pallas import tpu as pltpu

F32 = jnp.float32
BF16 = jnp.bfloat16

EPS = 1e-6
HEAD_DIM = 128
GRID_W = 64
ROPE_THETA = 10000.0
POOL_WINDOWS = (2, 4, 8, 16)
M_CHUNK = 128
M_CONV_W = 5
N_M_HEADS = 4
N_Q_HEADS = 8
N_KV_HEADS = 2
Q_PER_KV = N_Q_HEADS // N_KV_HEADS
N_EXPERTS = 8
LANES = 128
HALO = 16
VMEM_LIMIT = 56 * 1024 * 1024
LOG2E = math.log2(math.e)


def _cparams(sem):
    return pltpu.CompilerParams(dimension_semantics=sem, vmem_limit_bytes=VMEM_LIMIT)


def _silu(a):
    return a * jax.nn.sigmoid(a)


def _mod_kernel(c_ref, w_ref, b_ref, o_ref):
    ca = _silu(c_ref[...])
    o_ref[0] = jnp.dot(ca.astype(BF16), w_ref[0].astype(BF16), preferred_element_type=F32) + b_ref[0]


def _adaln_mod(c, ada_w, ada_b, tn=768):
    depth, d, n6 = ada_w.shape
    b = c.shape[0]
    cp = jnp.zeros((8, d), F32).at[:b].set(c)
    out = pl.pallas_call(
        _mod_kernel,
        grid=(depth, n6 // tn),
        in_specs=[
            pl.BlockSpec((8, d), lambda l, j: (0, 0)),
            pl.BlockSpec((1, d, tn), lambda l, j: (l, 0, j)),
            pl.BlockSpec((1, 1, tn), lambda l, j: (l, 0, j)),
        ],
        out_specs=pl.BlockSpec((1, 8, tn), lambda l, j: (l, 0, j)),
        out_shape=jax.ShapeDtypeStruct((depth, 8, n6), F32),
        compiler_params=_cparams(("arbitrary", "arbitrary")),
        name="adaln_mod",
    )(cp, ada_w, ada_b.reshape(depth, 1, n6))
    return out[:, :b]


def _norm_kernel(*refs, modulate):
    if modulate:
        x_ref, g_ref, sc_ref, sh_ref, o_ref = refs
    else:
        x_ref, g_ref, o_ref = refs
    x = x_ref[...]
    ms = jnp.mean(x * x, axis=-1, keepdims=True)
    y = x * lax.rsqrt(ms + EPS) * g_ref[...]
    if modulate:
        y = y * (1.0 + sc_ref[0]) + sh_ref[0]
    o_ref[...] = y.astype(o_ref.dtype)


def _norm(x, g, sc=None, sh=None, *, rows_per_batch, out_dtype, tm=256):
    n, d = x.shape
    tpb = rows_per_batch // tm
    modulate = sc is not None
    in_specs = [pl.BlockSpec((tm, d), lambda i: (i, 0)), pl.BlockSpec((1, d), lambda i: (0, 0))]
    args = [x, g.reshape(1, d)]
    if modulate:
        in_specs += [pl.BlockSpec((1, 1, d), lambda i: (i // tpb, 0, 0))] * 2
        args += [sc, sh]
    return pl.pallas_call(
        functools.partial(_norm_kernel, modulate=modulate),
        grid=(n // tm,),
        in_specs=in_specs,
        out_specs=pl.BlockSpec((tm, d), lambda i: (i, 0)),
        out_shape=jax.ShapeDtypeStruct((n, d), out_dtype),
        compiler_params=_cparams(("arbitrary",)),
        name="rms_norm",
    )(*args)


def _mm_kernel(*refs, n_a, resid, valid_cols):
    a_refs, w_ref = refs[:n_a], refs[n_a]
    if resid:
        x_ref, g_ref, o_ref, wb = refs[n_a + 1:]
    else:
        o_ref, wb = refs[n_a + 1:]

    @pl.when(pl.program_id(1) == 0)
    def _():
        w = w_ref[...]
        if valid_cols is not None:
            w = jnp.where(lax.broadcasted_iota(jnp.int32, w.shape, 1) < valid_cols, w, 0.0)
        wb[...] = w.astype(BF16)

    acc, k0 = None, 0
    for a_ref in a_refs:
        kk = a_ref.shape[1]
        part = jnp.dot(a_ref[...], wb[k0:k0 + kk, :], preferred_element_type=F32)
        acc = part if acc is None else acc + part
        k0 += kk
    if resid:
        o_ref[...] = x_ref[...] + g_ref[0] * acc
    else:
        o_ref[...] = acc.astype(o_ref.dtype)


def _mm(a_list, w, w_idx, *, tm, tn, ncols=None, col_blk0=0, valid_cols=None, out_dtype=F32,
        x=None, g=None, rows_per_batch=None):
    m = a_list[0].shape[0]
    kdim = w.shape[-2]
    ncols = w.shape[-1] if ncols is None else ncols
    resid = x is not None
    lead = (None,) * len(w_idx)
    in_specs = [pl.BlockSpec((tm, a.shape[1]), lambda j, i: (i, 0)) for a in a_list]
    in_specs.append(pl.BlockSpec(lead + (kdim, tn), lambda j, i: tuple(w_idx) + (0, j + col_blk0)))
    args = list(a_list) + [w]
    if resid:
        tpb = rows_per_batch // tm
        in_specs += [pl.BlockSpec((tm, tn), lambda j, i: (i, j)),
                     pl.BlockSpec((1, 1, tn), lambda j, i: (i // tpb, 0, j))]
        args += [x, g]
    return pl.pallas_call(
        functools.partial(_mm_kernel, n_a=len(a_list), resid=resid, valid_cols=valid_cols),
        grid=(ncols // tn, m // tm),
        in_specs=in_specs,
        out_specs=pl.BlockSpec((tm, tn), lambda j, i: (i, j)),
        out_shape=jax.ShapeDtypeStruct((m, ncols), out_dtype),
        scratch_shapes=[pltpu.VMEM((kdim, tn), BF16)],
        compiler_params=_cparams(("arbitrary", "arbitrary")),
        name="matmul_resid" if resid else "matmul",
    )(*args)


def _up_kernel(h_ref, w1_ref, w3_ref, o_ref, w1b, w3b):
    @pl.when(pl.program_id(1) == 0)
    def _():
        w1b[...] = w1_ref[...].astype(BF16)
        w3b[...] = w3_ref[...].astype(BF16)

    h = h_ref[...]
    a1 = jnp.dot(h, w1b[...], preferred_element_type=F32)
    a3 = jnp.dot(h, w3b[...], preferred_element_type=F32)
    o_ref[...] = (_silu(a1) * a3).astype(o_ref.dtype)


def _swiglu_up(h, w1, w3, w_idx, *, tm, tn):
    n, d = h.shape
    f = w1.shape[-1]
    lead = (None,) * len(w_idx)
    wspec = pl.BlockSpec(lead + (d, tn), lambda j, i: tuple(w_idx) + (0, j))
    return pl.pallas_call(
        _up_kernel,
        grid=(f // tn, n // tm),
        in_specs=[pl.BlockSpec((tm, d), lambda j, i: (i, 0)), wspec, wspec],
        out_specs=pl.BlockSpec((tm, tn), lambda j, i: (i, j)),
        out_shape=jax.ShapeDtypeStruct((n, f), BF16),
        scratch_shapes=[pltpu.VMEM((d, tn), BF16)] * 2,
        compiler_params=_cparams(("arbitrary", "arbitrary")),
        name="swiglu_up",
    )(h, w1, w3)


def _halo_specs(tl, width, col_block, nl, batch):
    hb = tl // HALO
    nh = nl * hb
    last = batch * nh - 1
    cur = pl.BlockSpec((tl, width), lambda b, i: (b * nl + i, col_block))
    prev = pl.BlockSpec((HALO, width), lambda b, i: (jnp.maximum(b * nh + i * hb - 1, 0), col_block))
    nxt = pl.BlockSpec((HALO, width), lambda b, i: (jnp.minimum(b * nh + (i + 1) * hb, last), col_block))
    return prev, cur, nxt


def _with_halo(prev_ref, cur_ref, next_ref, nl):
    i = pl.program_id(1)
    prev = jnp.where(i == 0, 0.0, prev_ref[...].astype(F32))
    nxt = jnp.where(i == nl - 1, 0.0, next_ref[...].astype(F32))
    return jnp.concatenate([prev, cur_ref[...].astype(F32), nxt], axis=0)


def _pool_kernel(prev_ref, cur_ref, next_ref, w_ref, s_ref, o_ref, *, tl, nl, seq):
    u = _with_halo(prev_ref, cur_ref, next_ref, nl)
    rows = tl + 2 * HALO
    t = (pl.program_id(1) * tl + lax.broadcasted_iota(jnp.int32, (tl, LANES), 0)).astype(F32)
    for g, w in enumerate(POOL_WINDOWS):
        half = w // 2
        ug = u[:, g * LANES:(g + 1) * LANES]
        s = ug + pltpu.roll(ug, 1, 0)
        sh = 1
        while sh < half:
            s = pltpu.roll(s, sh, 0) + pltpu.roll(s, rows - sh, 0)
            sh *= 2
        cnt = jnp.minimum(t + (half - 1), seq - 1.0) - jnp.maximum(t - half, 0.0) + 1.0
        mean = s[HALO:HALO + tl] / cnt
        diff = mean - ug[HALO:HALO + tl]
        y = jnp.dot(diff.astype(BF16), w_ref[g], preferred_element_type=F32)
        o_ref[:, g * LANES:(g + 1) * LANES] = (y * s_ref[:, g * LANES:(g + 1) * LANES]).astype(o_ref.dtype)


def _pool_mixer(z, w_pool, scale, *, batch, seq, tl=512):
    n = z.shape[0]
    nl = seq // tl
    width = len(POOL_WINDOWS) * LANES
    prev, cur, nxt = _halo_specs(tl, width, 0, nl, batch)
    return pl.pallas_call(
        functools.partial(_pool_kernel, tl=tl, nl=nl, seq=seq),
        grid=(batch, nl),
        in_specs=[prev, cur, nxt,
                  pl.BlockSpec((len(POOL_WINDOWS), LANES, LANES), lambda b, i: (0, 0, 0)),
                  pl.BlockSpec((1, width), lambda b, i: (0, 0))],
        out_specs=pl.BlockSpec((tl, width), lambda b, i: (b * nl + i, 0)),
        out_shape=jax.ShapeDtypeStruct((n, width), BF16),
        compiler_params=_cparams(("arbitrary", "arbitrary")),
        name="pool_mixer",
    )(z, z, z, w_pool.astype(BF16), scale.reshape(1, width))


def _rope_tables(seq):
    rows = seq // GRID_W
    row = jnp.broadcast_to(jnp.arange(rows, dtype=F32)[:, None], (rows, GRID_W)).reshape(seq)
    col = jnp.broadcast_to(jnp.arange(GRID_W, dtype=F32)[None, :], (rows, GRID_W)).reshape(seq)
    half = HEAD_DIM // 4
    inv = ROPE_THETA ** (-jnp.arange(half, dtype=F32) / half)
    ar = row[:, None] * inv[None, :]
    ac = col[:, None] * inv[None, :]
    cos = jnp.concatenate([jnp.cos(ar), jnp.cos(ar), jnp.cos(ac), jnp.cos(ac)], axis=-1)
    sin = jnp.concatenate([-jnp.sin(ar), jnp.sin(ar), -jnp.sin(ac), jnp.sin(ac)], axis=-1)
    return cos, sin


def _rope_kernel(z_ref, cos_ref, sin_ref, qg_ref, kg_ref, o_ref):
    j = pl.program_id(2)
    cos = cos_ref[...]
    sin = sin_ref[...]
    lane = lax.broadcasted_iota(jnp.int32, cos.shape, 1)
    first = (lane % (HEAD_DIM // 2)) < (HEAD_DIM // 4)

    def norm_rope(xh, g, scale):
        xf = xh.astype(F32)
        ms = jnp.mean(xf * xf, axis=-1, keepdims=True)
        y = xf * lax.rsqrt(ms + EPS) * g
        rot = jnp.where(first, pltpu.roll(y, HEAD_DIM - HEAD_DIM // 4, 1), pltpu.roll(y, HEAD_DIM // 4, 1))
        return ((y * cos + rot * sin) * scale).astype(o_ref.dtype)

    @pl.when(j < 2)
    def _():
        for hh in range(4):
            sl = slice(hh * HEAD_DIM, (hh + 1) * HEAD_DIM)
            o_ref[:, sl] = norm_rope(z_ref[:, sl], qg_ref[...], LOG2E / math.sqrt(HEAD_DIM))

    @pl.when(j == 2)
    def _():
        for hh in range(2):
            sl = slice(hh * HEAD_DIM, (hh + 1) * HEAD_DIM)
            o_ref[:, sl] = norm_rope(z_ref[:, sl], kg_ref[...], 1.0)
        o_ref[:, 2 * HEAD_DIM:] = z_ref[:, 2 * HEAD_DIM:]


def _qk_norm_rope(z, cos, sin, qg, kg, *, batch, seq, tr=512):
    n = z.shape[0]
    nl = seq // tr
    return pl.pallas_call(
        _rope_kernel,
        grid=(batch, nl, 3),
        in_specs=[pl.BlockSpec((tr, 512), lambda b, i, j: (b * nl + i, 1 + j)),
                  pl.BlockSpec((tr, HEAD_DIM), lambda b, i, j: (i, 0)),
                  pl.BlockSpec((tr, HEAD_DIM), lambda b, i, j: (i, 0)),
                  pl.BlockSpec((1, HEAD_DIM), lambda b, i, j: (0, 0)),
                  pl.BlockSpec((1, HEAD_DIM), lambda b, i, j: (0, 0))],
        out_specs=pl.BlockSpec((tr, 512), lambda b, i, j: (b * nl + i, j)),
        out_shape=jax.ShapeDtypeStruct((n, 1536), BF16),
        compiler_params=_cparams(("arbitrary", "arbitrary", "arbitrary")),
        name="qk_norm_rope",
    )(z, cos, sin, qg.reshape(1, HEAD_DIM), kg.reshape(1, HEAD_DIM))


def _attn_kernel(q_ref, k_ref, v_ref, o_ref, m_scr, l_scr, acc_scr, *, nk, tk):
    ki = pl.program_id(3)

    @pl.when(ki == 0)
    def _():
        m_scr[...] = jnp.full(m_scr.shape, -jnp.inf, F32)
        l_scr[...] = jnp.zeros(l_scr.shape, F32)
        acc_scr[...] = jnp.zeros(acc_scr.shape, F32)

    k = k_ref[...]
    v = v_ref[...]
    for g in range(Q_PER_KV):
        q = q_ref[:, g * HEAD_DIM:(g + 1) * HEAD_DIM]
        s = lax.dot_general(q, k, (((1,), (1,)), ((), ())), preferred_element_type=F32)
        m_prev = m_scr[g]
        m_next = jnp.maximum(m_prev, jnp.max(s, axis=1, keepdims=True))
        p = jnp.exp2(s - jnp.concatenate([m_next] * (tk // LANES), axis=1))
        alpha = jnp.exp2(m_prev - m_next)
        l_scr[g] = alpha * l_scr[g] + jnp.sum(p, axis=1, keepdims=True)
        acc_scr[g] = alpha * acc_scr[g] + jnp.dot(p.astype(BF16), v, preferred_element_type=F32)
        m_scr[g] = m_next

    @pl.when(ki == nk - 1)
    def _():
        for g in range(Q_PER_KV):
            o_ref[:, g * HEAD_DIM:(g + 1) * HEAD_DIM] = (acc_scr[g] / l_scr[g]).astype(o_ref.dtype)


def _attention(qkv, *, batch, seq, tq=512, tk=512):
    n = qkv.shape[0]
    nq, nk = seq // tq, seq // tk
    qw = Q_PER_KV * HEAD_DIM
    kcol = N_Q_HEADS
    vcol = N_Q_HEADS + N_KV_HEADS
    return pl.pallas_call(
        functools.partial(_attn_kernel, nk=nk, tk=tk),
        grid=(batch, N_KV_HEADS, nq, nk),
        in_specs=[pl.BlockSpec((tq, qw), lambda b, h, qi, ki: (b * nq + qi, h)),
                  pl.BlockSpec((tk, HEAD_DIM), lambda b, h, qi, ki: (b * nk + ki, kcol + h)),
                  pl.BlockSpec((tk, HEAD_DIM), lambda b, h, qi, ki: (b * nk + ki, vcol + h))],
        out_specs=pl.BlockSpec((tq, qw), lambda b, h, qi, ki: (b * nq + qi, h)),
        out_shape=jax.ShapeDtypeStruct((n, N_Q_HEADS * HEAD_DIM), BF16),
        scratch_shapes=[pltpu.VMEM((Q_PER_KV, tq, HEAD_DIM), F32)] * 3,
        compiler_params=_cparams(("arbitrary", "arbitrary", "arbitrary", "arbitrary")),
        name="flash_attention",
    )(qkv, qkv, qkv)


def _mconv_kernel(prev_ref, cur_ref, next_ref, cw_ref, wq_ref, wk_ref, q_ref, k_ref, *, tl, nl):
    u = _with_halo(prev_ref, cur_ref, next_ref, nl)
    rows = tl + 2 * HALO
    acc = None
    for kk in range(M_CONV_W):
        sh = (M_CONV_W // 2 - kk) % rows
        tap = (pltpu.roll(u, sh, 0) if sh else u) * cw_ref[kk:kk + 1, :]
        acc = tap if acc is None else acc + tap
    uc = _silu(acc[HALO:HALO + tl]).astype(BF16)
    for h in range(N_M_HEADS):
        sl = slice(h * HEAD_DIM, (h + 1) * HEAD_DIM)
        q_ref[:, sl] = jnp.dot(uc[:, sl], wq_ref[h], preferred_element_type=F32).astype(q_ref.dtype)
        kh = jnp.dot(uc[:, sl], wk_ref[h], preferred_element_type=F32) * (HEAD_DIM ** -0.5)
        k_ref[:, sl] = kh.astype(k_ref.dtype)


def _mlstm_qk(z, conv_w, wq, wk, *, batch, seq, tl=512):
    n = z.shape[0]
    nl = seq // tl
    width = N_M_HEADS * HEAD_DIM
    prev, cur, nxt = _halo_specs(tl, width, 4, nl, batch)
    wspec = pl.BlockSpec((N_M_HEADS, HEAD_DIM, HEAD_DIM), lambda b, i: (0, 0, 0))
    ospec = pl.BlockSpec((tl, width), lambda b, i: (b * nl + i, 0))
    return pl.pallas_call(
        functools.partial(_mconv_kernel, tl=tl, nl=nl),
        grid=(batch, nl),
        in_specs=[prev, cur, nxt, pl.BlockSpec((M_CONV_W, width), lambda b, i: (0, 0)), wspec, wspec],
        out_specs=[ospec, ospec],
        out_shape=[jax.ShapeDtypeStruct((n, width), BF16)] * 2,
        compiler_params=_cparams(("arbitrary", "arbitrary")),
        name="mlstm_conv_qk",
    )(z, z, z, conv_w, wq.astype(BF16), wk.astype(BF16))


def _split3(x):
    x1 = x.astype(BF16)
    r = x - x1.astype(F32)
    x2 = r.astype(BF16)
    x3 = (r - x2.astype(F32)).astype(BF16)
    return x1, x2, x3


def _mscan_kernel(qf_ref, kf_ref, vf_ref, gf_ref, gtf_ref, qb_ref, kb_ref, vb_ref, gb_ref, gtb_ref,
                  bias_ref, biast_ref, hf_ref, hb_ref, s_scr, n_scr, m_scr):
    c = pl.program_id(1)

    @pl.when(c == 0)
    def _():
        s_scr[...] = jnp.zeros(s_scr.shape, F32)
        n_scr[...] = jnp.zeros(n_scr.shape, F32)
        m_scr[...] = jnp.zeros(m_scr.shape, F32)

    ch = M_CHUNK
    ti = lax.broadcasted_iota(jnp.int32, (ch, ch), 0)
    si = lax.broadcasted_iota(jnp.int32, (ch, ch), 1)
    nh = N_M_HEADS

    for rev in (False, True):
        q_ref, k_ref, v_ref, g_ref, gt_ref, h_ref = (
            (qb_ref, kb_ref, vb_ref, gb_ref, gtb_ref, hb_ref) if rev else
            (qf_ref, kf_ref, vf_ref, gf_ref, gtf_ref, hf_ref))
        causal = (si >= ti) if rev else (si <= ti)
        tri = jnp.where(causal, 1.0, 0.0).astype(BF16)
        tri_t = jnp.where((ti >= si) if rev else (ti <= si), 1.0, 0.0).astype(BF16)
        gates = g_ref[...] + bias_ref[...]
        gates_t = gt_ref[...] + biast_ref[...]
        lf = jax.nn.log_sigmoid(gates)
        lf_t = jax.nn.log_sigmoid(gates_t)
        cum = sum(jnp.dot(tri, part, preferred_element_type=F32) for part in _split3(lf))
        cum_t = sum(jnp.dot(part, tri_t, preferred_element_type=F32) for part in _split3(lf_t))
        last = 0 if rev else ch - 1
        for h in range(nh):
            idx = (nh if rev else 0) + h
            icol = (2 * nh if rev else 0) + h
            fcol = icol + nh
            sl = slice(h * HEAD_DIM, (h + 1) * HEAD_DIM)
            q = q_ref[:, sl]
            k = k_ref[:, sl]
            v = v_ref[:, sl]
            b_col = cum[:, fcol:fcol + 1]
            b_row = cum_t[fcol:fcol + 1, :]
            li_col = gates[:, icol:icol + 1]
            li_row = gates_t[icol:icol + 1, :]
            gtot = cum[last:last + 1, fcol:fcol + 1]
            m_prev = m_scr[idx][:, :1]
            n_prev = n_scr[idx]
            s_prev = s_scr[idx]

            dm = jnp.where(causal, b_col - b_row + li_row, -jnp.inf)
            m_inter = b_col + m_prev
            m_t = jnp.maximum(m_inter, jnp.max(dm, axis=1, keepdims=True))
            p = jnp.exp(dm - m_t)
            qk = lax.dot_general(q, k, (((1,), (1,)), ((), ())), preferred_element_type=F32)
            sc = qk * p
            w_inter = jnp.exp(m_inter - m_t)
            num = (jnp.dot(sc.astype(BF16), v, preferred_element_type=F32)
                   + w_inter * jnp.dot(q, s_prev.astype(BF16), preferred_element_type=F32))
            den = (jnp.sum(sc, axis=1, keepdims=True)
                   + w_inter * jnp.sum(q.astype(F32) * n_prev, axis=1, keepdims=True))
            h_ref[:, sl] = num / jnp.maximum(jnp.abs(den), jnp.exp(-m_t))

            w_st = gtot - b_col + li_col
            m_loc = jnp.max(w_st, axis=0, keepdims=True)
            ak = jnp.exp(w_st - m_loc) * k.astype(F32)
            s_c = lax.dot_general(ak.astype(BF16), v, (((0,), (0,)), ((), ())), preferred_element_type=F32)
            n_c = jnp.sum(ak, axis=0, keepdims=True)
            m_new = jnp.maximum(gtot + m_prev, m_loc)
            decay = jnp.exp(gtot + m_prev - m_new)
            add = jnp.exp(m_loc - m_new)
            s_scr[idx] = decay * s_prev + add * s_c
            n_scr[idx] = decay * n_prev + add * n_c
            m_scr[idx] = jnp.broadcast_to(m_new, (1, HEAD_DIM))


def _mlstm_scan(qm, km, z, gates, gates_t, gate_b, *, batch, seq):
    n = qm.shape[0]
    nc = seq // M_CHUNK
    width = N_M_HEADS * HEAD_DIM
    ng = 4 * N_M_HEADS
    fwd = lambda b, c: (b * nc + c, 0)
    bwd = lambda b, c: (b * nc + nc - 1 - c, 0)
    fwd_v = lambda b, c: (b * nc + c, 5)
    bwd_v = lambda b, c: (b * nc + nc - 1 - c, 5)
    fwd_t = lambda b, c: (0, b * nc + c)
    bwd_t = lambda b, c: (0, b * nc + nc - 1 - c)
    blk = (M_CHUNK, width)
    bias = jnp.zeros((1, LANES), F32).at[0, :ng].set(gate_b)
    bias_t = jnp.broadcast_to(gate_b[:, None], (ng, M_CHUNK))
    const = lambda b, c: (0, 0)
    in_specs = [
        pl.BlockSpec(blk, fwd), pl.BlockSpec(blk, fwd), pl.BlockSpec(blk, fwd_v),
        pl.BlockSpec((M_CHUNK, LANES), fwd), pl.BlockSpec((ng, M_CHUNK), fwd_t),
        pl.BlockSpec(blk, bwd), pl.BlockSpec(blk, bwd), pl.BlockSpec(blk, bwd_v),
        pl.BlockSpec((M_CHUNK, LANES), bwd), pl.BlockSpec((ng, M_CHUNK), bwd_t),
        pl.BlockSpec((1, LANES), const), pl.BlockSpec((ng, M_CHUNK), const),
    ]
    return pl.pallas_call(
        _mscan_kernel,
        grid=(batch, nc),
        in_specs=in_specs,
        out_specs=[pl.BlockSpec(blk, fwd), pl.BlockSpec(blk, bwd)],
        out_shape=[jax.ShapeDtypeStruct((n, width), F32)] * 2,
        scratch_shapes=[pltpu.VMEM((2 * N_M_HEADS, HEAD_DIM, HEAD_DIM), F32),
                        pltpu.VMEM((2 * N_M_HEADS, 1, HEAD_DIM), F32),
                        pltpu.VMEM((2 * N_M_HEADS, 1, HEAD_DIM), F32)],
        compiler_params=_cparams(("arbitrary", "arbitrary")),
        name="mlstm_scan",
    )(qm, km, z, gates, gates_t, qm, km, z, gates, gates_t, bias, bias_t)


def _mout_kernel(hf_ref, hb_ref, o_ref, g_ref, y_ref):
    for h in range(N_M_HEADS):
        sl = slice(h * HEAD_DIM, (h + 1) * HEAD_DIM)
        x = hf_ref[:, sl] + hb_ref[:, sl]
        ms = jnp.mean(x * x, axis=-1, keepdims=True)
        hn = x * lax.rsqrt(ms + EPS) * g_ref[...]
        y_ref[:, sl] = (jax.nn.sigmoid(o_ref[:, sl].astype(F32)) * hn).astype(y_ref.dtype)


def _mlstm_out(hf, hb, z, norm_g, *, tm=512):
    n, width = hf.shape
    spec = pl.BlockSpec((tm, width), lambda i: (i, 0))
    return pl.pallas_call(
        _mout_kernel,
        grid=(n // tm,),
        in_specs=[spec, spec, pl.BlockSpec((tm, width), lambda i: (i, 6)),
                  pl.BlockSpec((1, HEAD_DIM), lambda i: (0, 0))],
        out_specs=spec,
        out_shape=jax.ShapeDtypeStruct((n, width), BF16),
        compiler_params=_cparams(("arbitrary",)),
        name="mlstm_out",
    )(hf, hb, z, norm_g.reshape(1, HEAD_DIM))


def _top2(logit):
    lane = lax.broadcasted_iota(jnp.int32, logit.shape, 1).astype(F32)
    logit = jnp.where(lane < N_EXPERTS, logit, -jnp.inf)
    m1 = jnp.max(logit, axis=1, keepdims=True)
    i1 = jnp.min(jnp.where(logit == m1, lane, float(LANES)), axis=1, keepdims=True)
    rest = jnp.where(lane == i1, -jnp.inf, logit)
    m2 = jnp.max(rest, axis=1, keepdims=True)
    i2 = jnp.min(jnp.where(rest == m2, lane, float(LANES)), axis=1, keepdims=True)
    e = jnp.exp(m2 - m1)
    p1 = 1.0 / (1.0 + e)
    p2 = e / (1.0 + e)
    comb = jnp.where(lane == i1, p1, 0.0) + jnp.where(lane == i2, p2, 0.0)
    sel = jnp.where((lane == i1) | (lane == i2), 1.0, 0.0)
    return comb, sel


def _pack_halves(y):
    half = y.shape[1] // 2
    lo = pltpu.bitcast(y[:, :half].astype(BF16).astype(F32), jnp.uint32)
    hi = pltpu.bitcast(y[:, half:].astype(BF16).astype(F32), jnp.uint32)
    return hi | (lo >> 16)


def _unpack_halves(w):
    lo = pltpu.bitcast(w << 16, F32).astype(BF16)
    hi = pltpu.bitcast(w & jnp.uint32(0xFFFF0000), F32).astype(BF16)
    return lo, hi


def _norm_route_kernel(x_ref, g_ref, sc_ref, sh_ref, rw_ref, rb_ref, hp_ref, comb_ref, sel_ref):
    x = x_ref[...]
    ms = jnp.mean(x * x, axis=-1, keepdims=True)
    y = x * lax.rsqrt(ms + EPS) * g_ref[...]
    y = y * (1.0 + sc_ref[0]) + sh_ref[0]
    hp_ref[...] = _pack_halves(y)
    logit = jnp.dot(y, rw_ref[...], precision=lax.Precision.HIGHEST, preferred_element_type=F32) + rb_ref[...]
    comb_ref[...], sel_ref[...] = _top2(logit)


def _norm_route(x, g, sc, sh, router_w, router_b, *, rows_per_batch, tm=256):
    n, d = x.shape
    tpb = rows_per_batch // tm
    lspec = pl.BlockSpec((tm, LANES), lambda i: (i, 0))
    rb = jnp.zeros((1, LANES), F32).at[0, :N_EXPERTS].set(router_b)
    return pl.pallas_call(
        _norm_route_kernel,
        grid=(n // tm,),
        in_specs=[pl.BlockSpec((tm, d), lambda i: (i, 0)), pl.BlockSpec((1, d), lambda i: (0, 0)),
                  pl.BlockSpec((1, 1, d), lambda i: (i // tpb, 0, 0)), pl.BlockSpec((1, 1, d), lambda i: (i // tpb, 0, 0)),
                  pl.BlockSpec((d, LANES), lambda i: (0, 0)), pl.BlockSpec((1, LANES), lambda i: (0, 0))],
        out_specs=[pl.BlockSpec((tm, d // 2), lambda i: (i, 0)), lspec, lspec],
        out_shape=[jax.ShapeDtypeStruct((n, d // 2), jnp.uint32), jax.ShapeDtypeStruct((n, LANES), F32),
                   jax.ShapeDtypeStruct((n, LANES), F32)],
        compiler_params=_cparams(("arbitrary",)),
        name="norm_route",
    )(x, g.reshape(1, d), sc, sh, _pad_cols(router_w, LANES), rb)


def _rank_kernel(sel_ref, rank_ref, tot_ref, carry):
    @pl.when(pl.program_id(0) == 0)
    def _():
        carry[...] = jnp.zeros(carry.shape, F32)

    sel = sel_ref[...]
    tb = sel.shape[0]
    r = lax.broadcasted_iota(jnp.int32, (tb, tb), 0)
    c = lax.broadcasted_iota(jnp.int32, (tb, tb), 1)
    tri = jnp.where(r >= c, 1.0, 0.0).astype(BF16)
    incl = jnp.dot(tri, sel.astype(BF16), preferred_element_type=F32)
    rank_ref[...] = incl - sel + carry[...]
    total = carry[...] + incl[tb - 1:tb, :]
    carry[...] = total
    tot_ref[...] = total


def _rank(sel, *, tb=512):
    n = sel.shape[0]
    return pl.pallas_call(
        _rank_kernel,
        grid=(n // tb,),
        in_specs=[pl.BlockSpec((tb, LANES), lambda i: (i, 0))],
        out_specs=[pl.BlockSpec((tb, LANES), lambda i: (i, 0)), pl.BlockSpec((1, LANES), lambda i: (0, 0))],
        out_shape=[jax.ShapeDtypeStruct((n, LANES), F32), jax.ShapeDtypeStruct((1, LANES), F32)],
        scratch_shapes=[pltpu.VMEM((1, LANES), F32)],
        compiler_params=_cparams(("arbitrary",)),
        name="moe_rank",
    )(sel)


def _dest_kernel(rank_ref, sel_ref, comb_ref, off_ref, dest_ref, p_ref):
    lane = lax.broadcasted_iota(jnp.int32, rank_ref.shape, 1).astype(F32)
    sel = sel_ref[...] > 0.5
    pos = rank_ref[...] + off_ref[...]
    la = jnp.min(jnp.where(sel, lane, float(LANES)), axis=1, keepdims=True)
    lb = jnp.max(jnp.where(sel, lane, -1.0), axis=1, keepdims=True)
    pick = lambda l, v: jnp.sum(jnp.where(lane == l, v, 0.0), axis=1, keepdims=True)
    two = lambda a, b: jnp.where(lane == 0.0, a, jnp.where(lane == 1.0, b, 0.0))
    dest_ref[...] = two(pick(la, pos), pick(lb, pos)).astype(jnp.int32)
    p_ref[...] = two(pick(la, comb_ref[...]), pick(lb, comb_ref[...]))


def _dest(rank, sel, comb, off, *, tb=1024):
    n = rank.shape[0]
    spec = pl.BlockSpec((tb, LANES), lambda i: (i, 0))
    return pl.pallas_call(
        _dest_kernel,
        grid=(n // tb,),
        in_specs=[spec, spec, spec, pl.BlockSpec((1, LANES), lambda i: (0, 0))],
        out_specs=[spec, spec],
        out_shape=[jax.ShapeDtypeStruct((n, LANES), jnp.int32), jax.ShapeDtypeStruct((n, LANES), F32)],
        compiler_params=_cparams(("arbitrary",)),
        name="moe_dest",
    )(rank, sel, comb, off)


def _row_copy(src, src_row, dst, dst_row, sem):
    return pltpu.make_async_copy(src.at[pl.ds(src_row, 1), :], dst.at[pl.ds(dst_row, 1), :], sem)


def _dispatch_kernel(dest_ref, hp_ref, xs_in_ref, xs_ref, sem, *, tb, steps):
    del xs_in_ref
    i = pl.program_id(0)

    def issue(r, carry):
        for s in range(2):
            _row_copy(hp_ref, i * tb + r, xs_ref, dest_ref[s, r], sem).start()
        return carry

    def drain(r, carry):
        for s in range(2):
            _row_copy(hp_ref, 0, xs_ref, 0, sem).wait()
        return carry

    lax.fori_loop(0, tb, issue, 0)

    @pl.when(i > 0)
    def _():
        lax.fori_loop(0, tb, drain, 0)

    @pl.when(i == steps - 1)
    def _():
        lax.fori_loop(0, tb, drain, 0)


def _dispatch(hp, dest3, rows, *, tb):
    n, w = hp.shape
    xs0 = jnp.zeros((rows, w), hp.dtype)
    return pl.pallas_call(
        functools.partial(_dispatch_kernel, tb=tb, steps=n // tb),
        grid=(n // tb,),
        in_specs=[pl.BlockSpec((None, 2, tb), lambda i: (i, 0, 0), memory_space=pltpu.SMEM),
                  pl.BlockSpec(memory_space=pl.ANY),
                  pl.BlockSpec(memory_space=pl.ANY)],
        out_specs=pl.BlockSpec(memory_space=pl.ANY),
        out_shape=jax.ShapeDtypeStruct((rows, w), hp.dtype),
        scratch_shapes=[pltpu.SemaphoreType.DMA(())],
        input_output_aliases={2: 0},
        compiler_params=_cparams(("arbitrary",)),
        name="moe_dispatch",
    )(dest3, hp, xs0)


def _expert_changed(te_ref, t):
    return (t == 0) | (te_ref[t] != te_ref[jnp.maximum(t - 1, 0)])


def _moe_up_kernel(te_ref, nu_ref, xs_ref, w1_ref, w3_ref, o_ref, w1b, w3b):
    t = pl.program_id(1)

    @pl.when(_expert_changed(te_ref, t))
    def _():
        w1b[...] = w1_ref[...].astype(BF16)
        w3b[...] = w3_ref[...].astype(BF16)

    @pl.when(t < nu_ref[0])
    def _():
        lo, hi = _unpack_halves(xs_ref[...])
        half = lo.shape[1]
        a1 = (jnp.dot(lo, w1b[:half], preferred_element_type=F32)
              + jnp.dot(hi, w1b[half:], preferred_element_type=F32))
        a3 = (jnp.dot(lo, w3b[:half], preferred_element_type=F32)
              + jnp.dot(hi, w3b[half:], preferred_element_type=F32))
        o_ref[...] = (_silu(a1) * a3).astype(o_ref.dtype)

    @pl.when(t >= nu_ref[0])
    def _():
        o_ref[...] = jnp.zeros(o_ref.shape, o_ref.dtype)


def _moe_up(xs, w1, w3, layer, te, nu, *, tg, tn):
    rows, half = xs.shape
    _, e, d, f = w1.shape
    wspec = pl.BlockSpec((None, None, d, tn), lambda j, t, te, nu: (layer, te[t], 0, j))
    return pl.pallas_call(
        _moe_up_kernel,
        grid_spec=pltpu.PrefetchScalarGridSpec(
            num_scalar_prefetch=2,
            grid=(f // tn, rows // tg),
            in_specs=[pl.BlockSpec((tg, half), lambda j, t, te, nu: (t, 0)), wspec, wspec],
            out_specs=pl.BlockSpec((tg, tn), lambda j, t, te, nu: (t, j)),
            scratch_shapes=[pltpu.VMEM((d, tn), BF16)] * 2),
        out_shape=jax.ShapeDtypeStruct((rows, f), BF16),
        compiler_params=_cparams(("arbitrary", "arbitrary")),
        name="moe_up",
    )(te, nu, xs, w1, w3)


def _moe_down_kernel(te_ref, nu_ref, a_ref, w2_ref, o_ref, w2b):
    t = pl.program_id(1)

    @pl.when(_expert_changed(te_ref, t))
    def _():
        w2b[...] = w2_ref[...].astype(BF16)

    @pl.when(t < nu_ref[0])
    def _():
        o_ref[...] = jnp.dot(a_ref[...], w2b[...], preferred_element_type=F32)

    @pl.when(t >= nu_ref[0])
    def _():
        o_ref[...] = jnp.zeros(o_ref.shape, o_ref.dtype)


def _moe_down(a, w2, layer, te, nu, *, tg, tn):
    rows, f = a.shape
    d = w2.shape[-1]
    return pl.pallas_call(
        _moe_down_kernel,
        grid_spec=pltpu.PrefetchScalarGridSpec(
            num_scalar_prefetch=2,
            grid=(d // tn, rows // tg),
            in_specs=[pl.BlockSpec((tg, f), lambda j, t, te, nu: (t, 0)),
                      pl.BlockSpec((None, None, f, tn), lambda j, t, te, nu: (layer, te[t], 0, j))],
            out_specs=pl.BlockSpec((tg, tn), lambda j, t, te, nu: (t, j)),
            scratch_shapes=[pltpu.VMEM((f, tn), BF16)]),
        out_shape=jax.ShapeDtypeStruct((rows, d), F32),
        compiler_params=_cparams(("arbitrary", "arbitrary")),
        name="moe_down",
    )(te, nu, a, w2)


def _combine_kernel(dest_ref, dnext_ref, x_ref, g_ref, p_ref, ys_ref, o_ref, buf, sem, *, steps):
    tb = x_ref.shape[0]
    i = pl.program_id(0)
    slot = i % 2

    def fetch(d_ref, to):
        def issue(r, carry):
            for s in range(2):
                _row_copy(ys_ref, d_ref[s, r], buf.at[to, s], r, sem.at[to]).start()
            return carry
        lax.fori_loop(0, tb, issue, 0)

    @pl.when(i == 0)
    def _():
        fetch(dest_ref, 0)

    @pl.when(i + 1 < steps)
    def _():
        fetch(dnext_ref, 1 - slot)

    def drain(r, carry):
        for s in range(2):
            _row_copy(ys_ref, 0, buf.at[slot, s], 0, sem.at[slot]).wait()
        return carry

    lax.fori_loop(0, tb, drain, 0)
    p = p_ref[...]
    o_ref[...] = x_ref[...] + g_ref[0] * (p[:, 0:1] * buf[slot, 0] + p[:, 1:2] * buf[slot, 1])


def _combine(x, g, p, dest3, ys, *, rows_per_batch, tb):
    n, d = x.shape
    tpb = rows_per_batch // tb
    steps = n // tb
    return pl.pallas_call(
        functools.partial(_combine_kernel, steps=steps),
        grid=(steps,),
        in_specs=[pl.BlockSpec((None, 2, tb), lambda i: (i, 0, 0), memory_space=pltpu.SMEM),
                  pl.BlockSpec((None, 2, tb), lambda i: (jnp.minimum(i + 1, steps - 1), 0, 0),
                               memory_space=pltpu.SMEM),
                  pl.BlockSpec((tb, d), lambda i: (i, 0)),
                  pl.BlockSpec((1, 1, d), lambda i: (i // tpb, 0, 0)),
                  pl.BlockSpec((tb, LANES), lambda i: (i, 0)),
                  pl.BlockSpec(memory_space=pl.ANY)],
        out_specs=pl.BlockSpec((tb, d), lambda i: (i, 0)),
        out_shape=jax.ShapeDtypeStruct((n, d), F32),
        scratch_shapes=[pltpu.VMEM((2, 2, tb, d), F32), pltpu.SemaphoreType.DMA((2,))],
        compiler_params=_cparams(("arbitrary",)),
        name="moe_combine",
    )(dest3, dest3, x, g, p, ys)


def _moe_ffn(x, norm_g, sc, sh, g, router_w, router_b, w1, w3, w2, layer, *, rows_per_batch, tg=512, tb=256,
             tn_up=896, tn_down=1024):
    n, d = x.shape
    e = w1.shape[1]
    t_max = 2 * n // tg + e
    hp, comb, sel = _norm_route(x, norm_g, sc, sh, router_w, router_b, rows_per_batch=rows_per_batch)
    rank, tot = _rank(sel)
    cnt = tot[0, :e].astype(jnp.int32)
    tiles = (cnt + tg - 1) // tg
    tile_end = jnp.cumsum(tiles)
    n_used = tile_end[-1]
    off = jnp.zeros((1, LANES), F32).at[0, :e].set(((tile_end - tiles) * tg).astype(F32))
    te = jnp.sum(jnp.arange(t_max, dtype=jnp.int32)[:, None] >= tile_end[None, :], axis=1).astype(jnp.int32)
    te = jnp.minimum(te, te[jnp.maximum(n_used - 1, 0)])
    nu = n_used.reshape(1).astype(jnp.int32)
    dest, p = _dest(rank, sel, comb, off)
    dest3 = dest[:, :2].T.reshape(2, n // tb, tb).transpose(1, 0, 2)
    xs = _dispatch(hp, dest3, t_max * tg, tb=tb)
    a = _moe_up(xs, w1, w3, layer, te, nu, tg=tg, tn=tn_up)
    ys = _moe_down(a, w2, layer, te, nu, tg=tg, tn=tn_down)
    return _combine(x, g, p, dest3, ys, rows_per_batch=rows_per_batch, tb=tb)


def _pad_cols(w, width):
    return jnp.zeros((w.shape[0], width), w.dtype).at[:, :w.shape[1]].set(w)


def kernel(x, c, ada_w, ada_b, norm1_g, norm2_g, w_in, w_out, pool_w, pool_scale, q_norm_g, k_norm_g,
           m_conv_w, m_wq, m_wk, m_gate_b, m_norm_g, ffn_w1, ffn_w3, ffn_w2, router_w, router_b,
           moe_w1, moe_w3, moe_w2, final_norm_g):
    batch, seq, d = x.shape
    depth = ada_w.shape[0]
    n = batch * seq
    ng = 4 * N_M_HEADS
    d_main = w_in.shape[2] - ng
    xf = x.reshape(n, d)
    mod = _adaln_mod(c, ada_w, ada_b)
    cos, sin = _rope_tables(seq)
    dims = dict(batch=batch, seq=seq)

    for l in range(depth):
        sh1, sc1, g1, sh2, sc2, g2 = [mod[l, :, i * d:(i + 1) * d].reshape(batch, 1, d) for i in range(6)]
        h = _norm(xf, norm1_g[l], sc1, sh1, rows_per_batch=seq, out_dtype=BF16)
        z = _mm([h], w_in, (l,), tm=1024, tn=512, ncols=d_main, out_dtype=BF16)
        gates = _mm([h], w_in, (l,), tm=1024, tn=LANES, ncols=LANES, col_blk0=d_main // LANES, valid_cols=ng)
        y_pool = _pool_mixer(z, pool_w[l], pool_scale[l], **dims)
        qkv = _qk_norm_rope(z, cos, sin, q_norm_g[l], k_norm_g[l], **dims)
        y_attn = _attention(qkv, tq=1024, tk=512, **dims)
        qm, km = _mlstm_qk(z, m_conv_w[l], m_wq[l], m_wk[l], **dims)
        hf, hb = _mlstm_scan(qm, km, z, gates, gates[:, :ng].T, m_gate_b[l], **dims)
        y_m = _mlstm_out(hf, hb, z, m_norm_g[l])
        xf = _mm([y_pool, y_attn, y_m], w_out, (l,), tm=1024, tn=512, x=xf, g=g1, rows_per_batch=seq)
        i = l // 2
        if l % 2 == 0:
            h2 = _norm(xf, norm2_g[l], sc2, sh2, rows_per_batch=seq, out_dtype=BF16)
            a = _swiglu_up(h2, ffn_w1, ffn_w3, (i,), tm=1024, tn=512)
            xf = _mm([a], ffn_w2, (i,), tm=512, tn=512, x=xf, g=g2, rows_per_batch=seq)
        else:
            xf = _moe_ffn(xf, norm2_g[l], sc2, sh2, g2, router_w[i], router_b[i], moe_w1, moe_w3, moe_w2, i,
                          rows_per_batch=seq)

    out = _norm(xf, final_norm_g, rows_per_batch=seq, out_dtype=F32)
    return out.reshape(batch, seq, d)
```

```python
import functools
import math

import numpy as np
import jax
import jax.numpy as jnp
from jax import lax
from jax.experimental import pallas as pl
from jax.experimental.pallas import tpu as pltpu

F32 = jnp.float32
BF16 = jnp.bfloat16

EPS = 1e-6
HEAD_DIM = 128
GRID_W = 64
ROPE_THETA = 10000.0
POOL_WINDOWS = (2, 4, 8, 16)
M_CHUNK = 128
M_CONV_W = 5
N_M_HEADS = 4
N_Q_HEADS = 8
N_KV_HEADS = 2
Q_PER_KV = N_Q_HEADS // N_KV_HEADS
N_EXPERTS = 8
LANES = 128
HALO = 16
VMEM_LIMIT = 56 * 1024 * 1024
LOG2E = math.log2(math.e)


def _cparams(sem, flags=None):
    return pltpu.CompilerParams(dimension_semantics=sem, vmem_limit_bytes=VMEM_LIMIT, flags=flags)


def _silu(a):
    return a * jax.nn.sigmoid(a)


def _mod_kernel(c_ref, w_ref, b_ref, o_ref):
    ca = _silu(c_ref[...])
    o_ref[0] = jnp.dot(ca.astype(BF16), w_ref[0].astype(BF16), preferred_element_type=F32) + b_ref[0]


def _adaln_mod(c, ada_w, ada_b, tn=768):
    depth, d, n6 = ada_w.shape
    b = c.shape[0]
    cp = jnp.zeros((8, d), F32).at[:b].set(c)
    out = pl.pallas_call(
        _mod_kernel,
        grid=(depth, n6 // tn),
        in_specs=[
            pl.BlockSpec((8, d), lambda l, j: (0, 0)),
            pl.BlockSpec((1, d, tn), lambda l, j: (l, 0, j)),
            pl.BlockSpec((1, 1, tn), lambda l, j: (l, 0, j)),
        ],
        out_specs=pl.BlockSpec((1, 8, tn), lambda l, j: (l, 0, j)),
        out_shape=jax.ShapeDtypeStruct((depth, 8, n6), F32),
        compiler_params=_cparams(("arbitrary", "arbitrary")),
        name="adaln_mod",
    )(cp, ada_w, ada_b.reshape(depth, 1, n6))
    return out[:, :b]


def _norm_kernel(*refs, modulate):
    if modulate:
        x_ref, g_ref, sc_ref, sh_ref, o_ref = refs
    else:
        x_ref, g_ref, o_ref = refs
    x = x_ref[...]
    ms = jnp.mean(x * x, axis=-1, keepdims=True)
    y = x * lax.rsqrt(ms + EPS) * g_ref[...]
    if modulate:
        y = y * (1.0 + sc_ref[0]) + sh_ref[0]
    o_ref[...] = y.astype(o_ref.dtype)


def _norm(x, g, sc=None, sh=None, *, rows_per_batch, out_dtype, tm=1024):
    n, d = x.shape
    tpb = rows_per_batch // tm
    modulate = sc is not None
    in_specs = [pl.BlockSpec((tm, d), lambda i: (i, 0)), pl.BlockSpec((1, d), lambda i: (0, 0))]
    args = [x, g.reshape(1, d)]
    if modulate:
        in_specs += [pl.BlockSpec((1, 1, d), lambda i: (i // tpb, 0, 0))] * 2
        args += [sc, sh]
    return pl.pallas_call(
        functools.partial(_norm_kernel, modulate=modulate),
        grid=(n // tm,),
        in_specs=in_specs,
        out_specs=pl.BlockSpec((tm, d), lambda i: (i, 0)),
        out_shape=jax.ShapeDtypeStruct((n, d), out_dtype),
        compiler_params=_cparams(("arbitrary",)),
        name="rms_norm",
    )(*args)


def _mm_kernel(*refs, n_a, resid, valid_cols):
    a_refs, w_ref = refs[:n_a], refs[n_a]
    if resid:
        x_ref, g_ref, o_ref, wb = refs[n_a + 1:]
    else:
        o_ref, wb = refs[n_a + 1:]

    @pl.when(pl.program_id(1) == 0)
    def _():
        w = w_ref[...]
        if valid_cols is not None:
            w = jnp.where(lax.broadcasted_iota(jnp.int32, w.shape, 1) < valid_cols, w, 0.0)
        wb[...] = w.astype(BF16)

    acc, k0 = None, 0
    for a_ref in a_refs:
        kk = a_ref.shape[1]
        part = jnp.dot(a_ref[...], wb[k0:k0 + kk, :], preferred_element_type=F32)
        acc = part if acc is None else acc + part
        k0 += kk
    if resid:
        o_ref[...] = x_ref[...] + g_ref[0] * acc
    else:
        o_ref[...] = acc.astype(o_ref.dtype)


def _mm(a_list, w, w_idx, *, tm, tn, ncols=None, col_blk0=0, valid_cols=None, out_dtype=F32,
        x=None, g=None, rows_per_batch=None):
    m = a_list[0].shape[0]
    kdim = w.shape[-2]
    ncols = w.shape[-1] if ncols is None else ncols
    resid = x is not None
    lead = (None,) * len(w_idx)
    in_specs = [pl.BlockSpec((tm, a.shape[1]), lambda j, i: (i, 0)) for a in a_list]
    in_specs.append(pl.BlockSpec(lead + (kdim, tn), lambda j, i: tuple(w_idx) + (0, j + col_blk0)))
    args = list(a_list) + [w]
    if resid:
        tpb = rows_per_batch // tm
        in_specs += [pl.BlockSpec((tm, tn), lambda j, i: (i, j)),
                     pl.BlockSpec((1, 1, tn), lambda j, i: (i // tpb, 0, j))]
        args += [x, g]
    return pl.pallas_call(
        functools.partial(_mm_kernel, n_a=len(a_list), resid=resid, valid_cols=valid_cols),
        grid=(ncols // tn, m // tm),
        in_specs=in_specs,
        out_specs=pl.BlockSpec((tm, tn), lambda j, i: (i, j)),
        out_shape=jax.ShapeDtypeStruct((m, ncols), out_dtype),
        scratch_shapes=[pltpu.VMEM((kdim, tn), BF16)],
        compiler_params=_cparams(("arbitrary", "arbitrary")),
        name="matmul_resid" if resid else "matmul",
    )(*args)


def _head_norm_rope(x, g, cos, sin, first, scale):
    ms = jnp.mean(x * x, axis=-1, keepdims=True)
    y = x * lax.rsqrt(ms + EPS) * g
    rot = jnp.where(first, pltpu.roll(y, HEAD_DIM - HEAD_DIM // 4, 1), pltpu.roll(y, HEAD_DIM // 4, 1))
    return (y * cos + rot * sin) * scale


def _win_kernel(h_ref, w_ref, cos_ref, sin_ref, qg_ref, kg_ref, o_ref, wb):
    j = pl.program_id(0)

    @pl.when(pl.program_id(1) == 0)
    def _():
        wb[...] = w_ref[...].astype(BF16)

    acc = jnp.dot(h_ref[...], wb[...], preferred_element_type=F32)

    def store_roped(g_ref, scale, n_rope):
        cos = cos_ref[...]
        sin = sin_ref[...]
        lane = lax.broadcasted_iota(jnp.int32, cos.shape, 1)
        first = (lane % (HEAD_DIM // 2)) < (HEAD_DIM // 4)
        for hh in range(acc.shape[1] // HEAD_DIM):
            sl = slice(hh * HEAD_DIM, (hh + 1) * HEAD_DIM)
            xh = acc[:, sl]
            if hh < n_rope:
                xh = _head_norm_rope(xh, g_ref[...], cos, sin, first, scale)
            o_ref[:, sl] = xh.astype(o_ref.dtype)

    @pl.when((j == 1) | (j == 2))
    def _():
        store_roped(qg_ref, LOG2E / math.sqrt(HEAD_DIM), 4)

    @pl.when(j == 3)
    def _():
        store_roped(kg_ref, 1.0, N_KV_HEADS)

    @pl.when((j == 0) | (j > 3))
    def _():
        o_ref[...] = acc.astype(o_ref.dtype)


def _in_proj(h, w_in, layer, cos, sin, qg, kg, *, seq, ncols, tm=1024, tn=512):
    n, d = h.shape
    tps = seq // tm
    return pl.pallas_call(
        _win_kernel,
        grid=(ncols // tn, n // tm),
        in_specs=[pl.BlockSpec((tm, d), lambda j, i: (i, 0)),
                  pl.BlockSpec((None, d, tn), lambda j, i: (layer, 0, j)),
                  pl.BlockSpec((tm, HEAD_DIM), lambda j, i: (i % tps, 0)),
                  pl.BlockSpec((tm, HEAD_DIM), lambda j, i: (i % tps, 0)),
                  pl.BlockSpec((1, HEAD_DIM), lambda j, i: (0, 0)),
                  pl.BlockSpec((1, HEAD_DIM), lambda j, i: (0, 0))],
        out_specs=pl.BlockSpec((tm, tn), lambda j, i: (i, j)),
        out_shape=jax.ShapeDtypeStruct((n, ncols), BF16),
        scratch_shapes=[pltpu.VMEM((d, tn), BF16)],
        compiler_params=_cparams(("arbitrary", "arbitrary")),
        name="in_proj",
    )(h, w_in, cos, sin, qg.reshape(1, HEAD_DIM), kg.reshape(1, HEAD_DIM))


def _up_kernel(h_ref, w1_ref, w3_ref, o_ref, w1b, w3b):
    @pl.when(pl.program_id(1) == 0)
    def _():
        w1b[...] = w1_ref[...].astype(BF16)
        w3b[...] = w3_ref[...].astype(BF16)

    h = h_ref[...]
    a1 = jnp.dot(h, w1b[...], preferred_element_type=F32)
    a3 = jnp.dot(h, w3b[...], preferred_element_type=F32)
    o_ref[...] = (_silu(a1) * a3).astype(o_ref.dtype)


def _swiglu_up(h, w1, w3, w_idx, *, tm, tn):
    n, d = h.shape
    f = w1.shape[-1]
    lead = (None,) * len(w_idx)
    wspec = pl.BlockSpec(lead + (d, tn), lambda j, i: tuple(w_idx) + (0, j))
    return pl.pallas_call(
        _up_kernel,
        grid=(f // tn, n // tm),
        in_specs=[pl.BlockSpec((tm, d), lambda j, i: (i, 0)), wspec, wspec],
        out_specs=pl.BlockSpec((tm, tn), lambda j, i: (i, j)),
        out_shape=jax.ShapeDtypeStruct((n, f), BF16),
        scratch_shapes=[pltpu.VMEM((d, tn), BF16)] * 2,
        compiler_params=_cparams(("arbitrary", "arbitrary")),
        name="swiglu_up",
    )(h, w1, w3)


def _halo_specs(tl, width, col_block, nl, batch):
    hb = tl // HALO
    nh = nl * hb
    last = batch * nh - 1
    cur = pl.BlockSpec((tl, width), lambda b, i: (b * nl + i, col_block))
    prev = pl.BlockSpec((HALO, width), lambda b, i: (jnp.maximum(b * nh + i * hb - 1, 0), col_block))
    nxt = pl.BlockSpec((HALO, width), lambda b, i: (jnp.minimum(b * nh + (i + 1) * hb, last), col_block))
    return prev, cur, nxt


def _with_halo(prev_ref, cur_ref, next_ref, nl):
    i = pl.program_id(1)
    prev = jnp.where(i == 0, 0.0, prev_ref[...].astype(F32))
    nxt = jnp.where(i == nl - 1, 0.0, next_ref[...].astype(F32))
    return jnp.concatenate([prev, cur_ref[...].astype(F32), nxt], axis=0)


def _pool_kernel(prev_ref, cur_ref, next_ref, w_ref, s_ref, o_ref, *, tl, nl, seq):
    u = _with_halo(prev_ref, cur_ref, next_ref, nl)
    rows = tl + 2 * HALO
    t = (pl.program_id(1) * tl + lax.broadcasted_iota(jnp.int32, (tl, LANES), 0)).astype(F32)
    for g, w in enumerate(POOL_WINDOWS):
        half = w // 2
        ug = u[:, g * LANES:(g + 1) * LANES]
        s = ug + pltpu.roll(ug, 1, 0)
        sh = 1
        while sh < half:
            s = pltpu.roll(s, sh, 0) + pltpu.roll(s, rows - sh, 0)
            sh *= 2
        cnt = jnp.minimum(t + (half - 1), seq - 1.0) - jnp.maximum(t - half, 0.0) + 1.0
        mean = s[HALO:HALO + tl] / cnt
        diff = mean - ug[HALO:HALO + tl]
        y = jnp.dot(diff.astype(BF16), w_ref[g], preferred_element_type=F32)
        o_ref[:, g * LANES:(g + 1) * LANES] = (y * s_ref[:, g * LANES:(g + 1) * LANES]).astype(o_ref.dtype)


def _pool_mixer(z, w_pool, scale, *, batch, seq, tl=512):
    n = z.shape[0]
    nl = seq // tl
    width = len(POOL_WINDOWS) * LANES
    prev, cur, nxt = _halo_specs(tl, width, 0, nl, batch)
    return pl.pallas_call(
        functools.partial(_pool_kernel, tl=tl, nl=nl, seq=seq),
        grid=(batch, nl),
        in_specs=[prev, cur, nxt,
                  pl.BlockSpec((len(POOL_WINDOWS), LANES, LANES), lambda b, i: (0, 0, 0)),
                  pl.BlockSpec((1, width), lambda b, i: (0, 0))],
        out_specs=pl.BlockSpec((tl, width), lambda b, i: (b * nl + i, 0)),
        out_shape=jax.ShapeDtypeStruct((n, width), BF16),
        compiler_params=_cparams(("arbitrary", "arbitrary")),
        name="pool_mixer",
    )(z, z, z, w_pool.astype(BF16), scale.reshape(1, width))


def _rope_tables(seq):
    rows = seq // GRID_W
    row = jnp.broadcast_to(jnp.arange(rows, dtype=F32)[:, None], (rows, GRID_W)).reshape(seq)
    col = jnp.broadcast_to(jnp.arange(GRID_W, dtype=F32)[None, :], (rows, GRID_W)).reshape(seq)
    half = HEAD_DIM // 4
    inv = ROPE_THETA ** (-jnp.arange(half, dtype=F32) / half)
    ar = row[:, None] * inv[None, :]
    ac = col[:, None] * inv[None, :]
    cos = jnp.concatenate([jnp.cos(ar), jnp.cos(ar), jnp.cos(ac), jnp.cos(ac)], axis=-1)
    sin = jnp.concatenate([-jnp.sin(ar), jnp.sin(ar), -jnp.sin(ac), jnp.sin(ac)], axis=-1)
    return cos, sin


def _attn_kernel(q_ref, k_ref, v_ref, o_ref, m_scr, l_scr, acc_scr, *, nk, tk):
    ki = pl.program_id(3)

    @pl.when(ki == 0)
    def _():
        m_scr[...] = jnp.full(m_scr.shape, -jnp.inf, F32)
        l_scr[...] = jnp.zeros(l_scr.shape, F32)
        acc_scr[...] = jnp.zeros(acc_scr.shape, F32)

    k = k_ref[...]
    v = v_ref[...]
    for g in range(Q_PER_KV):
        q = q_ref[:, g * HEAD_DIM:(g + 1) * HEAD_DIM]
        s = lax.dot_general(q, k, (((1,), (1,)), ((), ())), preferred_element_type=F32)
        m_prev = m_scr[g]
        m_next = jnp.maximum(m_prev, jnp.max(s, axis=1, keepdims=True))
        p = jnp.exp2(s - jnp.concatenate([m_next] * (tk // LANES), axis=1))
        alpha = jnp.exp2(m_prev - m_next)
        l_scr[g] = alpha * l_scr[g] + jnp.sum(p, axis=1, keepdims=True)
        acc_scr[g] = alpha * acc_scr[g] + jnp.dot(p.astype(BF16), v, preferred_element_type=F32)
        m_scr[g] = m_next

    @pl.when(ki == nk - 1)
    def _():
        for g in range(Q_PER_KV):
            o_ref[:, g * HEAD_DIM:(g + 1) * HEAD_DIM] = (acc_scr[g] / l_scr[g]).astype(o_ref.dtype)


def _attention(z, *, batch, seq, q_col, tq=512, tk=512):
    n = z.shape[0]
    nq, nk = seq // tq, seq // tk
    qw = Q_PER_KV * HEAD_DIM
    qblk = q_col // qw
    kcol = q_col // HEAD_DIM + N_Q_HEADS
    vcol = kcol + N_KV_HEADS
    return pl.pallas_call(
        functools.partial(_attn_kernel, nk=nk, tk=tk),
        grid=(batch, N_KV_HEADS, nq, nk),
        in_specs=[pl.BlockSpec((tq, qw), lambda b, h, qi, ki: (b * nq + qi, qblk + h)),
                  pl.BlockSpec((tk, HEAD_DIM), lambda b, h, qi, ki: (b * nk + ki, kcol + h)),
                  pl.BlockSpec((tk, HEAD_DIM), lambda b, h, qi, ki: (b * nk + ki, vcol + h))],
        out_specs=pl.BlockSpec((tq, qw), lambda b, h, qi, ki: (b * nq + qi, h)),
        out_shape=jax.ShapeDtypeStruct((n, N_Q_HEADS * HEAD_DIM), BF16),
        scratch_shapes=[pltpu.VMEM((Q_PER_KV, tq, HEAD_DIM), F32)] * 3,
        compiler_params=_cparams(("arbitrary", "arbitrary", "arbitrary", "arbitrary")),
        name="flash_attention",
    )(z, z, z)


def _mconv_kernel(prev_ref, cur_ref, next_ref, cw_ref, wq_ref, wk_ref, q_ref, k_ref, *, tl, nl):
    u = _with_halo(prev_ref, cur_ref, next_ref, nl)
    rows = tl + 2 * HALO
    acc = None
    for kk in range(M_CONV_W):
        sh = (M_CONV_W // 2 - kk) % rows
        tap = (pltpu.roll(u, sh, 0) if sh else u) * cw_ref[kk:kk + 1, :]
        acc = tap if acc is None else acc + tap
    uc = _silu(acc[HALO:HALO + tl]).astype(BF16)
    for h in range(N_M_HEADS):
        sl = slice(h * HEAD_DIM, (h + 1) * HEAD_DIM)
        q_ref[:, sl] = jnp.dot(uc[:, sl], wq_ref[h], preferred_element_type=F32).astype(q_ref.dtype)
        kh = jnp.dot(uc[:, sl], wk_ref[h], preferred_element_type=F32) * (HEAD_DIM ** -0.5)
        k_ref[:, sl] = kh.astype(k_ref.dtype)


def _mlstm_qk(z, conv_w, wq, wk, *, batch, seq, tl=512):
    n = z.shape[0]
    nl = seq // tl
    width = N_M_HEADS * HEAD_DIM
    prev, cur, nxt = _halo_specs(tl, width, 4, nl, batch)
    wspec = pl.BlockSpec((N_M_HEADS, HEAD_DIM, HEAD_DIM), lambda b, i: (0, 0, 0))
    ospec = pl.BlockSpec((tl, width), lambda b, i: (b * nl + i, 0))
    return pl.pallas_call(
        functools.partial(_mconv_kernel, tl=tl, nl=nl),
        grid=(batch, nl),
        in_specs=[prev, cur, nxt, pl.BlockSpec((M_CONV_W, width), lambda b, i: (0, 0)), wspec, wspec],
        out_specs=[ospec, ospec],
        out_shape=[jax.ShapeDtypeStruct((n, width), BF16)] * 2,
        compiler_params=_cparams(("arbitrary", "arbitrary")),
        name="mlstm_conv_qk",
    )(z, z, z, conv_w, wq.astype(BF16), wk.astype(BF16))


def _split3(x):
    x1 = x.astype(BF16)
    r = x - x1.astype(F32)
    x2 = r.astype(BF16)
    x3 = (r - x2.astype(F32)).astype(BF16)
    return x1, x2, x3


def _mscan_kernel(qf_ref, kf_ref, vf_ref, gf_ref, gtf_ref, qb_ref, kb_ref, vb_ref, gb_ref, gtb_ref,
                  bias_ref, biast_ref, hf_ref, hb_ref, s_scr, n_scr, m_scr):
    c = pl.program_id(0)

    @pl.when(c == 0)
    def _():
        s_scr[...] = jnp.zeros(s_scr.shape, F32)
        n_scr[...] = jnp.zeros(n_scr.shape, F32)
        m_scr[...] = jnp.zeros(m_scr.shape, F32)

    ch = M_CHUNK
    ti = lax.broadcasted_iota(jnp.int32, (ch, ch), 0)
    si = lax.broadcasted_iota(jnp.int32, (ch, ch), 1)
    nh = N_M_HEADS

    for b, rev in [(b, rev) for b in range(qf_ref.shape[0]) for rev in (False, True)]:
        q_ref, k_ref, v_ref, g_ref, gt_ref, h_ref = (
            (qb_ref, kb_ref, vb_ref, gb_ref, gtb_ref, hb_ref) if rev else
            (qf_ref, kf_ref, vf_ref, gf_ref, gtf_ref, hf_ref))
        causal = (si >= ti) if rev else (si <= ti)
        tri = jnp.where(causal, 1.0, 0.0).astype(BF16)
        tri_t = jnp.where((ti >= si) if rev else (ti <= si), 1.0, 0.0).astype(BF16)
        gates = g_ref[b] + bias_ref[...]
        gates_t = gt_ref[b] + biast_ref[...]
        lf = jax.nn.log_sigmoid(gates)
        lf_t = jax.nn.log_sigmoid(gates_t)
        cum = sum(jnp.dot(tri, part, preferred_element_type=F32) for part in _split3(lf))
        cum_t = sum(jnp.dot(part, tri_t, preferred_element_type=F32) for part in _split3(lf_t))
        last = 0 if rev else ch - 1
        for h in range(nh):
            idx = (2 * b + (1 if rev else 0)) * nh + h
            icol = (2 * nh if rev else 0) + h
            fcol = icol + nh
            sl = slice(h * HEAD_DIM, (h + 1) * HEAD_DIM)
            q = q_ref[b, :, sl]
            k = k_ref[b, :, sl]
            v = v_ref[b, :, sl]
            b_col = cum[:, fcol:fcol + 1]
            b_row = cum_t[fcol:fcol + 1, :]
            li_col = gates[:, icol:icol + 1]
            li_row = gates_t[icol:icol + 1, :]
            gtot = cum[last:last + 1, fcol:fcol + 1]
            m_prev = m_scr[idx][:, :1]
            n_prev = n_scr[idx]
            s_prev = s_scr[idx]

            dm = jnp.where(causal, b_col - b_row + li_row, -jnp.inf)
            m_inter = b_col + m_prev
            m_t = jnp.maximum(m_inter, jnp.max(dm, axis=1, keepdims=True))
            p = jnp.exp(dm - m_t)
            qk = lax.dot_general(q, k, (((1,), (1,)), ((), ())), preferred_element_type=F32)
            sc = qk * p
            w_inter = jnp.exp(m_inter - m_t)
            num = (jnp.dot(sc.astype(BF16), v, preferred_element_type=F32)
                   + w_inter * jnp.dot(q, s_prev.astype(BF16), preferred_element_type=F32))
            den = (jnp.sum(sc, axis=1, keepdims=True)
                   + w_inter * jnp.sum(q.astype(F32) * n_prev, axis=1, keepdims=True))
            h_ref[b, :, sl] = num / jnp.maximum(jnp.abs(den), jnp.exp(-m_t))

            w_st = gtot - b_col + li_col
            m_loc = jnp.max(w_st, axis=0, keepdims=True)
            ak = jnp.exp(w_st - m_loc) * k.astype(F32)
            s_c = lax.dot_general(ak.astype(BF16), v, (((0,), (0,)), ((), ())), preferred_element_type=F32)
            n_c = jnp.sum(ak, axis=0, keepdims=True)
            m_new = jnp.maximum(gtot + m_prev, m_loc)
            decay = jnp.exp(gtot + m_prev - m_new)
            add = jnp.exp(m_loc - m_new)
            s_scr[idx] = decay * s_prev + add * s_c
            n_scr[idx] = decay * n_prev + add * n_c
            m_scr[idx] = jnp.broadcast_to(m_new, (1, HEAD_DIM))


def _mlstm_scan(qm, km, z, gates, gates_t, gate_b, *, batch, seq):
    n = qm.shape[0]
    nc = seq // M_CHUNK
    width = N_M_HEADS * HEAD_DIM
    ng = 4 * N_M_HEADS
    chains = 2 * batch * N_M_HEADS
    fwd = lambda c: (0, c, 0)
    bwd = lambda c: (0, nc - 1 - c, 0)
    fwd_v = lambda c: (0, c, 5)
    bwd_v = lambda c: (0, nc - 1 - c, 5)
    fwd_t = lambda c: (0, 0, c)
    bwd_t = lambda c: (0, 0, nc - 1 - c)
    blk = (batch, M_CHUNK, width)
    gblk = (batch, M_CHUNK, LANES)
    tblk = (batch, ng, M_CHUNK)
    bias = jnp.zeros((1, LANES), F32).at[0, :ng].set(gate_b)
    bias_t = jnp.broadcast_to(gate_b[:, None], (ng, M_CHUNK))
    const = lambda c: (0, 0)
    in_specs = [
        pl.BlockSpec(blk, fwd), pl.BlockSpec(blk, fwd), pl.BlockSpec(blk, fwd_v),
        pl.BlockSpec(gblk, fwd), pl.BlockSpec(tblk, fwd_t),
        pl.BlockSpec(blk, bwd), pl.BlockSpec(blk, bwd), pl.BlockSpec(blk, bwd_v),
        pl.BlockSpec(gblk, bwd), pl.BlockSpec(tblk, bwd_t),
        pl.BlockSpec((1, LANES), const), pl.BlockSpec((ng, M_CHUNK), const),
    ]
    q3, k3, z3, g3 = (a.reshape(batch, seq, a.shape[1]) for a in (qm, km, z, gates))
    hf, hb = pl.pallas_call(
        _mscan_kernel,
        grid=(nc,),
        in_specs=in_specs,
        out_specs=[pl.BlockSpec(blk, fwd), pl.BlockSpec(blk, bwd)],
        out_shape=[jax.ShapeDtypeStruct((batch, seq, width), F32)] * 2,
        scratch_shapes=[pltpu.VMEM((chains, HEAD_DIM, HEAD_DIM), F32),
                        pltpu.VMEM((chains, 1, HEAD_DIM), F32),
                        pltpu.VMEM((chains, 1, HEAD_DIM), F32)],
        compiler_params=_cparams(("arbitrary",)),
        name="mlstm_scan",
    )(q3, k3, z3, g3, gates_t, q3, k3, z3, g3, gates_t, bias, bias_t)
    return hf.reshape(n, width), hb.reshape(n, width)


def _mout_kernel(hf_ref, hb_ref, o_ref, g_ref, y_ref):
    for h in range(N_M_HEADS):
        sl = slice(h * HEAD_DIM, (h + 1) * HEAD_DIM)
        x = hf_ref[:, sl] + hb_ref[:, sl]
        ms = jnp.mean(x * x, axis=-1, keepdims=True)
        hn = x * lax.rsqrt(ms + EPS) * g_ref[...]
        y_ref[:, sl] = (jax.nn.sigmoid(o_ref[:, sl].astype(F32)) * hn).astype(y_ref.dtype)


def _mlstm_out(hf, hb, z, norm_g, *, tm=1024):
    n, width = hf.shape
    spec = pl.BlockSpec((tm, width), lambda i: (i, 0))
    return pl.pallas_call(
        _mout_kernel,
        grid=(n // tm,),
        in_specs=[spec, spec, pl.BlockSpec((tm, width), lambda i: (i, 6)),
                  pl.BlockSpec((1, HEAD_DIM), lambda i: (0, 0))],
        out_specs=spec,
        out_shape=jax.ShapeDtypeStruct((n, width), BF16),
        compiler_params=_cparams(("arbitrary",)),
        name="mlstm_out",
    )(hf, hb, z, norm_g.reshape(1, HEAD_DIM))


def _top2(logit):
    lane = lax.broadcasted_iota(jnp.int32, logit.shape, 1).astype(F32)
    logit = jnp.where(lane < N_EXPERTS, logit, -jnp.inf)
    m1 = jnp.max(logit, axis=1, keepdims=True)
    i1 = jnp.min(jnp.where(logit == m1, lane, float(LANES)), axis=1, keepdims=True)
    rest = jnp.where(lane == i1, -jnp.inf, logit)
    m2 = jnp.max(rest, axis=1, keepdims=True)
    i2 = jnp.min(jnp.where(rest == m2, lane, float(LANES)), axis=1, keepdims=True)
    e = jnp.exp(m2 - m1)
    p1 = 1.0 / (1.0 + e)
    p2 = e / (1.0 + e)
    comb = jnp.where(lane == i1, p1, 0.0) + jnp.where(lane == i2, p2, 0.0)
    sel = jnp.where((lane == i1) | (lane == i2), 1.0, 0.0)
    return comb, sel


def _pack_halves(y):
    half = y.shape[1] // 2
    lo = pltpu.bitcast(y[:, :half].astype(BF16).astype(F32), jnp.uint32)
    hi = pltpu.bitcast(y[:, half:].astype(BF16).astype(F32), jnp.uint32)
    return hi | (lo >> 16)


def _unpack_halves(w):
    lo = pltpu.bitcast(w << 16, F32).astype(BF16)
    hi = pltpu.bitcast(w & jnp.uint32(0xFFFF0000), F32).astype(BF16)
    return lo, hi


def _norm_route_kernel(x_ref, g_ref, sc_ref, sh_ref, rw_ref, rb_ref, hp_ref, comb_ref, sel_ref):
    x = x_ref[...]
    ms = jnp.mean(x * x, axis=-1, keepdims=True)
    y = x * lax.rsqrt(ms + EPS) * g_ref[...]
    y = y * (1.0 + sc_ref[0]) + sh_ref[0]
    hp_ref[...] = _pack_halves(y)
    logit = jnp.dot(y, rw_ref[...], precision=lax.Precision.HIGHEST, preferred_element_type=F32) + rb_ref[...]
    comb_ref[...], sel_ref[...] = _top2(logit)


def _norm_route(x, g, sc, sh, router_w, router_b, *, rows_per_batch, tm=512):
    n, d = x.shape
    tpb = rows_per_batch // tm
    lspec = pl.BlockSpec((tm, LANES), lambda i: (i, 0))
    rb = jnp.zeros((1, LANES), F32).at[0, :N_EXPERTS].set(router_b)
    return pl.pallas_call(
        _norm_route_kernel,
        grid=(n // tm,),
        in_specs=[pl.BlockSpec((tm, d), lambda i: (i, 0)), pl.BlockSpec((1, d), lambda i: (0, 0)),
                  pl.BlockSpec((1, 1, d), lambda i: (i // tpb, 0, 0)), pl.BlockSpec((1, 1, d), lambda i: (i // tpb, 0, 0)),
                  pl.BlockSpec((d, LANES), lambda i: (0, 0)), pl.BlockSpec((1, LANES), lambda i: (0, 0))],
        out_specs=[pl.BlockSpec((tm, d // 2), lambda i: (i, 0)), lspec, lspec],
        out_shape=[jax.ShapeDtypeStruct((n, d // 2), jnp.uint32), jax.ShapeDtypeStruct((n, LANES), F32),
                   jax.ShapeDtypeStruct((n, LANES), F32)],
        compiler_params=_cparams(("arbitrary",)),
        name="norm_route",
    )(x, g.reshape(1, d), sc, sh, _pad_cols(router_w, LANES), rb)


def _rank_kernel(sel_ref, rank_ref, tot_ref, carry):
    @pl.when(pl.program_id(0) == 0)
    def _():
        carry[...] = jnp.zeros(carry.shape, F32)

    sel = sel_ref[...]
    tb = sel.shape[0]
    r = lax.broadcasted_iota(jnp.int32, (tb, tb), 0)
    c = lax.broadcasted_iota(jnp.int32, (tb, tb), 1)
    tri = jnp.where(r >= c, 1.0, 0.0).astype(BF16)
    incl = jnp.dot(tri, sel.astype(BF16), preferred_element_type=F32)
    rank_ref[...] = incl - sel + carry[...]
    total = carry[...] + incl[tb - 1:tb, :]
    carry[...] = total
    tot_ref[...] = total


def _rank(sel, *, tb=512):
    n = sel.shape[0]
    return pl.pallas_call(
        _rank_kernel,
        grid=(n // tb,),
        in_specs=[pl.BlockSpec((tb, LANES), lambda i: (i, 0))],
        out_specs=[pl.BlockSpec((tb, LANES), lambda i: (i, 0)), pl.BlockSpec((1, LANES), lambda i: (0, 0))],
        out_shape=[jax.ShapeDtypeStruct((n, LANES), F32), jax.ShapeDtypeStruct((1, LANES), F32)],
        scratch_shapes=[pltpu.VMEM((1, LANES), F32)],
        compiler_params=_cparams(("arbitrary",)),
        name="moe_rank",
    )(sel)


def _dest_kernel(rank_ref, sel_ref, comb_ref, off_ref, dest_ref, p_ref):
    lane = lax.broadcasted_iota(jnp.int32, rank_ref.shape, 1).astype(F32)
    sel = sel_ref[...] > 0.5
    pos = rank_ref[...] + off_ref[...]
    la = jnp.min(jnp.where(sel, lane, float(LANES)), axis=1, keepdims=True)
    lb = jnp.max(jnp.where(sel, lane, -1.0), axis=1, keepdims=True)
    pick = lambda l, v: jnp.sum(jnp.where(lane == l, v, 0.0), axis=1, keepdims=True)
    two = lambda a, b: jnp.where(lane == 0.0, a, jnp.where(lane == 1.0, b, 0.0))
    dest_ref[...] = two(pick(la, pos), pick(lb, pos)).astype(jnp.int32)
    p_ref[...] = two(pick(la, comb_ref[...]), pick(lb, comb_ref[...]))


def _dest(rank, sel, comb, off, *, tb=1024):
    n = rank.shape[0]
    spec = pl.BlockSpec((tb, LANES), lambda i: (i, 0))
    return pl.pallas_call(
        _dest_kernel,
        grid=(n // tb,),
        in_specs=[spec, spec, spec, pl.BlockSpec((1, LANES), lambda i: (0, 0))],
        out_specs=[spec, spec],
        out_shape=[jax.ShapeDtypeStruct((n, LANES), jnp.int32), jax.ShapeDtypeStruct((n, LANES), F32)],
        compiler_params=_cparams(("arbitrary",)),
        name="moe_dest",
    )(rank, sel, comb, off)


def _row_copy(src, src_row, dst, dst_row, sem):
    return pltpu.make_async_copy(src.at[pl.ds(src_row, 1), :], dst.at[pl.ds(dst_row, 1), :], sem)


def _dispatch_kernel(dest_ref, hp_ref, xs_in_ref, xs_ref, sem):
    del xs_in_ref
    tb = hp_ref.shape[0]
    for r in range(tb):
        for s in range(2):
            _row_copy(hp_ref, r, xs_ref, dest_ref[s, r], sem).start()
    for s in range(2):
        pltpu.make_async_copy(hp_ref, xs_ref.at[pl.ds(0, tb), :], sem).wait()


def _dispatch(hp, dest3, rows, *, tb):
    n, w = hp.shape
    xs0 = jnp.zeros((rows, w), hp.dtype)
    return pl.pallas_call(
        _dispatch_kernel,
        grid=(n // tb,),
        in_specs=[pl.BlockSpec((None, 2, tb), lambda i: (i, 0, 0), memory_space=pltpu.SMEM),
                  pl.BlockSpec((tb, w), lambda i: (i, 0)),
                  pl.BlockSpec(memory_space=pl.ANY)],
        out_specs=pl.BlockSpec(memory_space=pl.ANY),
        out_shape=jax.ShapeDtypeStruct((rows, w), hp.dtype),
        scratch_shapes=[pltpu.SemaphoreType.DMA(())],
        input_output_aliases={2: 0},
        compiler_params=_cparams(("arbitrary",)),
        name="moe_dispatch",
    )(dest3, hp, xs0)


def _expert_changed(te_ref, t):
    return (t == 0) | (te_ref[t] != te_ref[jnp.maximum(t - 1, 0)])


def _moe_up_kernel(te_ref, nu_ref, xs_ref, w1_ref, w3_ref, o_ref, w1b, w3b):
    t = pl.program_id(1)

    @pl.when(_expert_changed(te_ref, t))
    def _():
        w1b[...] = w1_ref[...].astype(BF16)
        w3b[...] = w3_ref[...].astype(BF16)

    @pl.when(t < nu_ref[0])
    def _():
        lo, hi = _unpack_halves(xs_ref[...])
        half = lo.shape[1]
        a1 = (jnp.dot(lo, w1b[:half], preferred_element_type=F32)
              + jnp.dot(hi, w1b[half:], preferred_element_type=F32))
        a3 = (jnp.dot(lo, w3b[:half], preferred_element_type=F32)
              + jnp.dot(hi, w3b[half:], preferred_element_type=F32))
        o_ref[...] = (_silu(a1) * a3).astype(o_ref.dtype)

    @pl.when(t >= nu_ref[0])
    def _():
        o_ref[...] = jnp.zeros(o_ref.shape, o_ref.dtype)


def _moe_up(xs, w1, w3, layer, te, nu, *, tg, tn):
    rows, half = xs.shape
    _, e, d, f = w1.shape
    wspec = pl.BlockSpec((None, None, d, tn), lambda j, t, te, nu: (layer, te[t], 0, j))
    return pl.pallas_call(
        _moe_up_kernel,
        grid_spec=pltpu.PrefetchScalarGridSpec(
            num_scalar_prefetch=2,
            grid=(f // tn, rows // tg),
            in_specs=[pl.BlockSpec((tg, half), lambda j, t, te, nu: (t, 0)), wspec, wspec],
            out_specs=pl.BlockSpec((tg, tn), lambda j, t, te, nu: (t, j)),
            scratch_shapes=[pltpu.VMEM((d, tn), BF16)] * 2),
        out_shape=jax.ShapeDtypeStruct((rows, f), BF16),
        compiler_params=_cparams(("arbitrary", "arbitrary")),
        name="moe_up",
    )(te, nu, xs, w1, w3)


def _moe_down_kernel(te_ref, nu_ref, a_ref, w2_ref, o_ref, w2b):
    t = pl.program_id(1)

    @pl.when(_expert_changed(te_ref, t))
    def _():
        w2b[...] = w2_ref[...].astype(BF16)

    @pl.when(t < nu_ref[0])
    def _():
        o_ref[...] = jnp.dot(a_ref[...], w2b[...], preferred_element_type=F32)

    @pl.when(t >= nu_ref[0])
    def _():
        o_ref[...] = jnp.zeros(o_ref.shape, o_ref.dtype)


def _moe_down(a, w2, layer, te, nu, *, tg, tn):
    rows, f = a.shape
    d = w2.shape[-1]
    return pl.pallas_call(
        _moe_down_kernel,
        grid_spec=pltpu.PrefetchScalarGridSpec(
            num_scalar_prefetch=2,
            grid=(d // tn, rows // tg),
            in_specs=[pl.BlockSpec((tg, f), lambda j, t, te, nu: (t, 0)),
                      pl.BlockSpec((None, None, f, tn), lambda j, t, te, nu: (layer, te[t], 0, j))],
            out_specs=pl.BlockSpec((tg, tn), lambda j, t, te, nu: (t, j)),
            scratch_shapes=[pltpu.VMEM((f, tn), BF16)]),
        out_shape=jax.ShapeDtypeStruct((rows, d), F32),
        compiler_params=_cparams(("arbitrary", "arbitrary")),
        name="moe_down",
    )(te, nu, a, w2)


def _combine_kernel(dest_ref, x_ref, g_ref, p_ref, ys_ref, o_ref, buf, sem):
    tb = x_ref.shape[0]
    for r in range(tb):
        for s in range(2):
            _row_copy(ys_ref, dest_ref[s, r], buf.at[s], r, sem).start()
    for s in range(2):
        pltpu.make_async_copy(ys_ref.at[pl.ds(0, tb), :], buf.at[s], sem).wait()
    p = p_ref[...]
    o_ref[...] = x_ref[...] + g_ref[0] * (p[:, 0:1] * buf[0] + p[:, 1:2] * buf[1])


def _combine(x, g, p, dest3, ys, *, rows_per_batch, tb):
    n, d = x.shape
    tpb = rows_per_batch // tb
    return pl.pallas_call(
        _combine_kernel,
        grid=(n // tb,),
        in_specs=[pl.BlockSpec((None, 2, tb), lambda i: (i, 0, 0), memory_space=pltpu.SMEM),
                  pl.BlockSpec((tb, d), lambda i: (i, 0)),
                  pl.BlockSpec((1, 1, d), lambda i: (i // tpb, 0, 0)),
                  pl.BlockSpec((tb, LANES), lambda i: (i, 0)),
                  pl.BlockSpec(memory_space=pl.ANY)],
        out_specs=pl.BlockSpec((tb, d), lambda i: (i, 0)),
        out_shape=jax.ShapeDtypeStruct((n, d), F32),
        scratch_shapes=[pltpu.VMEM((2, tb, d), F32), pltpu.SemaphoreType.DMA(())],
        compiler_params=_cparams(("arbitrary",)),
        name="moe_combine",
    )(dest3, x, g, p, ys)


def _moe_ffn(x, norm_g, sc, sh, g, router_w, router_b, w1, w3, w2, layer, *, rows_per_batch, tg=512, tb=256,
             tn_up=896, tn_down=1024):
    n, d = x.shape
    e = w1.shape[1]
    t_max = 2 * n // tg + e
    hp, comb, sel = _norm_route(x, norm_g, sc, sh, router_w, router_b, rows_per_batch=rows_per_batch)
    rank, tot = _rank(sel)
    cnt = tot[0, :e].astype(jnp.int32)
    tiles = (cnt + tg - 1) // tg
    tile_end = jnp.cumsum(tiles)
    n_used = tile_end[-1]
    off = jnp.zeros((1, LANES), F32).at[0, :e].set(((tile_end - tiles) * tg).astype(F32))
    te = jnp.sum(jnp.arange(t_max, dtype=jnp.int32)[:, None] >= tile_end[None, :], axis=1).astype(jnp.int32)
    te = jnp.minimum(te, te[jnp.maximum(n_used - 1, 0)])
    nu = n_used.reshape(1).astype(jnp.int32)
    dest, p = _dest(rank, sel, comb, off)
    dest3 = dest[:, :2].T.reshape(2, n // tb, tb).transpose(1, 0, 2)
    xs = _dispatch(hp, dest3, t_max * tg, tb=tb)
    a = _moe_up(xs, w1, w3, layer, te, nu, tg=tg, tn=tn_up)
    ys = _moe_down(a, w2, layer, te, nu, tg=tg, tn=tn_down)
    return _combine(x, g, p, dest3, ys, rows_per_batch=rows_per_batch, tb=tb)


def _pad_cols(w, width):
    return jnp.zeros((w.shape[0], width), w.dtype).at[:, :w.shape[1]].set(w)


def kernel(x, c, ada_w, ada_b, norm1_g, norm2_g, w_in, w_out, pool_w, pool_scale, q_norm_g, k_norm_g,
           m_conv_w, m_wq, m_wk, m_gate_b, m_norm_g, ffn_w1, ffn_w3, ffn_w2, router_w, router_b,
           moe_w1, moe_w3, moe_w2, final_norm_g):
    batch, seq, d = x.shape
    depth = ada_w.shape[0]
    n = batch * seq
    ng = 4 * N_M_HEADS
    d_main = w_in.shape[2] - ng
    xf = x.reshape(n, d)
    mod = _adaln_mod(c, ada_w, ada_b)
    cos, sin = _rope_tables(seq)
    dims = dict(batch=batch, seq=seq)

    for l in range(depth):
        sh1, sc1, g1, sh2, sc2, g2 = [mod[l, :, i * d:(i + 1) * d].reshape(batch, 1, d) for i in range(6)]
        h = _norm(xf, norm1_g[l], sc1, sh1, rows_per_batch=seq, out_dtype=BF16)
        z = _in_proj(h, w_in, l, cos, sin, q_norm_g[l], k_norm_g[l], seq=seq, ncols=d_main)
        gates = _mm([h], w_in, (l,), tm=1024, tn=LANES, ncols=LANES, col_blk0=d_main // LANES, valid_cols=ng)
        y_pool = _pool_mixer(z, pool_w[l], pool_scale[l], **dims)
        y_attn = _attention(z, q_col=pool_w.shape[1] * pool_w.shape[2], tq=2048, tk=512, **dims)
        qm, km = _mlstm_qk(z, m_conv_w[l], m_wq[l], m_wk[l], **dims)
        gates_t = gates[:, :ng].reshape(batch, seq, ng).transpose(0, 2, 1)
        hf, hb = _mlstm_scan(qm, km, z, gates, gates_t, m_gate_b[l], **dims)
        y_m = _mlstm_out(hf, hb, z, m_norm_g[l])
        xf = _mm([y_pool, y_attn, y_m], w_out, (l,), tm=1024, tn=512, x=xf, g=g1, rows_per_batch=seq)
        i = l // 2
        if l % 2 == 0:
            h2 = _norm(xf, norm2_g[l], sc2, sh2, rows_per_batch=seq, out_dtype=BF16)
            a = _swiglu_up(h2, ffn_w1, ffn_w3, (i,), tm=1024, tn=512)
            xf = _mm([a], ffn_w2, (i,), tm=512, tn=512, x=xf, g=g2, rows_per_batch=seq)
        else:
            xf = _moe_ffn(xf, norm2_g[l], sc2, sh2, g2, router_w[i], router_b[i], moe_w1, moe_w3, moe_w2, i,
                          rows_per_batch=seq)

    out = _norm(xf, final_norm_g, rows_per_batch=seq, out_dtype=F32)
    return out.reshape(batch, seq, d)
```

```python
import functools
import math

import numpy as np
import jax
import jax.numpy as jnp
from jax import lax
from jax.experimental import pallas as pl
from jax.experimental.pallas import tpu as pltpu

F32 = jnp.float32
BF16 = jnp.bfloat16

EPS = 1e-6
HEAD_DIM = 128
GRID_W = 64
ROPE_THETA = 10000.0
POOL_WINDOWS = (2, 4, 8, 16)
M_CHUNK = 128
M_CONV_W = 5
N_M_HEADS = 4
N_Q_HEADS = 8
N_KV_HEADS = 2
Q_PER_KV = N_Q_HEADS // N_KV_HEADS
N_EXPERTS = 8
LANES = 128
HALO = 16
VMEM_LIMIT = 56 * 1024 * 1024
LOG2E = math.log2(math.e)


def _cparams(sem, flags=None):
    return pltpu.CompilerParams(dimension_semantics=sem, vmem_limit_bytes=VMEM_LIMIT, flags=flags)


def _silu(a):
    return a * jax.nn.sigmoid(a)


def _mod_kernel(c_ref, w_ref, b_ref, o_ref):
    ca = _silu(c_ref[...])
    o_ref[0] = jnp.dot(ca.astype(BF16), w_ref[0].astype(BF16), preferred_element_type=F32) + b_ref[0]


def _adaln_mod(c, ada_w, ada_b, tn=768):
    depth, d, n6 = ada_w.shape
    b = c.shape[0]
    cp = jnp.zeros((8, d), F32).at[:b].set(c)
    out = pl.pallas_call(
        _mod_kernel,
        grid=(depth, n6 // tn),
        in_specs=[
            pl.BlockSpec((8, d), lambda l, j: (0, 0)),
            pl.BlockSpec((1, d, tn), lambda l, j: (l, 0, j)),
            pl.BlockSpec((1, 1, tn), lambda l, j: (l, 0, j)),
        ],
        out_specs=pl.BlockSpec((1, 8, tn), lambda l, j: (l, 0, j)),
        out_shape=jax.ShapeDtypeStruct((depth, 8, n6), F32),
        compiler_params=_cparams(("arbitrary", "arbitrary")),
        name="adaln_mod",
    )(cp, ada_w, ada_b.reshape(depth, 1, n6))
    return out[:, :b]


def _norm_kernel(*refs, modulate):
    if modulate:
        x_ref, g_ref, sc_ref, sh_ref, o_ref = refs
    else:
        x_ref, g_ref, o_ref = refs
    x = x_ref[...]
    ms = jnp.mean(x * x, axis=-1, keepdims=True)
    y = x * lax.rsqrt(ms + EPS) * g_ref[...]
    if modulate:
        y = y * (1.0 + sc_ref[0]) + sh_ref[0]
    o_ref[...] = y.astype(o_ref.dtype)


def _norm(x, g, sc=None, sh=None, *, rows_per_batch, out_dtype, tm=1024):
    n, d = x.shape
    tpb = rows_per_batch // tm
    modulate = sc is not None
    in_specs = [pl.BlockSpec((tm, d), lambda i: (i, 0)), pl.BlockSpec((1, d), lambda i: (0, 0))]
    args = [x, g.reshape(1, d)]
    if modulate:
        in_specs += [pl.BlockSpec((1, 1, d), lambda i: (i // tpb, 0, 0))] * 2
        args += [sc, sh]
    return pl.pallas_call(
        functools.partial(_norm_kernel, modulate=modulate),
        grid=(n // tm,),
        in_specs=in_specs,
        out_specs=pl.BlockSpec((tm, d), lambda i: (i, 0)),
        out_shape=jax.ShapeDtypeStruct((n, d), out_dtype),
        compiler_params=_cparams(("arbitrary",)),
        name="rms_norm",
    )(*args)


def _mm_kernel(*refs, n_a, resid, valid_cols):
    a_refs, w_ref = refs[:n_a], refs[n_a]
    if resid:
        x_ref, g_ref, o_ref, wb = refs[n_a + 1:]
    else:
        o_ref, wb = refs[n_a + 1:]

    @pl.when(pl.program_id(1) == 0)
    def _():
        w = w_ref[...]
        if valid_cols is not None:
            w = jnp.where(lax.broadcasted_iota(jnp.int32, w.shape, 1) < valid_cols, w, 0.0)
        wb[...] = w.astype(BF16)

    acc, k0 = None, 0
    for a_ref in a_refs:
        kk = a_ref.shape[1]
        part = jnp.dot(a_ref[...], wb[k0:k0 + kk, :], preferred_element_type=F32)
        acc = part if acc is None else acc + part
        k0 += kk
    if resid:
        o_ref[...] = x_ref[...] + g_ref[0] * acc
    else:
        o_ref[...] = acc.astype(o_ref.dtype)


def _mm(a_list, w, w_idx, *, tm, tn, ncols=None, col_blk0=0, valid_cols=None, out_dtype=F32,
        x=None, g=None, rows_per_batch=None):
    m = a_list[0].shape[0]
    kdim = w.shape[-2]
    ncols = w.shape[-1] if ncols is None else ncols
    resid = x is not None
    lead = (None,) * len(w_idx)
    in_specs = [pl.BlockSpec((tm, a.shape[1]), lambda j, i: (i, 0)) for a in a_list]
    in_specs.append(pl.BlockSpec(lead + (kdim, tn), lambda j, i: tuple(w_idx) + (0, j + col_blk0)))
    args = list(a_list) + [w]
    if resid:
        tpb = rows_per_batch // tm
        in_specs += [pl.BlockSpec((tm, tn), lambda j, i: (i, j)),
                     pl.BlockSpec((1, 1, tn), lambda j, i: (i // tpb, 0, j))]
        args += [x, g]
    return pl.pallas_call(
        functools.partial(_mm_kernel, n_a=len(a_list), resid=resid, valid_cols=valid_cols),
        grid=(ncols // tn, m // tm),
        in_specs=in_specs,
        out_specs=pl.BlockSpec((tm, tn), lambda j, i: (i, j)),
        out_shape=jax.ShapeDtypeStruct((m, ncols), out_dtype),
        scratch_shapes=[pltpu.VMEM((kdim, tn), BF16)],
        compiler_params=_cparams(("arbitrary", "arbitrary")),
        name="matmul_resid" if resid else "matmul",
    )(*args)


def _head_norm_rope(x, g, cos, sin, first, scale):
    ms = jnp.mean(x * x, axis=-1, keepdims=True)
    y = x * lax.rsqrt(ms + EPS) * g
    rot = jnp.where(first, pltpu.roll(y, HEAD_DIM - HEAD_DIM // 4, 1), pltpu.roll(y, HEAD_DIM // 4, 1))
    return (y * cos + rot * sin) * scale


def _win_kernel(h_ref, w_ref, cos_ref, sin_ref, qg_ref, kg_ref, o_ref, wb):
    j = pl.program_id(0)

    @pl.when(pl.program_id(1) == 0)
    def _():
        wb[...] = w_ref[...].astype(BF16)

    acc = jnp.dot(h_ref[...], wb[...], preferred_element_type=F32)

    def store_roped(g_ref, scale, n_rope):
        cos = cos_ref[...]
        sin = sin_ref[...]
        lane = lax.broadcasted_iota(jnp.int32, cos.shape, 1)
        first = (lane % (HEAD_DIM // 2)) < (HEAD_DIM // 4)
        for hh in range(acc.shape[1] // HEAD_DIM):
            sl = slice(hh * HEAD_DIM, (hh + 1) * HEAD_DIM)
            xh = acc[:, sl]
            if hh < n_rope:
                xh = _head_norm_rope(xh, g_ref[...], cos, sin, first, scale)
            o_ref[:, sl] = xh.astype(o_ref.dtype)

    @pl.when((j == 1) | (j == 2))
    def _():
        store_roped(qg_ref, LOG2E / math.sqrt(HEAD_DIM), 4)

    @pl.when(j == 3)
    def _():
        store_roped(kg_ref, 1.0, N_KV_HEADS)

    @pl.when((j == 0) | (j > 3))
    def _():
        o_ref[...] = acc.astype(o_ref.dtype)


def _in_proj(h, w_in, layer, cos, sin, qg, kg, *, seq, ncols, tm=1024, tn=512):
    n, d = h.shape
    tps = seq // tm
    return pl.pallas_call(
        _win_kernel,
        grid=(ncols // tn, n // tm),
        in_specs=[pl.BlockSpec((tm, d), lambda j, i: (i, 0)),
                  pl.BlockSpec((None, d, tn), lambda j, i: (layer, 0, j)),
                  pl.BlockSpec((tm, HEAD_DIM), lambda j, i: (i % tps, 0)),
                  pl.BlockSpec((tm, HEAD_DIM), lambda j, i: (i % tps, 0)),
                  pl.BlockSpec((1, HEAD_DIM), lambda j, i: (0, 0)),
                  pl.BlockSpec((1, HEAD_DIM), lambda j, i: (0, 0))],
        out_specs=pl.BlockSpec((tm, tn), lambda j, i: (i, j)),
        out_shape=jax.ShapeDtypeStruct((n, ncols), BF16),
        scratch_shapes=[pltpu.VMEM((d, tn), BF16)],
        compiler_params=_cparams(("arbitrary", "arbitrary")),
        name="in_proj",
    )(h, w_in, cos, sin, qg.reshape(1, HEAD_DIM), kg.reshape(1, HEAD_DIM))


def _up_kernel(h_ref, w1_ref, w3_ref, o_ref, w1b, w3b):
    @pl.when(pl.program_id(1) == 0)
    def _():
        w1b[...] = w1_ref[...].astype(BF16)
        w3b[...] = w3_ref[...].astype(BF16)

    h = h_ref[...]
    a1 = jnp.dot(h, w1b[...], preferred_element_type=F32)
    a3 = jnp.dot(h, w3b[...], preferred_element_type=F32)
    o_ref[...] = (_silu(a1) * a3).astype(o_ref.dtype)


def _swiglu_up(h, w1, w3, w_idx, *, tm, tn):
    n, d = h.shape
    f = w1.shape[-1]
    lead = (None,) * len(w_idx)
    wspec = pl.BlockSpec(lead + (d, tn), lambda j, i: tuple(w_idx) + (0, j))
    return pl.pallas_call(
        _up_kernel,
        grid=(f // tn, n // tm),
        in_specs=[pl.BlockSpec((tm, d), lambda j, i: (i, 0)), wspec, wspec],
        out_specs=pl.BlockSpec((tm, tn), lambda j, i: (i, j)),
        out_shape=jax.ShapeDtypeStruct((n, f), BF16),
        scratch_shapes=[pltpu.VMEM((d, tn), BF16)] * 2,
        compiler_params=_cparams(("arbitrary", "arbitrary")),
        name="swiglu_up",
    )(h, w1, w3)


def _halo_specs(tl, width, col_block, nl, batch):
    hb = tl // HALO
    nh = nl * hb
    last = batch * nh - 1
    cur = pl.BlockSpec((tl, width), lambda b, i: (b * nl + i, col_block))
    prev = pl.BlockSpec((HALO, width), lambda b, i: (jnp.maximum(b * nh + i * hb - 1, 0), col_block))
    nxt = pl.BlockSpec((HALO, width), lambda b, i: (jnp.minimum(b * nh + (i + 1) * hb, last), col_block))
    return prev, cur, nxt


def _with_halo(prev_ref, cur_ref, next_ref, nl):
    i = pl.program_id(1)
    prev = jnp.where(i == 0, 0.0, prev_ref[...].astype(F32))
    nxt = jnp.where(i == nl - 1, 0.0, next_ref[...].astype(F32))
    return jnp.concatenate([prev, cur_ref[...].astype(F32), nxt], axis=0)


def _pool_kernel(prev_ref, cur_ref, next_ref, w_ref, s_ref, o_ref, *, tl, nl, seq):
    u = _with_halo(prev_ref, cur_ref, next_ref, nl)
    rows = tl + 2 * HALO
    t = (pl.program_id(1) * tl + lax.broadcasted_iota(jnp.int32, (tl, LANES), 0)).astype(F32)
    for g, w in enumerate(POOL_WINDOWS):
        half = w // 2
        ug = u[:, g * LANES:(g + 1) * LANES]
        s = ug + pltpu.roll(ug, 1, 0)
        sh = 1
        while sh < half:
            s = pltpu.roll(s, sh, 0) + pltpu.roll(s, rows - sh, 0)
            sh *= 2
        cnt = jnp.minimum(t + (half - 1), seq - 1.0) - jnp.maximum(t - half, 0.0) + 1.0
        mean = s[HALO:HALO + tl] / cnt
        diff = mean - ug[HALO:HALO + tl]
        y = jnp.dot(diff.astype(BF16), w_ref[g], preferred_element_type=F32)
        o_ref[:, g * LANES:(g + 1) * LANES] = (y * s_ref[:, g * LANES:(g + 1) * LANES]).astype(o_ref.dtype)


def _pool_mixer(z, w_pool, scale, *, batch, seq, tl=512):
    n = z.shape[0]
    nl = seq // tl
    width = len(POOL_WINDOWS) * LANES
    prev, cur, nxt = _halo_specs(tl, width, 0, nl, batch)
    return pl.pallas_call(
        functools.partial(_pool_kernel, tl=tl, nl=nl, seq=seq),
        grid=(batch, nl),
        in_specs=[prev, cur, nxt,
                  pl.BlockSpec((len(POOL_WINDOWS), LANES, LANES), lambda b, i: (0, 0, 0)),
                  pl.BlockSpec((1, width), lambda b, i: (0, 0))],
        out_specs=pl.BlockSpec((tl, width), lambda b, i: (b * nl + i, 0)),
        out_shape=jax.ShapeDtypeStruct((n, width), BF16),
        compiler_params=_cparams(("arbitrary", "arbitrary")),
        name="pool_mixer",
    )(z, z, z, w_pool.astype(BF16), scale.reshape(1, width))


def _rope_tables(seq):
    rows = seq // GRID_W
    row = jnp.broadcast_to(jnp.arange(rows, dtype=F32)[:, None], (rows, GRID_W)).reshape(seq)
    col = jnp.broadcast_to(jnp.arange(GRID_W, dtype=F32)[None, :], (rows, GRID_W)).reshape(seq)
    half = HEAD_DIM // 4
    inv = ROPE_THETA ** (-jnp.arange(half, dtype=F32) / half)
    ar = row[:, None] * inv[None, :]
    ac = col[:, None] * inv[None, :]
    cos = jnp.concatenate([jnp.cos(ar), jnp.cos(ar), jnp.cos(ac), jnp.cos(ac)], axis=-1)
    sin = jnp.concatenate([-jnp.sin(ar), jnp.sin(ar), -jnp.sin(ac), jnp.sin(ac)], axis=-1)
    return cos, sin


def _attn_kernel(q_ref, k_ref, v_ref, o_ref, m_scr, l_scr, acc_scr, *, nk, tk):
    ki = pl.program_id(3)

    @pl.when(ki == 0)
    def _():
        m_scr[...] = jnp.full(m_scr.shape, -jnp.inf, F32)
        l_scr[...] = jnp.zeros(l_scr.shape, F32)
        acc_scr[...] = jnp.zeros(acc_scr.shape, F32)

    k = k_ref[...]
    v = v_ref[...]
    for g in range(Q_PER_KV):
        q = q_ref[:, g * HEAD_DIM:(g + 1) * HEAD_DIM]
        s = lax.dot_general(q, k, (((1,), (1,)), ((), ())), preferred_element_type=F32)
        m_prev = m_scr[g]
        m_next = jnp.maximum(m_prev, jnp.max(s, axis=1, keepdims=True))
        p = jnp.exp2(s - jnp.concatenate([m_next] * (tk // LANES), axis=1))
        alpha = jnp.exp2(m_prev - m_next)
        l_scr[g] = alpha * l_scr[g] + jnp.sum(p, axis=1, keepdims=True)
        acc_scr[g] = alpha * acc_scr[g] + jnp.dot(p.astype(BF16), v, preferred_element_type=F32)
        m_scr[g] = m_next

    @pl.when(ki == nk - 1)
    def _():
        for g in range(Q_PER_KV):
            o_ref[:, g * HEAD_DIM:(g + 1) * HEAD_DIM] = (acc_scr[g] / l_scr[g]).astype(o_ref.dtype)


def _attention(z, *, batch, seq, q_col, tq=512, tk=512):
    n = z.shape[0]
    nq, nk = seq // tq, seq // tk
    qw = Q_PER_KV * HEAD_DIM
    qblk = q_col // qw
    kcol = q_col // HEAD_DIM + N_Q_HEADS
    vcol = kcol + N_KV_HEADS
    return pl.pallas_call(
        functools.partial(_attn_kernel, nk=nk, tk=tk),
        grid=(batch, N_KV_HEADS, nq, nk),
        in_specs=[pl.BlockSpec((tq, qw), lambda b, h, qi, ki: (b * nq + qi, qblk + h)),
                  pl.BlockSpec((tk, HEAD_DIM), lambda b, h, qi, ki: (b * nk + ki, kcol + h)),
                  pl.BlockSpec((tk, HEAD_DIM), lambda b, h, qi, ki: (b * nk + ki, vcol + h))],
        out_specs=pl.BlockSpec((tq, qw), lambda b, h, qi, ki: (b * nq + qi, h)),
        out_shape=jax.ShapeDtypeStruct((n, N_Q_HEADS * HEAD_DIM), BF16),
        scratch_shapes=[pltpu.VMEM((Q_PER_KV, tq, HEAD_DIM), F32)] * 3,
        compiler_params=_cparams(("arbitrary", "arbitrary", "arbitrary", "arbitrary")),
        name="flash_attention",
    )(z, z, z)


def _mconv_kernel(prev_ref, cur_ref, next_ref, cw_ref, wq_ref, wk_ref, q_ref, k_ref, *, tl, nl):
    u = _with_halo(prev_ref, cur_ref, next_ref, nl)
    rows = tl + 2 * HALO
    acc = None
    for kk in range(M_CONV_W):
        sh = (M_CONV_W // 2 - kk) % rows
        tap = (pltpu.roll(u, sh, 0) if sh else u) * cw_ref[kk:kk + 1, :]
        acc = tap if acc is None else acc + tap
    uc = _silu(acc[HALO:HALO + tl]).astype(BF16)
    for h in range(N_M_HEADS):
        sl = slice(h * HEAD_DIM, (h + 1) * HEAD_DIM)
        q_ref[:, sl] = jnp.dot(uc[:, sl], wq_ref[h], preferred_element_type=F32).astype(q_ref.dtype)
        kh = jnp.dot(uc[:, sl], wk_ref[h], preferred_element_type=F32) * (HEAD_DIM ** -0.5)
        k_ref[:, sl] = kh.astype(k_ref.dtype)


def _mlstm_qk(z, conv_w, wq, wk, *, batch, seq, tl=512):
    n = z.shape[0]
    nl = seq // tl
    width = N_M_HEADS * HEAD_DIM
    prev, cur, nxt = _halo_specs(tl, width, 4, nl, batch)
    wspec = pl.BlockSpec((N_M_HEADS, HEAD_DIM, HEAD_DIM), lambda b, i: (0, 0, 0))
    ospec = pl.BlockSpec((tl, width), lambda b, i: (b * nl + i, 0))
    return pl.pallas_call(
        functools.partial(_mconv_kernel, tl=tl, nl=nl),
        grid=(batch, nl),
        in_specs=[prev, cur, nxt, pl.BlockSpec((M_CONV_W, width), lambda b, i: (0, 0)), wspec, wspec],
        out_specs=[ospec, ospec],
        out_shape=[jax.ShapeDtypeStruct((n, width), BF16)] * 2,
        compiler_params=_cparams(("arbitrary", "arbitrary")),
        name="mlstm_conv_qk",
    )(z, z, z, conv_w, wq.astype(BF16), wk.astype(BF16))


def _split3(x):
    x1 = x.astype(BF16)
    r = x - x1.astype(F32)
    x2 = r.astype(BF16)
    x3 = (r - x2.astype(F32)).astype(BF16)
    return x1, x2, x3


def _mscan_kernel(qf_ref, kf_ref, vf_ref, gf_ref, gtf_ref, qb_ref, kb_ref, vb_ref, gb_ref, gtb_ref,
                  bias_ref, biast_ref, hf_ref, hb_ref, s_scr, n_scr, m_scr):
    c = pl.program_id(0)

    @pl.when(c == 0)
    def _():
        s_scr[...] = jnp.zeros(s_scr.shape, F32)
        n_scr[...] = jnp.zeros(n_scr.shape, F32)
        m_scr[...] = jnp.zeros(m_scr.shape, F32)

    ch = M_CHUNK
    ti = lax.broadcasted_iota(jnp.int32, (ch, ch), 0)
    si = lax.broadcasted_iota(jnp.int32, (ch, ch), 1)
    nh = N_M_HEADS

    for b, rev in [(b, rev) for b in range(qf_ref.shape[0]) for rev in (False, True)]:
        q_ref, k_ref, v_ref, g_ref, gt_ref, h_ref = (
            (qb_ref, kb_ref, vb_ref, gb_ref, gtb_ref, hb_ref) if rev else
            (qf_ref, kf_ref, vf_ref, gf_ref, gtf_ref, hf_ref))
        causal = (si >= ti) if rev else (si <= ti)
        tri = jnp.where(causal, 1.0, 0.0).astype(BF16)
        tri_t = jnp.where((ti >= si) if rev else (ti <= si), 1.0, 0.0).astype(BF16)
        gates = g_ref[b] + bias_ref[...]
        gates_t = gt_ref[b] + biast_ref[...]
        lf = jax.nn.log_sigmoid(gates)
        lf_t = jax.nn.log_sigmoid(gates_t)
        cum = sum(jnp.dot(tri, part, preferred_element_type=F32) for part in _split3(lf))
        cum_t = sum(jnp.dot(part, tri_t, preferred_element_type=F32) for part in _split3(lf_t))
        last = 0 if rev else ch - 1
        for h in range(nh):
            idx = (2 * b + (1 if rev else 0)) * nh + h
            icol = (2 * nh if rev else 0) + h
            fcol = icol + nh
            sl = slice(h * HEAD_DIM, (h + 1) * HEAD_DIM)
            q = q_ref[b, :, sl]
            k = k_ref[b, :, sl]
            v = v_ref[b, :, sl]
            b_col = cum[:, fcol:fcol + 1]
            b_row = cum_t[fcol:fcol + 1, :]
            li_col = gates[:, icol:icol + 1]
            li_row = gates_t[icol:icol + 1, :]
            gtot = cum[last:last + 1, fcol:fcol + 1]
            m_prev = m_scr[idx][:, :1]
            n_prev = n_scr[idx]
            s_prev = s_scr[idx]

            dm = jnp.where(causal, b_col - b_row + li_row, -jnp.inf)
            m_inter = b_col + m_prev
            m_t = jnp.maximum(m_inter, jnp.max(dm, axis=1, keepdims=True))
            p = jnp.exp(dm - m_t)
            qk = lax.dot_general(q, k, (((1,), (1,)), ((), ())), preferred_element_type=F32)
            sc = qk * p
            w_inter = jnp.exp(m_inter - m_t)
            num = (jnp.dot(sc.astype(BF16), v, preferred_element_type=F32)
                   + w_inter * jnp.dot(q, s_prev.astype(BF16), preferred_element_type=F32))
            den = (jnp.sum(sc, axis=1, keepdims=True)
                   + w_inter * jnp.sum(q.astype(F32) * n_prev, axis=1, keepdims=True))
            h_ref[b, :, sl] = num / jnp.maximum(jnp.abs(den), jnp.exp(-m_t))

            w_st = gtot - b_col + li_col
            m_loc = jnp.max(w_st, axis=0, keepdims=True)
            ak = jnp.exp(w_st - m_loc) * k.astype(F32)
            s_c = lax.dot_general(ak.astype(BF16), v, (((0,), (0,)), ((), ())), preferred_element_type=F32)
            n_c = jnp.sum(ak, axis=0, keepdims=True)
            m_new = jnp.maximum(gtot + m_prev, m_loc)
            decay = jnp.exp(gtot + m_prev - m_new)
            add = jnp.exp(m_loc - m_new)
            s_scr[idx] = decay * s_prev + add * s_c
            n_scr[idx] = decay * n_prev + add * n_c
            m_scr[idx] = jnp.broadcast_to(m_new, (1, HEAD_DIM))


def _mlstm_scan(qm, km, z, gates, gates_t, gate_b, *, batch, seq):
    n = qm.shape[0]
    nc = seq // M_CHUNK
    width = N_M_HEADS * HEAD_DIM
    ng = 4 * N_M_HEADS
    chains = 2 * batch * N_M_HEADS
    fwd = lambda c: (0, c, 0)
    bwd = lambda c: (0, nc - 1 - c, 0)
    fwd_v = lambda c: (0, c, 5)
    bwd_v = lambda c: (0, nc - 1 - c, 5)
    fwd_t = lambda c: (0, 0, c)
    bwd_t = lambda c: (0, 0, nc - 1 - c)
    blk = (batch, M_CHUNK, width)
    gblk = (batch, M_CHUNK, LANES)
    tblk = (batch, ng, M_CHUNK)
    bias = jnp.zeros((1, LANES), F32).at[0, :ng].set(gate_b)
    bias_t = jnp.broadcast_to(gate_b[:, None], (ng, M_CHUNK))
    const = lambda c: (0, 0)
    in_specs = [
        pl.BlockSpec(blk, fwd), pl.BlockSpec(blk, fwd), pl.BlockSpec(blk, fwd_v),
        pl.BlockSpec(gblk, fwd), pl.BlockSpec(tblk, fwd_t),
        pl.BlockSpec(blk, bwd), pl.BlockSpec(blk, bwd), pl.BlockSpec(blk, bwd_v),
        pl.BlockSpec(gblk, bwd), pl.BlockSpec(tblk, bwd_t),
        pl.BlockSpec((1, LANES), const), pl.BlockSpec((ng, M_CHUNK), const),
    ]
    q3, k3, z3, g3 = (a.reshape(batch, seq, a.shape[1]) for a in (qm, km, z, gates))
    hf, hb = pl.pallas_call(
        _mscan_kernel,
        grid=(nc,),
        in_specs=in_specs,
        out_specs=[pl.BlockSpec(blk, fwd), pl.BlockSpec(blk, bwd)],
        out_shape=[jax.ShapeDtypeStruct((batch, seq, width), F32)] * 2,
        scratch_shapes=[pltpu.VMEM((chains, HEAD_DIM, HEAD_DIM), F32),
                        pltpu.VMEM((chains, 1, HEAD_DIM), F32),
                        pltpu.VMEM((chains, 1, HEAD_DIM), F32)],
        compiler_params=_cparams(("arbitrary",)),
        name="mlstm_scan",
    )(q3, k3, z3, g3, gates_t, q3, k3, z3, g3, gates_t, bias, bias_t)
    return hf.reshape(n, width), hb.reshape(n, width)


def _mout_kernel(hf_ref, hb_ref, o_ref, g_ref, y_ref):
    for h in range(N_M_HEADS):
        sl = slice(h * HEAD_DIM, (h + 1) * HEAD_DIM)
        x = hf_ref[:, sl] + hb_ref[:, sl]
        ms = jnp.mean(x * x, axis=-1, keepdims=True)
        hn = x * lax.rsqrt(ms + EPS) * g_ref[...]
        y_ref[:, sl] = (jax.nn.sigmoid(o_ref[:, sl].astype(F32)) * hn).astype(y_ref.dtype)


def _mlstm_out(hf, hb, z, norm_g, *, tm=1024):
    n, width = hf.shape
    spec = pl.BlockSpec((tm, width), lambda i: (i, 0))
    return pl.pallas_call(
        _mout_kernel,
        grid=(n // tm,),
        in_specs=[spec, spec, pl.BlockSpec((tm, width), lambda i: (i, 6)),
                  pl.BlockSpec((1, HEAD_DIM), lambda i: (0, 0))],
        out_specs=spec,
        out_shape=jax.ShapeDtypeStruct((n, width), BF16),
        compiler_params=_cparams(("arbitrary",)),
        name="mlstm_out",
    )(hf, hb, z, norm_g.reshape(1, HEAD_DIM))


def _top2(logit):
    lane = lax.broadcasted_iota(jnp.int32, logit.shape, 1).astype(F32)
    logit = jnp.where(lane < N_EXPERTS, logit, -jnp.inf)
    m1 = jnp.max(logit, axis=1, keepdims=True)
    i1 = jnp.min(jnp.where(logit == m1, lane, float(LANES)), axis=1, keepdims=True)
    rest = jnp.where(lane == i1, -jnp.inf, logit)
    m2 = jnp.max(rest, axis=1, keepdims=True)
    i2 = jnp.min(jnp.where(rest == m2, lane, float(LANES)), axis=1, keepdims=True)
    e = jnp.exp(m2 - m1)
    p1 = 1.0 / (1.0 + e)
    p2 = e / (1.0 + e)
    comb = jnp.where(lane == i1, p1, 0.0) + jnp.where(lane == i2, p2, 0.0)
    sel = jnp.where((lane == i1) | (lane == i2), 1.0, 0.0)
    return comb, sel


def _pack_halves(y):
    half = y.shape[1] // 2
    lo = pltpu.bitcast(y[:, :half].astype(BF16).astype(F32), jnp.uint32)
    hi = pltpu.bitcast(y[:, half:].astype(BF16).astype(F32), jnp.uint32)
    return hi | (lo >> 16)


def _unpack_halves(w):
    lo = pltpu.bitcast(w << 16, F32).astype(BF16)
    hi = pltpu.bitcast(w & jnp.uint32(0xFFFF0000), F32).astype(BF16)
    return lo, hi


def _mod_norm(x, g, sc=None, sh=None):
    ms = jnp.mean(x * x, axis=-1, keepdims=True)
    y = x * lax.rsqrt(ms + EPS) * g
    return y if sc is None else y * (1.0 + sc) + sh


def _out_proj_kernel(*refs, n_a, route):
    a_refs = refs[:n_a]
    w_ref, x_ref, g1_ref, ng_ref, sc_ref, sh_ref = refs[n_a:n_a + 6]
    if route:
        rw_ref, rb_ref, xo_ref, hp_ref, comb_ref, sel_ref, wb = refs[n_a + 6:]
    else:
        xo_ref, h_ref, wb = refs[n_a + 6:]

    @pl.when(pl.program_id(0) == 0)
    def _():
        wb[...] = w_ref[...].astype(BF16)

    acc, k0 = None, 0
    for a_ref in a_refs:
        kk = a_ref.shape[1]
        part = jnp.dot(a_ref[...], wb[k0:k0 + kk, :], preferred_element_type=F32)
        acc = part if acc is None else acc + part
        k0 += kk
    xn = x_ref[...] + g1_ref[0] * acc
    xo_ref[...] = xn
    y = _mod_norm(xn, ng_ref[...], sc_ref[0], sh_ref[0])
    if route:
        hp_ref[...] = _pack_halves(y)
        y_hi = y.astype(BF16)
        y_lo = (y - y_hi.astype(F32)).astype(BF16)
        logit = (jnp.dot(y_hi, rw_ref[0], preferred_element_type=F32)
                 + jnp.dot(y_lo, rw_ref[0], preferred_element_type=F32)
                 + jnp.dot(y_hi, rw_ref[1], preferred_element_type=F32)) + rb_ref[...]
        comb_ref[...], sel_ref[...] = _top2(logit)
    else:
        h_ref[...] = y.astype(h_ref.dtype)


def _out_proj(a_list, w_out, layer, x, g1, norm_g, sc, sh, router=None, *, rows_per_batch, tm=512):
    n, d = x.shape
    tpb = rows_per_batch // tm
    route = router is not None
    row = lambda w: pl.BlockSpec((tm, w), lambda i: (i, 0))
    per_batch = pl.BlockSpec((1, 1, d), lambda i: (i // tpb, 0, 0))
    const = lambda shape: pl.BlockSpec(shape, lambda i: (0,) * len(shape))
    in_specs = [row(a.shape[1]) for a in a_list]
    in_specs += [pl.BlockSpec((None, d, d), lambda i: (layer, 0, 0), pipeline_mode=pl.Buffered(1)), row(d), per_batch,
                 const((1, d)), per_batch, per_batch]
    args = list(a_list) + [w_out, x, g1, norm_g.reshape(1, d), sc, sh]
    out_specs = [row(d)]
    out_shape = [jax.ShapeDtypeStruct((n, d), F32)]
    if route:
        router_w, router_b = router
        rw = _pad_cols(router_w, LANES)
        rw_hi = rw.astype(BF16)
        rw_split = jnp.stack([rw_hi, (rw - rw_hi.astype(F32)).astype(BF16)])
        in_specs += [const((2, d, LANES)), const((1, LANES))]
        args += [rw_split, jnp.zeros((1, LANES), F32).at[0, :N_EXPERTS].set(router_b)]
        out_specs += [row(d // 2), row(LANES), row(LANES)]
        out_shape += [jax.ShapeDtypeStruct((n, d // 2), jnp.uint32), jax.ShapeDtypeStruct((n, LANES), F32),
                      jax.ShapeDtypeStruct((n, LANES), F32)]
    else:
        out_specs.append(row(d))
        out_shape.append(jax.ShapeDtypeStruct((n, d), BF16))
    return pl.pallas_call(
        functools.partial(_out_proj_kernel, n_a=len(a_list), route=route),
        grid=(n // tm,),
        in_specs=in_specs,
        out_specs=out_specs,
        out_shape=out_shape,
        scratch_shapes=[pltpu.VMEM((d, d), BF16)],
        compiler_params=_cparams(("arbitrary",)),
        name="out_proj_route" if route else "out_proj",
    )(*args)


def _rank_kernel(sel_ref, rank_ref, tot_ref, carry):
    @pl.when(pl.program_id(0) == 0)
    def _():
        carry[...] = jnp.zeros(carry.shape, F32)

    sel = sel_ref[...]
    tb = sel.shape[0]
    r = lax.broadcasted_iota(jnp.int32, (tb, tb), 0)
    c = lax.broadcasted_iota(jnp.int32, (tb, tb), 1)
    tri = jnp.where(r >= c, 1.0, 0.0).astype(BF16)
    incl = jnp.dot(tri, sel.astype(BF16), preferred_element_type=F32)
    rank_ref[...] = incl - sel + carry[...]
    total = carry[...] + incl[tb - 1:tb, :]
    carry[...] = total
    tot_ref[...] = total


def _rank(sel, *, tb=512):
    n = sel.shape[0]
    return pl.pallas_call(
        _rank_kernel,
        grid=(n // tb,),
        in_specs=[pl.BlockSpec((tb, LANES), lambda i: (i, 0))],
        out_specs=[pl.BlockSpec((tb, LANES), lambda i: (i, 0)), pl.BlockSpec((1, LANES), lambda i: (0, 0))],
        out_shape=[jax.ShapeDtypeStruct((n, LANES), F32), jax.ShapeDtypeStruct((1, LANES), F32)],
        scratch_shapes=[pltpu.VMEM((1, LANES), F32)],
        compiler_params=_cparams(("arbitrary",)),
        name="moe_rank",
    )(sel)


def _dest_kernel(rank_ref, sel_ref, comb_ref, off_ref, dest_ref, p_ref):
    lane = lax.broadcasted_iota(jnp.int32, rank_ref.shape, 1).astype(F32)
    sel = sel_ref[...] > 0.5
    pos = rank_ref[...] + off_ref[...]
    la = jnp.min(jnp.where(sel, lane, float(LANES)), axis=1, keepdims=True)
    lb = jnp.max(jnp.where(sel, lane, -1.0), axis=1, keepdims=True)
    pick = lambda l, v: jnp.sum(jnp.where(lane == l, v, 0.0), axis=1, keepdims=True)
    two = lambda a, b: jnp.where(lane == 0.0, a, jnp.where(lane == 1.0, b, 0.0))
    dest_ref[...] = two(pick(la, pos), pick(lb, pos)).astype(jnp.int32)
    p_ref[...] = two(pick(la, comb_ref[...]), pick(lb, comb_ref[...]))


def _dest(rank, sel, comb, off, *, tb=1024):
    n = rank.shape[0]
    spec = pl.BlockSpec((tb, LANES), lambda i: (i, 0))
    return pl.pallas_call(
        _dest_kernel,
        grid=(n // tb,),
        in_specs=[spec, spec, spec, pl.BlockSpec((1, LANES), lambda i: (0, 0))],
        out_specs=[spec, spec],
        out_shape=[jax.ShapeDtypeStruct((n, LANES), jnp.int32), jax.ShapeDtypeStruct((n, LANES), F32)],
        compiler_params=_cparams(("arbitrary",)),
        name="moe_dest",
    )(rank, sel, comb, off)


def _row_copy(src, src_row, dst, dst_row, sem):
    return pltpu.make_async_copy(src.at[pl.ds(src_row, 1), :], dst.at[pl.ds(dst_row, 1), :], sem)


def _dispatch_kernel(dest_ref, hp_ref, xs_in_ref, xs_ref, sem):
    del xs_in_ref
    tb = hp_ref.shape[0]
    for r in range(tb):
        for s in range(2):
            _row_copy(hp_ref, r, xs_ref, dest_ref[s, r], sem).start(priority=s)
    for s in range(2):
        pltpu.make_async_copy(hp_ref, xs_ref.at[pl.ds(0, tb), :], sem).wait()


def _dispatch(hp, dest3, rows, *, tb):
    n, w = hp.shape
    xs0 = jnp.zeros((rows, w), hp.dtype)
    return pl.pallas_call(
        _dispatch_kernel,
        grid=(n // tb,),
        in_specs=[pl.BlockSpec((None, 2, tb), lambda i: (i, 0, 0), memory_space=pltpu.SMEM),
                  pl.BlockSpec((tb, w), lambda i: (i, 0)),
                  pl.BlockSpec(memory_space=pl.ANY)],
        out_specs=pl.BlockSpec(memory_space=pl.ANY),
        out_shape=jax.ShapeDtypeStruct((rows, w), hp.dtype),
        scratch_shapes=[pltpu.SemaphoreType.DMA(())],
        input_output_aliases={2: 0},
        compiler_params=_cparams(("arbitrary",)),
        name="moe_dispatch",
    )(dest3, hp, xs0)


def _expert_changed(te_ref, t):
    return (t == 0) | (te_ref[t] != te_ref[jnp.maximum(t - 1, 0)])


def _moe_up_kernel(te_ref, nu_ref, xs_ref, w1_ref, w3_ref, o_ref, w1b, w3b):
    t = pl.program_id(1)

    @pl.when(_expert_changed(te_ref, t))
    def _():
        w1b[...] = w1_ref[...].astype(BF16)
        w3b[...] = w3_ref[...].astype(BF16)

    @pl.when(t < nu_ref[0])
    def _():
        x = jnp.concatenate(_unpack_halves(xs_ref[...]), axis=1)
        a1 = jnp.dot(x, w1b[...], preferred_element_type=F32)
        a3 = jnp.dot(x, w3b[...], preferred_element_type=F32)
        o_ref[...] = (_silu(a1) * a3).astype(o_ref.dtype)

    @pl.when(t >= nu_ref[0])
    def _():
        o_ref[...] = jnp.zeros(o_ref.shape, o_ref.dtype)


def _moe_up(xs, w1, w3, layer, te, nu, *, tg, tn):
    rows, half = xs.shape
    _, e, d, f = w1.shape
    wspec = pl.BlockSpec((None, None, d, tn), lambda j, t, te, nu: (layer, te[t], 0, j))
    return pl.pallas_call(
        _moe_up_kernel,
        grid_spec=pltpu.PrefetchScalarGridSpec(
            num_scalar_prefetch=2,
            grid=(f // tn, rows // tg),
            in_specs=[pl.BlockSpec((tg, half), lambda j, t, te, nu: (t, 0)), wspec, wspec],
            out_specs=pl.BlockSpec((tg, tn), lambda j, t, te, nu: (t, j)),
            scratch_shapes=[pltpu.VMEM((d, tn), BF16)] * 2),
        out_shape=jax.ShapeDtypeStruct((rows, f), BF16),
        compiler_params=_cparams(("arbitrary", "arbitrary")),
        name="moe_up",
    )(te, nu, xs, w1, w3)


def _moe_down_kernel(te_ref, nu_ref, a_ref, w2_ref, o_ref, w2b):
    t = pl.program_id(1)

    @pl.when(_expert_changed(te_ref, t))
    def _():
        w2b[...] = w2_ref[...].astype(BF16)

    @pl.when(t < nu_ref[0])
    def _():
        o_ref[...] = jnp.dot(a_ref[...], w2b[...], preferred_element_type=F32)

    @pl.when(t >= nu_ref[0])
    def _():
        o_ref[...] = jnp.zeros(o_ref.shape, o_ref.dtype)


def _moe_down(a, w2, layer, te, nu, *, tg, tn):
    rows, f = a.shape
    d = w2.shape[-1]
    return pl.pallas_call(
        _moe_down_kernel,
        grid_spec=pltpu.PrefetchScalarGridSpec(
            num_scalar_prefetch=2,
            grid=(d // tn, rows // tg),
            in_specs=[pl.BlockSpec((tg, f), lambda j, t, te, nu: (t, 0)),
                      pl.BlockSpec((None, None, f, tn), lambda j, t, te, nu: (layer, te[t], 0, j))],
            out_specs=pl.BlockSpec((tg, tn), lambda j, t, te, nu: (t, j)),
            scratch_shapes=[pltpu.VMEM((f, tn), BF16)]),
        out_shape=jax.ShapeDtypeStruct((rows, d), F32),
        compiler_params=_cparams(("arbitrary", "arbitrary")),
        name="moe_down",
    )(te, nu, a, w2)


def _combine_kernel(*refs, modulate):
    if modulate:
        dest_ref, x_ref, g_ref, p_ref, ng_ref, sc_ref, sh_ref, ys_ref, o_ref, h_ref, buf, sem = refs
    else:
        dest_ref, x_ref, g_ref, p_ref, ng_ref, ys_ref, o_ref, h_ref, buf, sem = refs
    tb = x_ref.shape[0]
    for r in range(tb):
        for s in range(2):
            _row_copy(ys_ref, dest_ref[s, r], buf.at[s], r, sem).start(priority=s)
    for s in range(2):
        pltpu.make_async_copy(ys_ref.at[pl.ds(0, tb), :], buf.at[s], sem).wait()
    p = p_ref[...]
    xn = x_ref[...] + g_ref[0] * (p[:, 0:1] * buf[0] + p[:, 1:2] * buf[1])
    o_ref[...] = xn
    if modulate:
        h_ref[...] = _mod_norm(xn, ng_ref[...], sc_ref[0], sh_ref[0]).astype(h_ref.dtype)
    else:
        h_ref[...] = _mod_norm(xn, ng_ref[...]).astype(h_ref.dtype)


def _combine(x, g, p, dest3, ys, norm_g, sc, sh, *, rows_per_batch, tb, h_dtype):
    n, d = x.shape
    tpb = rows_per_batch // tb
    modulate = sc is not None
    row = pl.BlockSpec((tb, d), lambda i: (i, 0))
    per_batch = pl.BlockSpec((1, 1, d), lambda i: (i // tpb, 0, 0))
    in_specs = [pl.BlockSpec((None, 2, tb), lambda i: (i, 0, 0), memory_space=pltpu.SMEM), row, per_batch,
                pl.BlockSpec((tb, LANES), lambda i: (i, 0)), pl.BlockSpec((1, d), lambda i: (0, 0))]
    args = [dest3, x, g, p, norm_g.reshape(1, d)]
    if modulate:
        in_specs += [per_batch, per_batch]
        args += [sc, sh]
    return pl.pallas_call(
        functools.partial(_combine_kernel, modulate=modulate),
        grid=(n // tb,),
        in_specs=in_specs + [pl.BlockSpec(memory_space=pl.ANY)],
        out_specs=[row, row],
        out_shape=[jax.ShapeDtypeStruct((n, d), F32), jax.ShapeDtypeStruct((n, d), h_dtype)],
        scratch_shapes=[pltpu.VMEM((2, tb, d), F32), pltpu.SemaphoreType.DMA(())],
        compiler_params=_cparams(("arbitrary",)),
        name="moe_combine",
    )(*args, ys)


def _moe_ffn(x, hp, comb, sel, g, w1, w3, w2, layer, next_norm, *, rows_per_batch, h_dtype, tg=512, tb=256,
             tn_up=896, tn_down=1024):
    n, d = x.shape
    e = w1.shape[1]
    t_max = 2 * n // tg + e
    rank, tot = _rank(sel)
    cnt = tot[0, :e].astype(jnp.int32)
    tiles = (cnt + tg - 1) // tg
    tile_end = jnp.cumsum(tiles)
    n_used = tile_end[-1]
    off = jnp.zeros((1, LANES), F32).at[0, :e].set(((tile_end - tiles) * tg).astype(F32))
    te = jnp.sum(jnp.arange(t_max, dtype=jnp.int32)[:, None] >= tile_end[None, :], axis=1).astype(jnp.int32)
    te = jnp.minimum(te, te[jnp.maximum(n_used - 1, 0)])
    nu = n_used.reshape(1).astype(jnp.int32)
    dest, p = _dest(rank, sel, comb, off)
    dest3 = dest[:, :2].T.reshape(2, n // tb, tb).transpose(1, 0, 2)
    xs = _dispatch(hp, dest3, t_max * tg, tb=tb)
    a = _moe_up(xs, w1, w3, layer, te, nu, tg=tg, tn=tn_up)
    ys = _moe_down(a, w2, layer, te, nu, tg=tg, tn=tn_down)
    return _combine(x, g, p, dest3, ys, *next_norm, rows_per_batch=rows_per_batch, tb=tb, h_dtype=h_dtype)


def _pad_cols(w, width):
    return jnp.zeros((w.shape[0], width), w.dtype).at[:, :w.shape[1]].set(w)


def kernel(x, c, ada_w, ada_b, norm1_g, norm2_g, w_in, w_out, pool_w, pool_scale, q_norm_g, k_norm_g,
           m_conv_w, m_wq, m_wk, m_gate_b, m_norm_g, ffn_w1, ffn_w3, ffn_w2, router_w, router_b,
           moe_w1, moe_w3, moe_w2, final_norm_g):
    batch, seq, d = x.shape
    depth = ada_w.shape[0]
    n = batch * seq
    ng = 4 * N_M_HEADS
    d_main = w_in.shape[2] - ng
    xf = x.reshape(n, d)
    mod = _adaln_mod(c, ada_w, ada_b)
    cos, sin = _rope_tables(seq)
    dims = dict(batch=batch, seq=seq)

    mods = [[mod[l, :, i * d:(i + 1) * d].reshape(batch, 1, d) for i in range(6)] for l in range(depth)]
    h = None
    for l in range(depth):
        sh1, sc1, g1, sh2, sc2, g2 = mods[l]
        if h is None:
            h = _norm(xf, norm1_g[l], sc1, sh1, rows_per_batch=seq, out_dtype=BF16)
        z = _in_proj(h, w_in, l, cos, sin, q_norm_g[l], k_norm_g[l], seq=seq, ncols=d_main)
        gates = _mm([h], w_in, (l,), tm=1024, tn=LANES, ncols=LANES, col_blk0=d_main // LANES, valid_cols=ng)
        y_pool = _pool_mixer(z, pool_w[l], pool_scale[l], **dims)
        y_attn = _attention(z, q_col=pool_w.shape[1] * pool_w.shape[2], tq=2048, tk=512, **dims)
        qm, km = _mlstm_qk(z, m_conv_w[l], m_wq[l], m_wk[l], **dims)
        gates_t = gates[:, :ng].reshape(batch, seq, ng).transpose(0, 2, 1)
        hf, hb = _mlstm_scan(qm, km, z, gates, gates_t, m_gate_b[l], **dims)
        y_m = _mlstm_out(hf, hb, z, m_norm_g[l])
        mixed = [y_pool, y_attn, y_m]
        i = l // 2
        if l % 2 == 0:
            xf, h2 = _out_proj(mixed, w_out, l, xf, g1, norm2_g[l], sc2, sh2, rows_per_batch=seq)
            a = _swiglu_up(h2, ffn_w1, ffn_w3, (i,), tm=1024, tn=512)
            xf = _mm([a], ffn_w2, (i,), tm=512, tn=512, x=xf, g=g2, rows_per_batch=seq)
            h = None
        else:
            xf, hp, comb, sel = _out_proj(mixed, w_out, l, xf, g1, norm2_g[l], sc2, sh2,
                                          (router_w[i], router_b[i]), rows_per_batch=seq)
            last = l == depth - 1
            next_norm = (final_norm_g, None, None) if last else (norm1_g[l + 1], mods[l + 1][1], mods[l + 1][0])
            xf, h = _moe_ffn(xf, hp, comb, sel, g2, moe_w1, moe_w3, moe_w2, i, next_norm, rows_per_batch=seq,
                             h_dtype=F32 if last else BF16)
            if last:
                return h.reshape(batch, seq, d)

    return _norm(xf, final_norm_g, rows_per_batch=seq, out_dtype=F32).reshape(batch, seq, d)
```

```python
import functools
import math

import numpy as np
import jax
import jax.numpy as jnp
from jax import lax
from jax.experimental import pallas as pl
from jax.experimental.pallas import tpu as pltpu

F32 = jnp.float32
BF16 = jnp.bfloat16

EPS = 1e-6
HEAD_DIM = 128
GRID_W = 64
ROPE_THETA = 10000.0
POOL_WINDOWS = (2, 4, 8, 16)
M_CHUNK = 128
M_CONV_W = 5
N_M_HEADS = 4
N_Q_HEADS = 8
N_KV_HEADS = 2
Q_PER_KV = N_Q_HEADS // N_KV_HEADS
N_EXPERTS = 8
LANES = 128
HALO = 16
VMEM_LIMIT = 56 * 1024 * 1024
LOG2E = math.log2(math.e)


def _cparams(sem, flags=None):
    return pltpu.CompilerParams(dimension_semantics=sem, vmem_limit_bytes=VMEM_LIMIT, flags=flags)


def _silu(a):
    return a * jax.nn.sigmoid(a)


def _mod_kernel(c_ref, w_ref, b_ref, o_ref):
    ca = _silu(c_ref[...])
    o_ref[0] = jnp.dot(ca.astype(BF16), w_ref[0].astype(BF16), preferred_element_type=F32) + b_ref[0]


def _adaln_mod(c, ada_w, ada_b, tn=768):
    depth, d, n6 = ada_w.shape
    b = c.shape[0]
    cp = jnp.zeros((8, d), F32).at[:b].set(c)
    out = pl.pallas_call(
        _mod_kernel,
        grid=(depth, n6 // tn),
        in_specs=[
            pl.BlockSpec((8, d), lambda l, j: (0, 0)),
            pl.BlockSpec((1, d, tn), lambda l, j: (l, 0, j)),
            pl.BlockSpec((1, 1, tn), lambda l, j: (l, 0, j)),
        ],
        out_specs=pl.BlockSpec((1, 8, tn), lambda l, j: (l, 0, j)),
        out_shape=jax.ShapeDtypeStruct((depth, 8, n6), F32),
        compiler_params=_cparams(("arbitrary", "arbitrary")),
        name="adaln_mod",
    )(cp, ada_w, ada_b.reshape(depth, 1, n6))
    return out[:, :b]


def _norm_kernel(*refs, modulate):
    if modulate:
        x_ref, g_ref, sc_ref, sh_ref, o_ref = refs
    else:
        x_ref, g_ref, o_ref = refs
    x = x_ref[...]
    ms = jnp.mean(x * x, axis=-1, keepdims=True)
    y = x * lax.rsqrt(ms + EPS) * g_ref[...]
    if modulate:
        y = y * (1.0 + sc_ref[0]) + sh_ref[0]
    o_ref[...] = y.astype(o_ref.dtype)


def _norm(x, g, sc=None, sh=None, *, rows_per_batch, out_dtype, tm=1024):
    n, d = x.shape
    tpb = rows_per_batch // tm
    modulate = sc is not None
    in_specs = [pl.BlockSpec((tm, d), lambda i: (i, 0)), pl.BlockSpec((1, d), lambda i: (0, 0))]
    args = [x, g.reshape(1, d)]
    if modulate:
        in_specs += [pl.BlockSpec((1, 1, d), lambda i: (i // tpb, 0, 0))] * 2
        args += [sc, sh]
    return pl.pallas_call(
        functools.partial(_norm_kernel, modulate=modulate),
        grid=(n // tm,),
        in_specs=in_specs,
        out_specs=pl.BlockSpec((tm, d), lambda i: (i, 0)),
        out_shape=jax.ShapeDtypeStruct((n, d), out_dtype),
        compiler_params=_cparams(("arbitrary",)),
        name="rms_norm",
    )(*args)


def _mm_kernel(*refs, n_a, resid, valid_cols):
    a_refs, w_ref = refs[:n_a], refs[n_a]
    if resid:
        x_ref, g_ref, o_ref, wb = refs[n_a + 1:]
    else:
        o_ref, wb = refs[n_a + 1:]

    @pl.when(pl.program_id(1) == 0)
    def _():
        w = w_ref[...]
        if valid_cols is not None:
            w = jnp.where(lax.broadcasted_iota(jnp.int32, w.shape, 1) < valid_cols, w, 0.0)
        wb[...] = w.astype(BF16)

    acc, k0 = None, 0
    for a_ref in a_refs:
        kk = a_ref.shape[1]
        part = jnp.dot(a_ref[...], wb[k0:k0 + kk, :], preferred_element_type=F32)
        acc = part if acc is None else acc + part
        k0 += kk
    if resid:
        o_ref[...] = x_ref[...] + g_ref[0] * acc
    else:
        o_ref[...] = acc.astype(o_ref.dtype)


def _mm(a_list, w, w_idx, *, tm, tn, ncols=None, col_blk0=0, valid_cols=None, out_dtype=F32,
        x=None, g=None, rows_per_batch=None):
    m = a_list[0].shape[0]
    kdim = w.shape[-2]
    ncols = w.shape[-1] if ncols is None else ncols
    resid = x is not None
    lead = (None,) * len(w_idx)
    in_specs = [pl.BlockSpec((tm, a.shape[1]), lambda j, i: (i, 0)) for a in a_list]
    in_specs.append(pl.BlockSpec(lead + (kdim, tn), lambda j, i: tuple(w_idx) + (0, j + col_blk0)))
    args = list(a_list) + [w]
    if resid:
        tpb = rows_per_batch // tm
        in_specs += [pl.BlockSpec((tm, tn), lambda j, i: (i, j)),
                     pl.BlockSpec((1, 1, tn), lambda j, i: (i // tpb, 0, j))]
        args += [x, g]
    return pl.pallas_call(
        functools.partial(_mm_kernel, n_a=len(a_list), resid=resid, valid_cols=valid_cols),
        grid=(ncols // tn, m // tm),
        in_specs=in_specs,
        out_specs=pl.BlockSpec((tm, tn), lambda j, i: (i, j)),
        out_shape=jax.ShapeDtypeStruct((m, ncols), out_dtype),
        scratch_shapes=[pltpu.VMEM((kdim, tn), BF16)],
        compiler_params=_cparams(("arbitrary", "arbitrary")),
        name="matmul_resid" if resid else "matmul",
    )(*args)


def _head_norm_rope(x, g, cos, sin, first, scale):
    ms = jnp.mean(x * x, axis=-1, keepdims=True)
    y = x * lax.rsqrt(ms + EPS) * g
    rot = jnp.where(first, pltpu.roll(y, HEAD_DIM - HEAD_DIM // 4, 1), pltpu.roll(y, HEAD_DIM // 4, 1))
    return (y * cos + rot * sin) * scale


def _win_kernel(h_ref, w_ref, cos_ref, sin_ref, qg_ref, kg_ref, o_ref, wb):
    j = pl.program_id(0)

    @pl.when(pl.program_id(1) == 0)
    def _():
        wb[...] = w_ref[...].astype(BF16)

    acc = jnp.dot(h_ref[...], wb[...], preferred_element_type=F32)

    def store_roped(g_ref, scale, n_rope):
        cos = cos_ref[...]
        sin = sin_ref[...]
        lane = lax.broadcasted_iota(jnp.int32, cos.shape, 1)
        first = (lane % (HEAD_DIM // 2)) < (HEAD_DIM // 4)
        for hh in range(acc.shape[1] // HEAD_DIM):
            sl = slice(hh * HEAD_DIM, (hh + 1) * HEAD_DIM)
            xh = acc[:, sl]
            if hh < n_rope:
                xh = _head_norm_rope(xh, g_ref[...], cos, sin, first, scale)
            o_ref[:, sl] = xh.astype(o_ref.dtype)

    @pl.when((j == 1) | (j == 2))
    def _():
        store_roped(qg_ref, LOG2E / math.sqrt(HEAD_DIM), 4)

    @pl.when(j == 3)
    def _():
        store_roped(kg_ref, 1.0, N_KV_HEADS)

    @pl.when((j == 0) | (j > 3))
    def _():
        o_ref[...] = acc.astype(o_ref.dtype)


def _in_proj(h, w_in, layer, cos, sin, qg, kg, *, seq, ncols, tm=1024, tn=512):
    n, d = h.shape
    tps = seq // tm
    return pl.pallas_call(
        _win_kernel,
        grid=(ncols // tn, n // tm),
        in_specs=[pl.BlockSpec((tm, d), lambda j, i: (i, 0)),
                  pl.BlockSpec((None, d, tn), lambda j, i: (layer, 0, j)),
                  pl.BlockSpec((tm, HEAD_DIM), lambda j, i: (i % tps, 0)),
                  pl.BlockSpec((tm, HEAD_DIM), lambda j, i: (i % tps, 0)),
                  pl.BlockSpec((1, HEAD_DIM), lambda j, i: (0, 0)),
                  pl.BlockSpec((1, HEAD_DIM), lambda j, i: (0, 0))],
        out_specs=pl.BlockSpec((tm, tn), lambda j, i: (i, j)),
        out_shape=jax.ShapeDtypeStruct((n, ncols), BF16),
        scratch_shapes=[pltpu.VMEM((d, tn), BF16)],
        compiler_params=_cparams(("arbitrary", "arbitrary")),
        name="in_proj",
    )(h, w_in, cos, sin, qg.reshape(1, HEAD_DIM), kg.reshape(1, HEAD_DIM))


def _up_kernel(h_ref, w1_ref, w3_ref, o_ref, w1b, w3b):
    @pl.when(pl.program_id(1) == 0)
    def _():
        w1b[...] = w1_ref[...].astype(BF16)
        w3b[...] = w3_ref[...].astype(BF16)

    h = h_ref[...]
    a1 = jnp.dot(h, w1b[...], preferred_element_type=F32)
    a3 = jnp.dot(h, w3b[...], preferred_element_type=F32)
    o_ref[...] = (_silu(a1) * a3).astype(o_ref.dtype)


def _swiglu_up(h, w1, w3, w_idx, *, tm, tn):
    n, d = h.shape
    f = w1.shape[-1]
    lead = (None,) * len(w_idx)
    wspec = pl.BlockSpec(lead + (d, tn), lambda j, i: tuple(w_idx) + (0, j))
    return pl.pallas_call(
        _up_kernel,
        grid=(f // tn, n // tm),
        in_specs=[pl.BlockSpec((tm, d), lambda j, i: (i, 0)), wspec, wspec],
        out_specs=pl.BlockSpec((tm, tn), lambda j, i: (i, j)),
        out_shape=jax.ShapeDtypeStruct((n, f), BF16),
        scratch_shapes=[pltpu.VMEM((d, tn), BF16)] * 2,
        compiler_params=_cparams(("arbitrary", "arbitrary")),
        name="swiglu_up",
    )(h, w1, w3)


def _halo_specs(tl, width, col_block, nl, batch):
    hb = tl // HALO
    nh = nl * hb
    last = batch * nh - 1
    cur = pl.BlockSpec((tl, width), lambda b, i: (b * nl + i, col_block))
    prev = pl.BlockSpec((HALO, width), lambda b, i: (jnp.maximum(b * nh + i * hb - 1, 0), col_block))
    nxt = pl.BlockSpec((HALO, width), lambda b, i: (jnp.minimum(b * nh + (i + 1) * hb, last), col_block))
    return prev, cur, nxt


def _with_halo(prev_ref, cur_ref, next_ref, nl):
    i = pl.program_id(1)
    prev = jnp.where(i == 0, 0.0, prev_ref[...].astype(F32))
    nxt = jnp.where(i == nl - 1, 0.0, next_ref[...].astype(F32))
    return jnp.concatenate([prev, cur_ref[...].astype(F32), nxt], axis=0)


def _pool_kernel(prev_ref, cur_ref, next_ref, w_ref, s_ref, o_ref, *, tl, nl, seq):
    u = _with_halo(prev_ref, cur_ref, next_ref, nl)
    rows = tl + 2 * HALO
    t = (pl.program_id(1) * tl + lax.broadcasted_iota(jnp.int32, (tl, LANES), 0)).astype(F32)
    for g, w in enumerate(POOL_WINDOWS):
        half = w // 2
        ug = u[:, g * LANES:(g + 1) * LANES]
        s = ug + pltpu.roll(ug, 1, 0)
        sh = 1
        while sh < half:
            s = pltpu.roll(s, sh, 0) + pltpu.roll(s, rows - sh, 0)
            sh *= 2
        cnt = jnp.minimum(t + (half - 1), seq - 1.0) - jnp.maximum(t - half, 0.0) + 1.0
        mean = s[HALO:HALO + tl] / cnt
        diff = mean - ug[HALO:HALO + tl]
        y = jnp.dot(diff.astype(BF16), w_ref[g], preferred_element_type=F32)
        o_ref[:, g * LANES:(g + 1) * LANES] = (y * s_ref[:, g * LANES:(g + 1) * LANES]).astype(o_ref.dtype)


def _pool_mixer(z, w_pool, scale, *, batch, seq, tl=512):
    n = z.shape[0]
    nl = seq // tl
    width = len(POOL_WINDOWS) * LANES
    prev, cur, nxt = _halo_specs(tl, width, 0, nl, batch)
    return pl.pallas_call(
        functools.partial(_pool_kernel, tl=tl, nl=nl, seq=seq),
        grid=(batch, nl),
        in_specs=[prev, cur, nxt,
                  pl.BlockSpec((len(POOL_WINDOWS), LANES, LANES), lambda b, i: (0, 0, 0)),
                  pl.BlockSpec((1, width), lambda b, i: (0, 0))],
        out_specs=pl.BlockSpec((tl, width), lambda b, i: (b * nl + i, 0)),
        out_shape=jax.ShapeDtypeStruct((n, width), BF16),
        compiler_params=_cparams(("arbitrary", "arbitrary")),
        name="pool_mixer",
    )(z, z, z, w_pool.astype(BF16), scale.reshape(1, width))


def _rope_tables(seq):
    rows = seq // GRID_W
    row = jnp.broadcast_to(jnp.arange(rows, dtype=F32)[:, None], (rows, GRID_W)).reshape(seq)
    col = jnp.broadcast_to(jnp.arange(GRID_W, dtype=F32)[None, :], (rows, GRID_W)).reshape(seq)
    half = HEAD_DIM // 4
    inv = ROPE_THETA ** (-jnp.arange(half, dtype=F32) / half)
    ar = row[:, None] * inv[None, :]
    ac = col[:, None] * inv[None, :]
    cos = jnp.concatenate([jnp.cos(ar), jnp.cos(ar), jnp.cos(ac), jnp.cos(ac)], axis=-1)
    sin = jnp.concatenate([-jnp.sin(ar), jnp.sin(ar), -jnp.sin(ac), jnp.sin(ac)], axis=-1)
    return cos, sin


def _attn_kernel(q_ref, k_ref, v_ref, o_ref, m_scr, l_scr, acc_scr, *, nk, tk):
    ki = pl.program_id(3)

    @pl.when(ki == 0)
    def _():
        m_scr[...] = jnp.full(m_scr.shape, -jnp.inf, F32)
        l_scr[...] = jnp.zeros(l_scr.shape, F32)
        acc_scr[...] = jnp.zeros(acc_scr.shape, F32)

    k = k_ref[...]
    v = v_ref[...]
    for g in range(Q_PER_KV):
        q = q_ref[:, g * HEAD_DIM:(g + 1) * HEAD_DIM]
        s = lax.dot_general(q, k, (((1,), (1,)), ((), ())), preferred_element_type=F32)
        m_prev = m_scr[g]
        m_next = jnp.maximum(m_prev, jnp.max(s, axis=1, keepdims=True))
        p = jnp.exp2(s - jnp.concatenate([m_next] * (tk // LANES), axis=1))
        alpha = jnp.exp2(m_prev - m_next)
        l_scr[g] = alpha * l_scr[g] + jnp.sum(p, axis=1, keepdims=True)
        acc_scr[g] = alpha * acc_scr[g] + jnp.dot(p.astype(BF16), v, preferred_element_type=F32)
        m_scr[g] = m_next

    @pl.when(ki == nk - 1)
    def _():
        for g in range(Q_PER_KV):
            o_ref[:, g * HEAD_DIM:(g + 1) * HEAD_DIM] = (acc_scr[g] / l_scr[g]).astype(o_ref.dtype)


def _attention(z, *, batch, seq, q_col, tq=512, tk=512):
    n = z.shape[0]
    nq, nk = seq // tq, seq // tk
    qw = Q_PER_KV * HEAD_DIM
    qblk = q_col // qw
    kcol = q_col // HEAD_DIM + N_Q_HEADS
    vcol = kcol + N_KV_HEADS
    return pl.pallas_call(
        functools.partial(_attn_kernel, nk=nk, tk=tk),
        grid=(batch, N_KV_HEADS, nq, nk),
        in_specs=[pl.BlockSpec((tq, qw), lambda b, h, qi, ki: (b * nq + qi, qblk + h)),
                  pl.BlockSpec((tk, HEAD_DIM), lambda b, h, qi, ki: (b * nk + ki, kcol + h)),
                  pl.BlockSpec((tk, HEAD_DIM), lambda b, h, qi, ki: (b * nk + ki, vcol + h))],
        out_specs=pl.BlockSpec((tq, qw), lambda b, h, qi, ki: (b * nq + qi, h)),
        out_shape=jax.ShapeDtypeStruct((n, N_Q_HEADS * HEAD_DIM), BF16),
        scratch_shapes=[pltpu.VMEM((Q_PER_KV, tq, HEAD_DIM), F32)] * 3,
        compiler_params=_cparams(("arbitrary", "arbitrary", "arbitrary", "arbitrary")),
        name="flash_attention",
    )(z, z, z)


def _mconv_kernel(prev_ref, cur_ref, next_ref, cw_ref, wq_ref, wk_ref, q_ref, k_ref, *, tl, nl):
    u = _with_halo(prev_ref, cur_ref, next_ref, nl)
    rows = tl + 2 * HALO
    acc = None
    for kk in range(M_CONV_W):
        sh = (M_CONV_W // 2 - kk) % rows
        tap = (pltpu.roll(u, sh, 0) if sh else u) * cw_ref[kk:kk + 1, :]
        acc = tap if acc is None else acc + tap
    uc = _silu(acc[HALO:HALO + tl]).astype(BF16)
    for h in range(N_M_HEADS):
        sl = slice(h * HEAD_DIM, (h + 1) * HEAD_DIM)
        q_ref[:, sl] = jnp.dot(uc[:, sl], wq_ref[h], preferred_element_type=F32).astype(q_ref.dtype)
        kh = jnp.dot(uc[:, sl], wk_ref[h], preferred_element_type=F32) * (HEAD_DIM ** -0.5)
        k_ref[:, sl] = kh.astype(k_ref.dtype)


def _mlstm_qk(z, conv_w, wq, wk, *, batch, seq, tl=512):
    n = z.shape[0]
    nl = seq // tl
    width = N_M_HEADS * HEAD_DIM
    prev, cur, nxt = _halo_specs(tl, width, 4, nl, batch)
    wspec = pl.BlockSpec((N_M_HEADS, HEAD_DIM, HEAD_DIM), lambda b, i: (0, 0, 0))
    ospec = pl.BlockSpec((tl, width), lambda b, i: (b * nl + i, 0))
    return pl.pallas_call(
        functools.partial(_mconv_kernel, tl=tl, nl=nl),
        grid=(batch, nl),
        in_specs=[prev, cur, nxt, pl.BlockSpec((M_CONV_W, width), lambda b, i: (0, 0)), wspec, wspec],
        out_specs=[ospec, ospec],
        out_shape=[jax.ShapeDtypeStruct((n, width), BF16)] * 2,
        compiler_params=_cparams(("arbitrary", "arbitrary")),
        name="mlstm_conv_qk",
    )(z, z, z, conv_w, wq.astype(BF16), wk.astype(BF16))


def _split3(x):
    x1 = x.astype(BF16)
    r = x - x1.astype(F32)
    x2 = r.astype(BF16)
    x3 = (r - x2.astype(F32)).astype(BF16)
    return x1, x2, x3


def _mscan_kernel(qf_ref, kf_ref, vf_ref, gf_ref, gtf_ref, qb_ref, kb_ref, vb_ref, gb_ref, gtb_ref,
                  bias_ref, biast_ref, hf_ref, hb_ref, s_scr, n_scr, m_scr):
    c = pl.program_id(0)

    @pl.when(c == 0)
    def _():
        s_scr[...] = jnp.zeros(s_scr.shape, F32)
        n_scr[...] = jnp.zeros(n_scr.shape, F32)
        m_scr[...] = jnp.zeros(m_scr.shape, F32)

    ch = M_CHUNK
    ti = lax.broadcasted_iota(jnp.int32, (ch, ch), 0)
    si = lax.broadcasted_iota(jnp.int32, (ch, ch), 1)
    nh = N_M_HEADS

    for b, rev in [(b, rev) for b in range(qf_ref.shape[0]) for rev in (False, True)]:
        q_ref, k_ref, v_ref, g_ref, gt_ref, h_ref = (
            (qb_ref, kb_ref, vb_ref, gb_ref, gtb_ref, hb_ref) if rev else
            (qf_ref, kf_ref, vf_ref, gf_ref, gtf_ref, hf_ref))
        causal = (si >= ti) if rev else (si <= ti)
        tri = jnp.where(causal, 1.0, 0.0).astype(BF16)
        tri_t = jnp.where((ti >= si) if rev else (ti <= si), 1.0, 0.0).astype(BF16)
        gates = g_ref[b] + bias_ref[...]
        gates_t = gt_ref[b] + biast_ref[...]
        lf = jax.nn.log_sigmoid(gates)
        lf_t = jax.nn.log_sigmoid(gates_t)
        cum = sum(jnp.dot(tri, part, preferred_element_type=F32) for part in _split3(lf))
        cum_t = sum(jnp.dot(part, tri_t, preferred_element_type=F32) for part in _split3(lf_t))
        last = 0 if rev else ch - 1
        for h in range(nh):
            idx = (2 * b + (1 if rev else 0)) * nh + h
            icol = (2 * nh if rev else 0) + h
            fcol = icol + nh
            sl = slice(h * HEAD_DIM, (h + 1) * HEAD_DIM)
            q = q_ref[b, :, sl]
            k = k_ref[b, :, sl]
            v = v_ref[b, :, sl]
            b_col = cum[:, fcol:fcol + 1]
            b_row = cum_t[fcol:fcol + 1, :]
            li_col = gates[:, icol:icol + 1]
            li_row = gates_t[icol:icol + 1, :]
            gtot = cum[last:last + 1, fcol:fcol + 1]
            m_prev = m_scr[idx][:, :1]
            n_prev = n_scr[idx]
            s_prev = s_scr[idx]

            dm = jnp.where(causal, b_col - b_row + li_row, -jnp.inf)
            m_inter = b_col + m_prev
            m_t = jnp.maximum(m_inter, jnp.max(dm, axis=1, keepdims=True))
            p = jnp.exp(dm - m_t)
            qk = lax.dot_general(q, k, (((1,), (1,)), ((), ())), preferred_element_type=F32)
            sc = qk * p
            w_inter = jnp.exp(m_inter - m_t)
            num = (jnp.dot(sc.astype(BF16), v, preferred_element_type=F32)
                   + w_inter * jnp.dot(q, s_prev.astype(BF16), preferred_element_type=F32))
            den = (jnp.sum(sc, axis=1, keepdims=True)
                   + w_inter * jnp.sum(q.astype(F32) * n_prev, axis=1, keepdims=True))
            h_ref[b, :, sl] = num / jnp.maximum(jnp.abs(den), jnp.exp(-m_t))

            w_st = gtot - b_col + li_col
            m_loc = jnp.max(w_st, axis=0, keepdims=True)
            ak = jnp.exp(w_st - m_loc) * k.astype(F32)
            s_c = lax.dot_general(ak.astype(BF16), v, (((0,), (0,)), ((), ())), preferred_element_type=F32)
            n_c = jnp.sum(ak, axis=0, keepdims=True)
            m_new = jnp.maximum(gtot + m_prev, m_loc)
            decay = jnp.exp(gtot + m_prev - m_new)
            add = jnp.exp(m_loc - m_new)
            s_scr[idx] = decay * s_prev + add * s_c
            n_scr[idx] = decay * n_prev + add * n_c
            m_scr[idx] = jnp.broadcast_to(m_new, (1, HEAD_DIM))


def _mlstm_scan(qm, km, z, gates, gates_t, gate_b, *, batch, seq):
    n = qm.shape[0]
    nc = seq // M_CHUNK
    width = N_M_HEADS * HEAD_DIM
    ng = 4 * N_M_HEADS
    chains = 2 * batch * N_M_HEADS
    fwd = lambda c: (0, c, 0)
    bwd = lambda c: (0, nc - 1 - c, 0)
    fwd_v = lambda c: (0, c, 5)
    bwd_v = lambda c: (0, nc - 1 - c, 5)
    fwd_t = lambda c: (0, 0, c)
    bwd_t = lambda c: (0, 0, nc - 1 - c)
    blk = (batch, M_CHUNK, width)
    gblk = (batch, M_CHUNK, LANES)
    tblk = (batch, ng, M_CHUNK)
    bias = jnp.zeros((1, LANES), F32).at[0, :ng].set(gate_b)
    bias_t = jnp.broadcast_to(gate_b[:, None], (ng, M_CHUNK))
    const = lambda c: (0, 0)
    in_specs = [
        pl.BlockSpec(blk, fwd), pl.BlockSpec(blk, fwd), pl.BlockSpec(blk, fwd_v),
        pl.BlockSpec(gblk, fwd), pl.BlockSpec(tblk, fwd_t),
        pl.BlockSpec(blk, bwd), pl.BlockSpec(blk, bwd), pl.BlockSpec(blk, bwd_v),
        pl.BlockSpec(gblk, bwd), pl.BlockSpec(tblk, bwd_t),
        pl.BlockSpec((1, LANES), const), pl.BlockSpec((ng, M_CHUNK), const),
    ]
    q3, k3, z3, g3 = (a.reshape(batch, seq, a.shape[1]) for a in (qm, km, z, gates))
    hf, hb = pl.pallas_call(
        _mscan_kernel,
        grid=(nc,),
        in_specs=in_specs,
        out_specs=[pl.BlockSpec(blk, fwd), pl.BlockSpec(blk, bwd)],
        out_shape=[jax.ShapeDtypeStruct((batch, seq, width), F32)] * 2,
        scratch_shapes=[pltpu.VMEM((chains, HEAD_DIM, HEAD_DIM), F32),
                        pltpu.VMEM((chains, 1, HEAD_DIM), F32),
                        pltpu.VMEM((chains, 1, HEAD_DIM), F32)],
        compiler_params=_cparams(("arbitrary",)),
        name="mlstm_scan",
    )(q3, k3, z3, g3, gates_t, q3, k3, z3, g3, gates_t, bias, bias_t)
    return hf.reshape(n, width), hb.reshape(n, width)


def _mout_kernel(hf_ref, hb_ref, o_ref, g_ref, y_ref):
    for h in range(N_M_HEADS):
        sl = slice(h * HEAD_DIM, (h + 1) * HEAD_DIM)
        x = hf_ref[:, sl] + hb_ref[:, sl]
        ms = jnp.mean(x * x, axis=-1, keepdims=True)
        hn = x * lax.rsqrt(ms + EPS) * g_ref[...]
        y_ref[:, sl] = (jax.nn.sigmoid(o_ref[:, sl].astype(F32)) * hn).astype(y_ref.dtype)


def _mlstm_out(hf, hb, z, norm_g, *, tm=1024):
    n, width = hf.shape
    spec = pl.BlockSpec((tm, width), lambda i: (i, 0))
    return pl.pallas_call(
        _mout_kernel,
        grid=(n // tm,),
        in_specs=[spec, spec, pl.BlockSpec((tm, width), lambda i: (i, 6)),
                  pl.BlockSpec((1, HEAD_DIM), lambda i: (0, 0))],
        out_specs=spec,
        out_shape=jax.ShapeDtypeStruct((n, width), BF16),
        compiler_params=_cparams(("arbitrary",)),
        name="mlstm_out",
    )(hf, hb, z, norm_g.reshape(1, HEAD_DIM))


def _top2(logit):
    lane = lax.broadcasted_iota(jnp.int32, logit.shape, 1).astype(F32)
    logit = jnp.where(lane < N_EXPERTS, logit, -jnp.inf)
    m1 = jnp.max(logit, axis=1, keepdims=True)
    i1 = jnp.min(jnp.where(logit == m1, lane, float(LANES)), axis=1, keepdims=True)
    rest = jnp.where(lane == i1, -jnp.inf, logit)
    m2 = jnp.max(rest, axis=1, keepdims=True)
    i2 = jnp.min(jnp.where(rest == m2, lane, float(LANES)), axis=1, keepdims=True)
    e = jnp.exp(m2 - m1)
    p1 = 1.0 / (1.0 + e)
    p2 = e / (1.0 + e)
    comb = jnp.where(lane == i1, p1, 0.0) + jnp.where(lane == i2, p2, 0.0)
    sel = jnp.where((lane == i1) | (lane == i2), 1.0, 0.0)
    return comb, sel


def _pack_halves(y):
    half = y.shape[1] // 2
    lo = pltpu.bitcast(y[:, :half].astype(BF16).astype(F32), jnp.uint32)
    hi = pltpu.bitcast(y[:, half:].astype(BF16).astype(F32), jnp.uint32)
    return hi | (lo >> 16)


def _unpack_halves(w):
    lo = pltpu.bitcast(w << 16, F32).astype(BF16)
    hi = pltpu.bitcast(w & jnp.uint32(0xFFFF0000), F32).astype(BF16)
    return lo, hi


def _mod_norm(x, g, sc=None, sh=None):
    ms = jnp.mean(x * x, axis=-1, keepdims=True)
    y = x * lax.rsqrt(ms + EPS) * g
    return y if sc is None else y * (1.0 + sc) + sh


def _out_proj_kernel(*refs, n_a, route):
    a_refs = refs[:n_a]
    w_ref, x_ref, g1_ref, ng_ref, sc_ref, sh_ref = refs[n_a:n_a + 6]
    if route:
        rw_ref, rb_ref, xo_ref, hp_ref, comb_ref, sel_ref, wb = refs[n_a + 6:]
    else:
        xo_ref, h_ref, wb = refs[n_a + 6:]

    @pl.when(pl.program_id(0) == 0)
    def _():
        wb[...] = w_ref[...].astype(BF16)

    acc, k0 = None, 0
    for a_ref in a_refs:
        kk = a_ref.shape[1]
        part = jnp.dot(a_ref[...], wb[k0:k0 + kk, :], preferred_element_type=F32)
        acc = part if acc is None else acc + part
        k0 += kk
    xn = x_ref[...] + g1_ref[0] * acc
    xo_ref[...] = xn
    y = _mod_norm(xn, ng_ref[...], sc_ref[0], sh_ref[0])
    if route:
        hp_ref[...] = _pack_halves(y)
        y_hi = y.astype(BF16)
        y_lo = (y - y_hi.astype(F32)).astype(BF16)
        logit = (jnp.dot(y_hi, rw_ref[0], preferred_element_type=F32)
                 + jnp.dot(y_lo, rw_ref[0], preferred_element_type=F32)
                 + jnp.dot(y_hi, rw_ref[1], preferred_element_type=F32)) + rb_ref[...]
        comb_ref[...], sel_ref[...] = _top2(logit)
    else:
        h_ref[...] = y.astype(h_ref.dtype)


def _out_proj(a_list, w_out, layer, x, g1, norm_g, sc, sh, router=None, *, rows_per_batch, tm=512):
    n, d = x.shape
    tpb = rows_per_batch // tm
    route = router is not None
    row = lambda w: pl.BlockSpec((tm, w), lambda i: (i, 0))
    per_batch = pl.BlockSpec((1, 1, d), lambda i: (i // tpb, 0, 0))
    const = lambda shape: pl.BlockSpec(shape, lambda i: (0,) * len(shape))
    in_specs = [row(a.shape[1]) for a in a_list]
    in_specs += [pl.BlockSpec((None, d, d), lambda i: (layer, 0, 0), pipeline_mode=pl.Buffered(1)), row(d), per_batch,
                 const((1, d)), per_batch, per_batch]
    args = list(a_list) + [w_out, x, g1, norm_g.reshape(1, d), sc, sh]
    out_specs = [row(d)]
    out_shape = [jax.ShapeDtypeStruct((n, d), F32)]
    if route:
        router_w, router_b = router
        rw = _pad_cols(router_w, LANES)
        rw_hi = rw.astype(BF16)
        rw_split = jnp.stack([rw_hi, (rw - rw_hi.astype(F32)).astype(BF16)])
        in_specs += [const((2, d, LANES)), const((1, LANES))]
        args += [rw_split, jnp.zeros((1, LANES), F32).at[0, :N_EXPERTS].set(router_b)]
        out_specs += [row(d // 2), row(LANES), row(LANES)]
        out_shape += [jax.ShapeDtypeStruct((n, d // 2), jnp.uint32), jax.ShapeDtypeStruct((n, LANES), F32),
                      jax.ShapeDtypeStruct((n, LANES), F32)]
    else:
        out_specs.append(row(d))
        out_shape.append(jax.ShapeDtypeStruct((n, d), BF16))
    return pl.pallas_call(
        functools.partial(_out_proj_kernel, n_a=len(a_list), route=route),
        grid=(n // tm,),
        in_specs=in_specs,
        out_specs=out_specs,
        out_shape=out_shape,
        scratch_shapes=[pltpu.VMEM((d, d), BF16)],
        compiler_params=_cparams(("arbitrary",)),
        name="out_proj_route" if route else "out_proj",
    )(*args)


def _rank_kernel(sel_ref, rank_ref, tot_ref, carry):
    @pl.when(pl.program_id(0) == 0)
    def _():
        carry[...] = jnp.zeros(carry.shape, F32)

    sel = sel_ref[...]
    tb = sel.shape[0]
    r = lax.broadcasted_iota(jnp.int32, (tb, tb), 0)
    c = lax.broadcasted_iota(jnp.int32, (tb, tb), 1)
    tri = jnp.where(r >= c, 1.0, 0.0).astype(BF16)
    incl = jnp.dot(tri, sel.astype(BF16), preferred_element_type=F32)
    rank_ref[...] = incl - sel + carry[...]
    total = carry[...] + incl[tb - 1:tb, :]
    carry[...] = total
    tot_ref[...] = total


def _rank(sel, *, tb=512):
    n = sel.shape[0]
    return pl.pallas_call(
        _rank_kernel,
        grid=(n // tb,),
        in_specs=[pl.BlockSpec((tb, LANES), lambda i: (i, 0))],
        out_specs=[pl.BlockSpec((tb, LANES), lambda i: (i, 0)), pl.BlockSpec((1, LANES), lambda i: (0, 0))],
        out_shape=[jax.ShapeDtypeStruct((n, LANES), F32), jax.ShapeDtypeStruct((1, LANES), F32)],
        scratch_shapes=[pltpu.VMEM((1, LANES), F32)],
        compiler_params=_cparams(("arbitrary",)),
        name="moe_rank",
    )(sel)


def _dest_kernel(rank_ref, sel_ref, comb_ref, off_ref, dest_ref, p_ref):
    lane = lax.broadcasted_iota(jnp.int32, rank_ref.shape, 1).astype(F32)
    sel = sel_ref[...] > 0.5
    pos = rank_ref[...] + off_ref[...]
    la = jnp.min(jnp.where(sel, lane, float(LANES)), axis=1, keepdims=True)
    lb = jnp.max(jnp.where(sel, lane, -1.0), axis=1, keepdims=True)
    pick = lambda l, v: jnp.sum(jnp.where(lane == l, v, 0.0), axis=1, keepdims=True)
    two = lambda a, b: jnp.where(lane == 0.0, a, jnp.where(lane == 1.0, b, 0.0))
    dest_ref[...] = two(pick(la, pos), pick(lb, pos)).astype(jnp.int32)
    p_ref[...] = two(pick(la, comb_ref[...]), pick(lb, comb_ref[...]))


def _dest(rank, sel, comb, off, *, tb=1024):
    n = rank.shape[0]
    spec = pl.BlockSpec((tb, LANES), lambda i: (i, 0))
    return pl.pallas_call(
        _dest_kernel,
        grid=(n // tb,),
        in_specs=[spec, spec, spec, pl.BlockSpec((1, LANES), lambda i: (0, 0))],
        out_specs=[spec, spec],
        out_shape=[jax.ShapeDtypeStruct((n, LANES), jnp.int32), jax.ShapeDtypeStruct((n, LANES), F32)],
        compiler_params=_cparams(("arbitrary",)),
        name="moe_dest",
    )(rank, sel, comb, off)


def _row_copy(src, src_row, dst, dst_row, sem):
    return pltpu.make_async_copy(src.at[pl.ds(src_row, 1), :], dst.at[pl.ds(dst_row, 1), :], sem)


def _dispatch_kernel(dest_ref, hp_ref, xs_in_ref, xs_ref, sem):
    del xs_in_ref
    tb = hp_ref.shape[0]
    for r in range(tb):
        for s in range(2):
            _row_copy(hp_ref, r, xs_ref, dest_ref[s, r], sem).start(priority=s)
    for s in range(2):
        pltpu.make_async_copy(hp_ref, xs_ref.at[pl.ds(0, tb), :], sem).wait()


def _dispatch(hp, dest3, rows, *, tb):
    n, w = hp.shape
    xs0 = jnp.zeros((rows, w), hp.dtype)
    return pl.pallas_call(
        _dispatch_kernel,
        grid=(n // tb,),
        in_specs=[pl.BlockSpec((None, 2, tb), lambda i: (i, 0, 0), memory_space=pltpu.SMEM),
                  pl.BlockSpec((tb, w), lambda i: (i, 0)),
                  pl.BlockSpec(memory_space=pl.ANY)],
        out_specs=pl.BlockSpec(memory_space=pl.ANY),
        out_shape=jax.ShapeDtypeStruct((rows, w), hp.dtype),
        scratch_shapes=[pltpu.SemaphoreType.DMA(())],
        input_output_aliases={2: 0},
        compiler_params=_cparams(("arbitrary",)),
        name="moe_dispatch",
    )(dest3, hp, xs0)


def _expert_changed(te_ref, t):
    return (t == 0) | (te_ref[t] != te_ref[jnp.maximum(t - 1, 0)])


def _moe_up_kernel(te_ref, nu_ref, xs_ref, w1_ref, w3_ref, o_ref, w1b, w3b):
    t = pl.program_id(1)

    @pl.when(_expert_changed(te_ref, t))
    def _():
        w1b[...] = w1_ref[...].astype(BF16)
        w3b[...] = w3_ref[...].astype(BF16)

    @pl.when(t < nu_ref[0])
    def _():
        x = jnp.concatenate(_unpack_halves(xs_ref[...]), axis=1)
        a1 = jnp.dot(x, w1b[...], preferred_element_type=F32)
        a3 = jnp.dot(x, w3b[...], preferred_element_type=F32)
        o_ref[...] = (_silu(a1) * a3).astype(o_ref.dtype)

    @pl.when(t >= nu_ref[0])
    def _():
        o_ref[...] = jnp.zeros(o_ref.shape, o_ref.dtype)


def _moe_up(xs, w1, w3, layer, te, nu, *, tg, tn):
    rows, half = xs.shape
    _, e, d, f = w1.shape
    wspec = pl.BlockSpec((None, None, d, tn), lambda j, t, te, nu: (layer, te[t], 0, j))
    return pl.pallas_call(
        _moe_up_kernel,
        grid_spec=pltpu.PrefetchScalarGridSpec(
            num_scalar_prefetch=2,
            grid=(f // tn, rows // tg),
            in_specs=[pl.BlockSpec((tg, half), lambda j, t, te, nu: (t, 0)), wspec, wspec],
            out_specs=pl.BlockSpec((tg, tn), lambda j, t, te, nu: (t, j)),
            scratch_shapes=[pltpu.VMEM((d, tn), BF16)] * 2),
        out_shape=jax.ShapeDtypeStruct((rows, f), BF16),
        compiler_params=_cparams(("arbitrary", "arbitrary")),
        name="moe_up",
    )(te, nu, xs, w1, w3)


def _moe_down_kernel(te_ref, nu_ref, a_ref, w2_ref, o_ref, w2b):
    t = pl.program_id(1)

    @pl.when(_expert_changed(te_ref, t))
    def _():
        w2b[...] = w2_ref[...].astype(BF16)

    @pl.when(t < nu_ref[0])
    def _():
        o_ref[...] = _pack_halves(jnp.dot(a_ref[...], w2b[...], preferred_element_type=F32))

    @pl.when(t >= nu_ref[0])
    def _():
        o_ref[...] = jnp.zeros(o_ref.shape, o_ref.dtype)


def _moe_down(a, w2, layer, te, nu, *, tg, tn):
    rows, f = a.shape
    d = w2.shape[-1]
    return pl.pallas_call(
        _moe_down_kernel,
        grid_spec=pltpu.PrefetchScalarGridSpec(
            num_scalar_prefetch=2,
            grid=(d // tn, rows // tg),
            in_specs=[pl.BlockSpec((tg, f), lambda j, t, te, nu: (t, 0)),
                      pl.BlockSpec((None, None, f, tn), lambda j, t, te, nu: (layer, te[t], 0, j))],
            out_specs=pl.BlockSpec((tg, tn // 2), lambda j, t, te, nu: (t, j)),
            scratch_shapes=[pltpu.VMEM((f, tn), BF16)]),
        out_shape=jax.ShapeDtypeStruct((rows, d // 2), jnp.uint32),
        compiler_params=_cparams(("arbitrary", "arbitrary")),
        name="moe_down",
    )(te, nu, a, w2)


def _combine_kernel(*refs, modulate, pack_w):
    if modulate:
        dest_ref, x_ref, g_ref, p_ref, ng_ref, sc_ref, sh_ref, ys_ref, o_ref, h_ref, buf, sem = refs
    else:
        dest_ref, x_ref, g_ref, p_ref, ng_ref, ys_ref, o_ref, h_ref, buf, sem = refs
    tb = x_ref.shape[0]
    for r in range(tb):
        for s in range(2):
            _row_copy(ys_ref, dest_ref[s, r], buf.at[s], r, sem).start(priority=s)
    for s in range(2):
        pltpu.make_async_copy(ys_ref.at[pl.ds(0, tb), :], buf.at[s], sem).wait()

    def expert_rows(s):
        w = buf[s]
        parts = []
        for c in range(w.shape[1] // pack_w):
            wc = w[:, c * pack_w:(c + 1) * pack_w]
            parts += [pltpu.bitcast(wc << 16, F32), pltpu.bitcast(wc & jnp.uint32(0xFFFF0000), F32)]
        return jnp.concatenate(parts, axis=1)

    p = p_ref[...]
    xn = x_ref[...] + g_ref[0] * (p[:, 0:1] * expert_rows(0) + p[:, 1:2] * expert_rows(1))
    o_ref[...] = xn
    if modulate:
        h_ref[...] = _mod_norm(xn, ng_ref[...], sc_ref[0], sh_ref[0]).astype(h_ref.dtype)
    else:
        h_ref[...] = _mod_norm(xn, ng_ref[...]).astype(h_ref.dtype)


def _combine(x, g, p, dest3, ys, norm_g, sc, sh, *, rows_per_batch, tb, h_dtype, pack_w):
    n, d = x.shape
    tpb = rows_per_batch // tb
    modulate = sc is not None
    row = pl.BlockSpec((tb, d), lambda i: (i, 0))
    per_batch = pl.BlockSpec((1, 1, d), lambda i: (i // tpb, 0, 0))
    in_specs = [pl.BlockSpec((None, 2, tb), lambda i: (i, 0, 0), memory_space=pltpu.SMEM), row, per_batch,
                pl.BlockSpec((tb, LANES), lambda i: (i, 0)), pl.BlockSpec((1, d), lambda i: (0, 0))]
    args = [dest3, x, g, p, norm_g.reshape(1, d)]
    if modulate:
        in_specs += [per_batch, per_batch]
        args += [sc, sh]
    return pl.pallas_call(
        functools.partial(_combine_kernel, modulate=modulate, pack_w=pack_w),
        grid=(n // tb,),
        in_specs=in_specs + [pl.BlockSpec(memory_space=pl.ANY)],
        out_specs=[row, row],
        out_shape=[jax.ShapeDtypeStruct((n, d), F32), jax.ShapeDtypeStruct((n, d), h_dtype)],
        scratch_shapes=[pltpu.VMEM((2, tb, d // 2), jnp.uint32), pltpu.SemaphoreType.DMA(())],
        compiler_params=_cparams(("arbitrary",)),
        name="moe_combine",
    )(*args, ys)


def _moe_ffn(x, hp, comb, sel, g, w1, w3, w2, layer, next_norm, *, rows_per_batch, h_dtype, tg=512, tb=256,
             tn_up=896, tn_down=1024):
    n, d = x.shape
    e = w1.shape[1]
    t_max = 2 * n // tg + e
    rank, tot = _rank(sel)
    cnt = tot[0, :e].astype(jnp.int32)
    tiles = (cnt + tg - 1) // tg
    tile_end = jnp.cumsum(tiles)
    n_used = tile_end[-1]
    off = jnp.zeros((1, LANES), F32).at[0, :e].set(((tile_end - tiles) * tg).astype(F32))
    te = jnp.sum(jnp.arange(t_max, dtype=jnp.int32)[:, None] >= tile_end[None, :], axis=1).astype(jnp.int32)
    te = jnp.minimum(te, te[jnp.maximum(n_used - 1, 0)])
    nu = n_used.reshape(1).astype(jnp.int32)
    dest, p = _dest(rank, sel, comb, off)
    dest3 = dest[:, :2].T.reshape(2, n // tb, tb).transpose(1, 0, 2)
    xs = _dispatch(hp, dest3, t_max * tg, tb=tb)
    a = _moe_up(xs, w1, w3, layer, te, nu, tg=tg, tn=tn_up)
    ys = _moe_down(a, w2, layer, te, nu, tg=tg, tn=tn_down)
    return _combine(x, g, p, dest3, ys, *next_norm, rows_per_batch=rows_per_batch, tb=tb, h_dtype=h_dtype,
                    pack_w=tn_down // 2)


def _pad_cols(w, width):
    return jnp.zeros((w.shape[0], width), w.dtype).at[:, :w.shape[1]].set(w)


def kernel(x, c, ada_w, ada_b, norm1_g, norm2_g, w_in, w_out, pool_w, pool_scale, q_norm_g, k_norm_g,
           m_conv_w, m_wq, m_wk, m_gate_b, m_norm_g, ffn_w1, ffn_w3, ffn_w2, router_w, router_b,
           moe_w1, moe_w3, moe_w2, final_norm_g):
    batch, seq, d = x.shape
    depth = ada_w.shape[0]
    n = batch * seq
    ng = 4 * N_M_HEADS
    d_main = w_in.shape[2] - ng
    xf = x.reshape(n, d)
    mod = _adaln_mod(c, ada_w, ada_b)
    cos, sin = _rope_tables(seq)
    dims = dict(batch=batch, seq=seq)

    mods = [[mod[l, :, i * d:(i + 1) * d].reshape(batch, 1, d) for i in range(6)] for l in range(depth)]
    h = None
    for l in range(depth):
        sh1, sc1, g1, sh2, sc2, g2 = mods[l]
        if h is None:
            h = _norm(xf, norm1_g[l], sc1, sh1, rows_per_batch=seq, out_dtype=BF16)
        z = _in_proj(h, w_in, l, cos, sin, q_norm_g[l], k_norm_g[l], seq=seq, ncols=d_main)
        gates = _mm([h], w_in, (l,), tm=1024, tn=LANES, ncols=LANES, col_blk0=d_main // LANES, valid_cols=ng)
        y_pool = _pool_mixer(z, pool_w[l], pool_scale[l], **dims)
        y_attn = _attention(z, q_col=pool_w.shape[1] * pool_w.shape[2], tq=4096, tk=256, **dims)
        qm, km = _mlstm_qk(z, m_conv_w[l], m_wq[l], m_wk[l], **dims)
        gates_t = gates[:, :ng].reshape(batch, seq, ng).transpose(0, 2, 1)
        hf, hb = _mlstm_scan(qm, km, z, gates, gates_t, m_gate_b[l], **dims)
        y_m = _mlstm_out(hf, hb, z, m_norm_g[l])
        mixed = [y_pool, y_attn, y_m]
        i = l // 2
        if l % 2 == 0:
            xf, h2 = _out_proj(mixed, w_out, l, xf, g1, norm2_g[l], sc2, sh2, rows_per_batch=seq)
            a = _swiglu_up(h2, ffn_w1, ffn_w3, (i,), tm=1024, tn=512)
            xf = _mm([a], ffn_w2, (i,), tm=512, tn=512, x=xf, g=g2, rows_per_batch=seq)
            h = None
        else:
            xf, hp, comb, sel = _out_proj(mixed, w_out, l, xf, g1, norm2_g[l], sc2, sh2,
                                          (router_w[i], router_b[i]), rows_per_batch=seq)
            last = l == depth - 1
            next_norm = (final_norm_g, None, None) if last else (norm1_g[l + 1], mods[l + 1][1], mods[l + 1][0])
            xf, h = _moe_ffn(xf, hp, comb, sel, g2, moe_w1, moe_w3, moe_w2, i, next_norm, rows_per_batch=seq,
                             h_dtype=F32 if last else BF16)
            if last:
                return h.reshape(batch, seq, d)

    return _norm(xf, final_norm_g, rows_per_batch=seq, out_dtype=F32).reshape(batch, seq, d)
```

```python
import functools
import math

import numpy as np
import jax
import jax.numpy as jnp
from jax import lax
from jax.experimental import pallas as pl
from jax.experimental.pallas import tpu as pltpu

F32 = jnp.float32
BF16 = jnp.bfloat16

EPS = 1e-6
HEAD_DIM = 128
GRID_W = 64
ROPE_THETA = 10000.0
POOL_WINDOWS = (2, 4, 8, 16)
M_CHUNK = 128
M_CONV_W = 5
N_M_HEADS = 4
N_Q_HEADS = 8
N_KV_HEADS = 2
Q_PER_KV = N_Q_HEADS // N_KV_HEADS
N_EXPERTS = 8
LANES = 128
HALO = 16
VMEM_LIMIT = 56 * 1024 * 1024
LOG2E = math.log2(math.e)


def _cparams(sem, flags=None):
    return pltpu.CompilerParams(dimension_semantics=sem, vmem_limit_bytes=VMEM_LIMIT, flags=flags)


def _silu(a):
    return a * jax.nn.sigmoid(a)


def _mod_kernel(c_ref, w_ref, b_ref, o_ref):
    ca = _silu(c_ref[...])
    o_ref[0] = jnp.dot(ca.astype(BF16), w_ref[0].astype(BF16), preferred_element_type=F32) + b_ref[0]


def _adaln_mod(c, ada_w, ada_b, tn=768):
    depth, d, n6 = ada_w.shape
    b = c.shape[0]
    cp = jnp.zeros((8, d), F32).at[:b].set(c)
    out = pl.pallas_call(
        _mod_kernel,
        grid=(depth, n6 // tn),
        in_specs=[
            pl.BlockSpec((8, d), lambda l, j: (0, 0)),
            pl.BlockSpec((1, d, tn), lambda l, j: (l, 0, j)),
            pl.BlockSpec((1, 1, tn), lambda l, j: (l, 0, j)),
        ],
        out_specs=pl.BlockSpec((1, 8, tn), lambda l, j: (l, 0, j)),
        out_shape=jax.ShapeDtypeStruct((depth, 8, n6), F32),
        compiler_params=_cparams(("arbitrary", "arbitrary")),
        name="adaln_mod",
    )(cp, ada_w, ada_b.reshape(depth, 1, n6))
    return out[:, :b]


def _norm_kernel(*refs, modulate):
    if modulate:
        x_ref, g_ref, sc_ref, sh_ref, o_ref = refs
    else:
        x_ref, g_ref, o_ref = refs
    x = x_ref[...]
    ms = jnp.mean(x * x, axis=-1, keepdims=True)
    y = x * lax.rsqrt(ms + EPS) * g_ref[...]
    if modulate:
        y = y * (1.0 + sc_ref[0]) + sh_ref[0]
    o_ref[...] = y.astype(o_ref.dtype)


def _norm(x, g, sc=None, sh=None, *, rows_per_batch, out_dtype, tm=1024):
    n, d = x.shape
    tpb = rows_per_batch // tm
    modulate = sc is not None
    in_specs = [pl.BlockSpec((tm, d), lambda i: (i, 0)), pl.BlockSpec((1, d), lambda i: (0, 0))]
    args = [x, g.reshape(1, d)]
    if modulate:
        in_specs += [pl.BlockSpec((1, 1, d), lambda i: (i // tpb, 0, 0))] * 2
        args += [sc, sh]
    return pl.pallas_call(
        functools.partial(_norm_kernel, modulate=modulate),
        grid=(n // tm,),
        in_specs=in_specs,
        out_specs=pl.BlockSpec((tm, d), lambda i: (i, 0)),
        out_shape=jax.ShapeDtypeStruct((n, d), out_dtype),
        compiler_params=_cparams(("arbitrary",)),
        name="rms_norm",
    )(*args)


def _mm_kernel(*refs, n_a, resid, valid_cols):
    a_refs, w_ref = refs[:n_a], refs[n_a]
    if resid:
        x_ref, g_ref, o_ref, wb = refs[n_a + 1:]
    else:
        o_ref, wb = refs[n_a + 1:]

    @pl.when(pl.program_id(1) == 0)
    def _():
        w = w_ref[...]
        if valid_cols is not None:
            w = jnp.where(lax.broadcasted_iota(jnp.int32, w.shape, 1) < valid_cols, w, 0.0)
        wb[...] = w.astype(BF16)

    acc, k0 = None, 0
    for a_ref in a_refs:
        kk = a_ref.shape[1]
        part = jnp.dot(a_ref[...], wb[k0:k0 + kk, :], preferred_element_type=F32)
        acc = part if acc is None else acc + part
        k0 += kk
    if resid:
        o_ref[...] = x_ref[...] + g_ref[0] * acc
    else:
        o_ref[...] = acc.astype(o_ref.dtype)


def _mm(a_list, w, w_idx, *, tm, tn, ncols=None, col_blk0=0, valid_cols=None, out_dtype=F32,
        x=None, g=None, rows_per_batch=None):
    m = a_list[0].shape[0]
    kdim = w.shape[-2]
    ncols = w.shape[-1] if ncols is None else ncols
    resid = x is not None
    lead = (None,) * len(w_idx)
    in_specs = [pl.BlockSpec((tm, a.shape[1]), lambda j, i: (i, 0)) for a in a_list]
    in_specs.append(pl.BlockSpec(lead + (kdim, tn), lambda j, i: tuple(w_idx) + (0, j + col_blk0)))
    args = list(a_list) + [w]
    if resid:
        tpb = rows_per_batch // tm
        in_specs += [pl.BlockSpec((tm, tn), lambda j, i: (i, j)),
                     pl.BlockSpec((1, 1, tn), lambda j, i: (i // tpb, 0, j))]
        args += [x, g]
    return pl.pallas_call(
        functools.partial(_mm_kernel, n_a=len(a_list), resid=resid, valid_cols=valid_cols),
        grid=(ncols // tn, m // tm),
        in_specs=in_specs,
        out_specs=pl.BlockSpec((tm, tn), lambda j, i: (i, j)),
        out_shape=jax.ShapeDtypeStruct((m, ncols), out_dtype),
        scratch_shapes=[pltpu.VMEM((kdim, tn), BF16)],
        compiler_params=_cparams(("arbitrary", "arbitrary")),
        name="matmul_resid" if resid else "matmul",
    )(*args)


def _head_norm_rope(x, g, cos, sin, ones, perm, scale):
    ss = jnp.dot((x * x).astype(BF16), ones, preferred_element_type=F32)
    y = x * lax.rsqrt(ss * (1.0 / HEAD_DIM) + EPS) * g
    rot = jnp.dot(y.astype(BF16), perm, preferred_element_type=F32)
    return (y * cos + rot * sin) * scale


def _win_kernel(h_ref, w_ref, cos_ref, sin_ref, qg_ref, kg_ref, o_ref, wb):
    j = pl.program_id(0)

    @pl.when(pl.program_id(1) == 0)
    def _():
        wb[...] = w_ref[...].astype(BF16)

    acc = jnp.dot(h_ref[...], wb[...], preferred_element_type=F32)

    def store_roped(g_ref, scale, n_rope):
        cos = cos_ref[...]
        sin = sin_ref[...]
        src = lax.broadcasted_iota(jnp.int32, (HEAD_DIM, HEAD_DIM), 0)
        dst = lax.broadcasted_iota(jnp.int32, (HEAD_DIM, HEAD_DIM), 1)
        quarter = HEAD_DIM // 4
        partner = jnp.where((dst % (2 * quarter)) < quarter, dst + quarter, dst - quarter)
        perm = jnp.where(src == partner, 1.0, 0.0).astype(BF16)
        ones = jnp.ones((HEAD_DIM, HEAD_DIM), BF16)
        for hh in range(acc.shape[1] // HEAD_DIM):
            sl = slice(hh * HEAD_DIM, (hh + 1) * HEAD_DIM)
            xh = acc[:, sl]
            if hh < n_rope:
                xh = _head_norm_rope(xh, g_ref[...], cos, sin, ones, perm, scale)
            o_ref[:, sl] = xh.astype(o_ref.dtype)

    @pl.when((j == 1) | (j == 2))
    def _():
        store_roped(qg_ref, LOG2E / math.sqrt(HEAD_DIM), 4)

    @pl.when(j == 3)
    def _():
        store_roped(kg_ref, 1.0, N_KV_HEADS)

    @pl.when((j == 0) | (j > 3))
    def _():
        o_ref[...] = acc.astype(o_ref.dtype)


def _in_proj(h, w_in, layer, cos, sin, qg, kg, *, seq, ncols, tm=1024, tn=512):
    n, d = h.shape
    tps = seq // tm
    return pl.pallas_call(
        _win_kernel,
        grid=(ncols // tn, n // tm),
        in_specs=[pl.BlockSpec((tm, d), lambda j, i: (i, 0)),
                  pl.BlockSpec((None, d, tn), lambda j, i: (layer, 0, j)),
                  pl.BlockSpec((tm, HEAD_DIM), lambda j, i: (i % tps, 0)),
                  pl.BlockSpec((tm, HEAD_DIM), lambda j, i: (i % tps, 0)),
                  pl.BlockSpec((1, HEAD_DIM), lambda j, i: (0, 0)),
                  pl.BlockSpec((1, HEAD_DIM), lambda j, i: (0, 0))],
        out_specs=pl.BlockSpec((tm, tn), lambda j, i: (i, j)),
        out_shape=jax.ShapeDtypeStruct((n, ncols), BF16),
        scratch_shapes=[pltpu.VMEM((d, tn), BF16)],
        compiler_params=_cparams(("arbitrary", "arbitrary")),
        name="in_proj",
    )(h, w_in, cos, sin, qg.reshape(1, HEAD_DIM), kg.reshape(1, HEAD_DIM))


def _up_kernel(h_ref, w1_ref, w3_ref, o_ref, w1b, w3b):
    @pl.when(pl.program_id(1) == 0)
    def _():
        w1b[...] = w1_ref[...].astype(BF16)
        w3b[...] = w3_ref[...].astype(BF16)

    h = h_ref[...]
    a1 = jnp.dot(h, w1b[...], preferred_element_type=F32)
    a3 = jnp.dot(h, w3b[...], preferred_element_type=F32)
    o_ref[...] = (_silu(a1) * a3).astype(o_ref.dtype)


def _swiglu_up(h, w1, w3, w_idx, *, tm, tn):
    n, d = h.shape
    f = w1.shape[-1]
    lead = (None,) * len(w_idx)
    wspec = pl.BlockSpec(lead + (d, tn), lambda j, i: tuple(w_idx) + (0, j))
    return pl.pallas_call(
        _up_kernel,
        grid=(f // tn, n // tm),
        in_specs=[pl.BlockSpec((tm, d), lambda j, i: (i, 0)), wspec, wspec],
        out_specs=pl.BlockSpec((tm, tn), lambda j, i: (i, j)),
        out_shape=jax.ShapeDtypeStruct((n, f), BF16),
        scratch_shapes=[pltpu.VMEM((d, tn), BF16)] * 2,
        compiler_params=_cparams(("arbitrary", "arbitrary")),
        name="swiglu_up",
    )(h, w1, w3)


def _halo_specs(tl, width, col_block, nl, batch):
    hb = tl // HALO
    nh = nl * hb
    last = batch * nh - 1
    cur = pl.BlockSpec((tl, width), lambda b, i: (b * nl + i, col_block))
    prev = pl.BlockSpec((HALO, width), lambda b, i: (jnp.maximum(b * nh + i * hb - 1, 0), col_block))
    nxt = pl.BlockSpec((HALO, width), lambda b, i: (jnp.minimum(b * nh + (i + 1) * hb, last), col_block))
    return prev, cur, nxt


def _with_halo(prev_ref, cur_ref, next_ref, nl):
    i = pl.program_id(1)
    prev = jnp.where(i == 0, 0.0, prev_ref[...].astype(F32))
    nxt = jnp.where(i == nl - 1, 0.0, next_ref[...].astype(F32))
    return jnp.concatenate([prev, cur_ref[...].astype(F32), nxt], axis=0)


def _pool_kernel(prev_ref, cur_ref, next_ref, w_ref, s_ref, o_ref, *, tl, nl, seq):
    u = _with_halo(prev_ref, cur_ref, next_ref, nl)
    rows = tl + 2 * HALO
    t = (pl.program_id(1) * tl + lax.broadcasted_iota(jnp.int32, (tl, LANES), 0)).astype(F32)
    for g, w in enumerate(POOL_WINDOWS):
        half = w // 2
        ug = u[:, g * LANES:(g + 1) * LANES]
        s = ug + pltpu.roll(ug, 1, 0)
        sh = 1
        while sh < half:
            s = pltpu.roll(s, sh, 0) + pltpu.roll(s, rows - sh, 0)
            sh *= 2
        cnt = jnp.minimum(t + (half - 1), seq - 1.0) - jnp.maximum(t - half, 0.0) + 1.0
        mean = s[HALO:HALO + tl] / cnt
        diff = mean - ug[HALO:HALO + tl]
        y = jnp.dot(diff.astype(BF16), w_ref[g], preferred_element_type=F32)
        o_ref[:, g * LANES:(g + 1) * LANES] = (y * s_ref[:, g * LANES:(g + 1) * LANES]).astype(o_ref.dtype)


def _pool_mixer(z, w_pool, scale, *, batch, seq, tl=512):
    n = z.shape[0]
    nl = seq // tl
    width = len(POOL_WINDOWS) * LANES
    prev, cur, nxt = _halo_specs(tl, width, 0, nl, batch)
    return pl.pallas_call(
        functools.partial(_pool_kernel, tl=tl, nl=nl, seq=seq),
        grid=(batch, nl),
        in_specs=[prev, cur, nxt,
                  pl.BlockSpec((len(POOL_WINDOWS), LANES, LANES), lambda b, i: (0, 0, 0)),
                  pl.BlockSpec((1, width), lambda b, i: (0, 0))],
        out_specs=pl.BlockSpec((tl, width), lambda b, i: (b * nl + i, 0)),
        out_shape=jax.ShapeDtypeStruct((n, width), BF16),
        compiler_params=_cparams(("arbitrary", "arbitrary")),
        name="pool_mixer",
    )(z, z, z, w_pool.astype(BF16), scale.reshape(1, width))


def _rope_tables(seq):
    rows = seq // GRID_W
    row = jnp.broadcast_to(jnp.arange(rows, dtype=F32)[:, None], (rows, GRID_W)).reshape(seq)
    col = jnp.broadcast_to(jnp.arange(GRID_W, dtype=F32)[None, :], (rows, GRID_W)).reshape(seq)
    half = HEAD_DIM // 4
    inv = ROPE_THETA ** (-jnp.arange(half, dtype=F32) / half)
    ar = row[:, None] * inv[None, :]
    ac = col[:, None] * inv[None, :]
    cos = jnp.concatenate([jnp.cos(ar), jnp.cos(ar), jnp.cos(ac), jnp.cos(ac)], axis=-1)
    sin = jnp.concatenate([-jnp.sin(ar), jnp.sin(ar), -jnp.sin(ac), jnp.sin(ac)], axis=-1)
    return cos, sin


def _attn_kernel(q_ref, k_ref, v_ref, o_ref, m_scr, l_scr, acc_scr, *, nk, tk):
    ki = pl.program_id(3)

    @pl.when(ki == 0)
    def _():
        m_scr[...] = jnp.full(m_scr.shape, -jnp.inf, F32)
        l_scr[...] = jnp.zeros(l_scr.shape, F32)
        acc_scr[...] = jnp.zeros(acc_scr.shape, F32)

    k = k_ref[...]
    v = v_ref[...]
    for g in range(Q_PER_KV):
        q = q_ref[:, g * HEAD_DIM:(g + 1) * HEAD_DIM]
        s = lax.dot_general(q, k, (((1,), (1,)), ((), ())), preferred_element_type=F32)
        m_prev = m_scr[g]
        m_next = jnp.maximum(m_prev, jnp.max(s, axis=1, keepdims=True))
        p = jnp.exp2(s - jnp.concatenate([m_next] * (tk // LANES), axis=1))
        alpha = jnp.exp2(m_prev - m_next)
        l_scr[g] = alpha * l_scr[g] + jnp.sum(p, axis=1, keepdims=True)
        acc_scr[g] = alpha * acc_scr[g] + jnp.dot(p.astype(BF16), v, preferred_element_type=F32)
        m_scr[g] = m_next

    @pl.when(ki == nk - 1)
    def _():
        for g in range(Q_PER_KV):
            o_ref[:, g * HEAD_DIM:(g + 1) * HEAD_DIM] = (acc_scr[g] / l_scr[g]).astype(o_ref.dtype)


def _attention(z, *, batch, seq, q_col, tq=512, tk=512):
    n = z.shape[0]
    nq, nk = seq // tq, seq // tk
    qw = Q_PER_KV * HEAD_DIM
    qblk = q_col // qw
    kcol = q_col // HEAD_DIM + N_Q_HEADS
    vcol = kcol + N_KV_HEADS
    return pl.pallas_call(
        functools.partial(_attn_kernel, nk=nk, tk=tk),
        grid=(batch, N_KV_HEADS, nq, nk),
        in_specs=[pl.BlockSpec((tq, qw), lambda b, h, qi, ki: (b * nq + qi, qblk + h)),
                  pl.BlockSpec((tk, HEAD_DIM), lambda b, h, qi, ki: (b * nk + ki, kcol + h)),
                  pl.BlockSpec((tk, HEAD_DIM), lambda b, h, qi, ki: (b * nk + ki, vcol + h))],
        out_specs=pl.BlockSpec((tq, qw), lambda b, h, qi, ki: (b * nq + qi, h)),
        out_shape=jax.ShapeDtypeStruct((n, N_Q_HEADS * HEAD_DIM), BF16),
        scratch_shapes=[pltpu.VMEM((Q_PER_KV, tq, HEAD_DIM), F32)] * 3,
        compiler_params=_cparams(("arbitrary", "arbitrary", "arbitrary", "arbitrary")),
        name="flash_attention",
    )(z, z, z)


def _mconv_kernel(prev_ref, cur_ref, next_ref, cw_ref, wq_ref, wk_ref, q_ref, k_ref, *, tl, nl):
    u = _with_halo(prev_ref, cur_ref, next_ref, nl)
    rows = tl + 2 * HALO
    acc = None
    for kk in range(M_CONV_W):
        sh = (M_CONV_W // 2 - kk) % rows
        tap = (pltpu.roll(u, sh, 0) if sh else u) * cw_ref[kk:kk + 1, :]
        acc = tap if acc is None else acc + tap
    uc = _silu(acc[HALO:HALO + tl]).astype(BF16)
    for h in range(N_M_HEADS):
        sl = slice(h * HEAD_DIM, (h + 1) * HEAD_DIM)
        q_ref[:, sl] = jnp.dot(uc[:, sl], wq_ref[h], preferred_element_type=F32).astype(q_ref.dtype)
        kh = jnp.dot(uc[:, sl], wk_ref[h], preferred_element_type=F32) * (HEAD_DIM ** -0.5)
        k_ref[:, sl] = kh.astype(k_ref.dtype)


def _mlstm_qk(z, conv_w, wq, wk, *, batch, seq, tl=512):
    n = z.shape[0]
    nl = seq // tl
    width = N_M_HEADS * HEAD_DIM
    prev, cur, nxt = _halo_specs(tl, width, 4, nl, batch)
    wspec = pl.BlockSpec((N_M_HEADS, HEAD_DIM, HEAD_DIM), lambda b, i: (0, 0, 0))
    ospec = pl.BlockSpec((tl, width), lambda b, i: (b * nl + i, 0))
    return pl.pallas_call(
        functools.partial(_mconv_kernel, tl=tl, nl=nl),
        grid=(batch, nl),
        in_specs=[prev, cur, nxt, pl.BlockSpec((M_CONV_W, width), lambda b, i: (0, 0)), wspec, wspec],
        out_specs=[ospec, ospec],
        out_shape=[jax.ShapeDtypeStruct((n, width), BF16)] * 2,
        compiler_params=_cparams(("arbitrary", "arbitrary")),
        name="mlstm_conv_qk",
    )(z, z, z, conv_w, wq.astype(BF16), wk.astype(BF16))


def _split3(x):
    x1 = x.astype(BF16)
    r = x - x1.astype(F32)
    x2 = r.astype(BF16)
    x3 = (r - x2.astype(F32)).astype(BF16)
    return x1, x2, x3


def _mscan_kernel(qf_ref, kf_ref, vf_ref, gf_ref, gtf_ref, qb_ref, kb_ref, vb_ref, gb_ref, gtb_ref,
                  bias_ref, biast_ref, hf_ref, hb_ref, s_scr, n_scr, m_scr):
    c = pl.program_id(0)

    @pl.when(c == 0)
    def _():
        s_scr[...] = jnp.zeros(s_scr.shape, F32)
        n_scr[...] = jnp.zeros(n_scr.shape, F32)
        m_scr[...] = jnp.zeros(m_scr.shape, F32)

    ch = M_CHUNK
    ti = lax.broadcasted_iota(jnp.int32, (ch, ch), 0)
    si = lax.broadcasted_iota(jnp.int32, (ch, ch), 1)
    nh = N_M_HEADS

    for b, rev in [(b, rev) for b in range(qf_ref.shape[0]) for rev in (False, True)]:
        q_ref, k_ref, v_ref, g_ref, gt_ref, h_ref = (
            (qb_ref, kb_ref, vb_ref, gb_ref, gtb_ref, hb_ref) if rev else
            (qf_ref, kf_ref, vf_ref, gf_ref, gtf_ref, hf_ref))
        causal = (si >= ti) if rev else (si <= ti)
        tri = jnp.where(causal, 1.0, 0.0).astype(BF16)
        tri_t = jnp.where((ti >= si) if rev else (ti <= si), 1.0, 0.0).astype(BF16)
        gates = g_ref[b] + bias_ref[...]
        gates_t = gt_ref[b] + biast_ref[...]
        lf = jax.nn.log_sigmoid(gates)
        lf_t = jax.nn.log_sigmoid(gates_t)
        cum = sum(jnp.dot(tri, part, preferred_element_type=F32) for part in _split3(lf))
        cum_t = sum(jnp.dot(part, tri_t, preferred_element_type=F32) for part in _split3(lf_t))
        last = 0 if rev else ch - 1
        for h in range(nh):
            idx = (2 * b + (1 if rev else 0)) * nh + h
            icol = (2 * nh if rev else 0) + h
            fcol = icol + nh
            sl = slice(h * HEAD_DIM, (h + 1) * HEAD_DIM)
            q = q_ref[b, :, sl]
            k = k_ref[b, :, sl]
            v = v_ref[b, :, sl]
            b_col = cum[:, fcol:fcol + 1]
            b_row = cum_t[fcol:fcol + 1, :]
            li_col = gates[:, icol:icol + 1]
            li_row = gates_t[icol:icol + 1, :]
            gtot = cum[last:last + 1, fcol:fcol + 1]
            m_prev = m_scr[idx][:, :1]
            n_prev = n_scr[idx]
            s_prev = s_scr[idx]

            dm = jnp.where(causal, b_col - b_row + li_row, -jnp.inf)
            m_inter = b_col + m_prev
            m_t = jnp.maximum(m_inter, jnp.max(dm, axis=1, keepdims=True))
            p = jnp.exp(dm - m_t)
            qk = lax.dot_general(q, k, (((1,), (1,)), ((), ())), preferred_element_type=F32)
            sc = qk * p
            w_inter = jnp.exp(m_inter - m_t)
            num = (jnp.dot(sc.astype(BF16), v, preferred_element_type=F32)
                   + w_inter * jnp.dot(q, s_prev.astype(BF16), preferred_element_type=F32))
            den = (jnp.sum(sc, axis=1, keepdims=True)
                   + w_inter * jnp.sum(q.astype(F32) * n_prev, axis=1, keepdims=True))
            h_ref[b, :, sl] = num / jnp.maximum(jnp.abs(den), jnp.exp(-m_t))

            w_st = gtot - b_col + li_col
            m_loc = jnp.max(w_st, axis=0, keepdims=True)
            ak = jnp.exp(w_st - m_loc) * k.astype(F32)
            s_c = lax.dot_general(ak.astype(BF16), v, (((0,), (0,)), ((), ())), preferred_element_type=F32)
            n_c = jnp.sum(ak, axis=0, keepdims=True)
            m_new = jnp.maximum(gtot + m_prev, m_loc)
            decay = jnp.exp(gtot + m_prev - m_new)
            add = jnp.exp(m_loc - m_new)
            s_scr[idx] = decay * s_prev + add * s_c
            n_scr[idx] = decay * n_prev + add * n_c
            m_scr[idx] = jnp.broadcast_to(m_new, (1, HEAD_DIM))


def _mlstm_scan(qm, km, z, gates, gates_t, gate_b, *, batch, seq):
    n = qm.shape[0]
    nc = seq // M_CHUNK
    width = N_M_HEADS * HEAD_DIM
    ng = 4 * N_M_HEADS
    chains = 2 * batch * N_M_HEADS
    fwd = lambda c: (0, c, 0)
    bwd = lambda c: (0, nc - 1 - c, 0)
    fwd_v = lambda c: (0, c, 5)
    bwd_v = lambda c: (0, nc - 1 - c, 5)
    fwd_t = lambda c: (0, 0, c)
    bwd_t = lambda c: (0, 0, nc - 1 - c)
    blk = (batch, M_CHUNK, width)
    gblk = (batch, M_CHUNK, LANES)
    tblk = (batch, ng, M_CHUNK)
    bias = jnp.zeros((1, LANES), F32).at[0, :ng].set(gate_b)
    bias_t = jnp.broadcast_to(gate_b[:, None], (ng, M_CHUNK))
    const = lambda c: (0, 0)
    in_specs = [
        pl.BlockSpec(blk, fwd), pl.BlockSpec(blk, fwd), pl.BlockSpec(blk, fwd_v),
        pl.BlockSpec(gblk, fwd), pl.BlockSpec(tblk, fwd_t),
        pl.BlockSpec(blk, bwd), pl.BlockSpec(blk, bwd), pl.BlockSpec(blk, bwd_v),
        pl.BlockSpec(gblk, bwd), pl.BlockSpec(tblk, bwd_t),
        pl.BlockSpec((1, LANES), const), pl.BlockSpec((ng, M_CHUNK), const),
    ]
    q3, k3, z3, g3 = (a.reshape(batch, seq, a.shape[1]) for a in (qm, km, z, gates))
    hf, hb = pl.pallas_call(
        _mscan_kernel,
        grid=(nc,),
        in_specs=in_specs,
        out_specs=[pl.BlockSpec(blk, fwd), pl.BlockSpec(blk, bwd)],
        out_shape=[jax.ShapeDtypeStruct((batch, seq, width), F32)] * 2,
        scratch_shapes=[pltpu.VMEM((chains, HEAD_DIM, HEAD_DIM), F32),
                        pltpu.VMEM((chains, 1, HEAD_DIM), F32),
                        pltpu.VMEM((chains, 1, HEAD_DIM), F32)],
        compiler_params=_cparams(("arbitrary",)),
        name="mlstm_scan",
    )(q3, k3, z3, g3, gates_t, q3, k3, z3, g3, gates_t, bias, bias_t)
    return hf.reshape(n, width), hb.reshape(n, width)


def _mout_kernel(hf_ref, hb_ref, o_ref, g_ref, y_ref):
    for h in range(N_M_HEADS):
        sl = slice(h * HEAD_DIM, (h + 1) * HEAD_DIM)
        x = hf_ref[:, sl] + hb_ref[:, sl]
        ms = jnp.mean(x * x, axis=-1, keepdims=True)
        hn = x * lax.rsqrt(ms + EPS) * g_ref[...]
        y_ref[:, sl] = (jax.nn.sigmoid(o_ref[:, sl].astype(F32)) * hn).astype(y_ref.dtype)


def _mlstm_out(hf, hb, z, norm_g, *, tm=1024):
    n, width = hf.shape
    spec = pl.BlockSpec((tm, width), lambda i: (i, 0))
    return pl.pallas_call(
        _mout_kernel,
        grid=(n // tm,),
        in_specs=[spec, spec, pl.BlockSpec((tm, width), lambda i: (i, 6)),
                  pl.BlockSpec((1, HEAD_DIM), lambda i: (0, 0))],
        out_specs=spec,
        out_shape=jax.ShapeDtypeStruct((n, width), BF16),
        compiler_params=_cparams(("arbitrary",)),
        name="mlstm_out",
    )(hf, hb, z, norm_g.reshape(1, HEAD_DIM))


def _top2(logit):
    lane = lax.broadcasted_iota(jnp.int32, logit.shape, 1).astype(F32)
    logit = jnp.where(lane < N_EXPERTS, logit, -jnp.inf)
    m1 = jnp.max(logit, axis=1, keepdims=True)
    i1 = jnp.min(jnp.where(logit == m1, lane, float(LANES)), axis=1, keepdims=True)
    rest = jnp.where(lane == i1, -jnp.inf, logit)
    m2 = jnp.max(rest, axis=1, keepdims=True)
    i2 = jnp.min(jnp.where(rest == m2, lane, float(LANES)), axis=1, keepdims=True)
    e = jnp.exp(m2 - m1)
    p1 = 1.0 / (1.0 + e)
    p2 = e / (1.0 + e)
    comb = jnp.where(lane == i1, p1, 0.0) + jnp.where(lane == i2, p2, 0.0)
    sel = jnp.where((lane == i1) | (lane == i2), 1.0, 0.0)
    return comb, sel


def _pack_halves(y):
    half = y.shape[1] // 2
    lo = pltpu.bitcast(y[:, :half].astype(BF16).astype(F32), jnp.uint32)
    hi = pltpu.bitcast(y[:, half:].astype(BF16).astype(F32), jnp.uint32)
    return hi | (lo >> 16)


def _unpack_halves(w):
    lo = pltpu.bitcast(w << 16, F32).astype(BF16)
    hi = pltpu.bitcast(w & jnp.uint32(0xFFFF0000), F32).astype(BF16)
    return lo, hi


def _mod_norm(x, g, sc=None, sh=None):
    ms = jnp.mean(x * x, axis=-1, keepdims=True)
    y = x * lax.rsqrt(ms + EPS) * g
    return y if sc is None else y * (1.0 + sc) + sh


def _out_proj_kernel(*refs, n_a, route):
    a_refs = refs[:n_a]
    w_ref, x_ref, g1_ref, ng_ref, sc_ref, sh_ref = refs[n_a:n_a + 6]
    if route:
        rw_ref, rb_ref, xo_ref, hp_ref, comb_ref, sel_ref, wb = refs[n_a + 6:]
    else:
        xo_ref, h_ref, wb = refs[n_a + 6:]

    @pl.when(pl.program_id(0) == 0)
    def _():
        wb[...] = w_ref[...].astype(BF16)

    acc, k0 = None, 0
    for a_ref in a_refs:
        kk = a_ref.shape[1]
        part = jnp.dot(a_ref[...], wb[k0:k0 + kk, :], preferred_element_type=F32)
        acc = part if acc is None else acc + part
        k0 += kk
    xn = x_ref[...] + g1_ref[0] * acc
    xo_ref[...] = xn
    y = _mod_norm(xn, ng_ref[...], sc_ref[0], sh_ref[0])
    if route:
        hp_ref[...] = _pack_halves(y)
        y_hi = y.astype(BF16)
        y_lo = (y - y_hi.astype(F32)).astype(BF16)
        logit = (jnp.dot(y_hi, rw_ref[0], preferred_element_type=F32)
                 + jnp.dot(y_lo, rw_ref[0], preferred_element_type=F32)
                 + jnp.dot(y_hi, rw_ref[1], preferred_element_type=F32)) + rb_ref[...]
        comb_ref[...], sel_ref[...] = _top2(logit)
    else:
        h_ref[...] = y.astype(h_ref.dtype)


def _out_proj(a_list, w_out, layer, x, g1, norm_g, sc, sh, router=None, *, rows_per_batch, tm=512):
    n, d = x.shape
    tpb = rows_per_batch // tm
    route = router is not None
    row = lambda w: pl.BlockSpec((tm, w), lambda i: (i, 0))
    per_batch = pl.BlockSpec((1, 1, d), lambda i: (i // tpb, 0, 0))
    const = lambda shape: pl.BlockSpec(shape, lambda i: (0,) * len(shape))
    in_specs = [row(a.shape[1]) for a in a_list]
    in_specs += [pl.BlockSpec((None, d, d), lambda i: (layer, 0, 0), pipeline_mode=pl.Buffered(1)), row(d), per_batch,
                 const((1, d)), per_batch, per_batch]
    args = list(a_list) + [w_out, x, g1, norm_g.reshape(1, d), sc, sh]
    out_specs = [row(d)]
    out_shape = [jax.ShapeDtypeStruct((n, d), F32)]
    if route:
        router_w, router_b = router
        rw = _pad_cols(router_w, LANES)
        rw_hi = rw.astype(BF16)
        rw_split = jnp.stack([rw_hi, (rw - rw_hi.astype(F32)).astype(BF16)])
        in_specs += [const((2, d, LANES)), const((1, LANES))]
        args += [rw_split, jnp.zeros((1, LANES), F32).at[0, :N_EXPERTS].set(router_b)]
        out_specs += [row(d // 2), row(LANES), row(LANES)]
        out_shape += [jax.ShapeDtypeStruct((n, d // 2), jnp.uint32), jax.ShapeDtypeStruct((n, LANES), F32),
                      jax.ShapeDtypeStruct((n, LANES), F32)]
    else:
        out_specs.append(row(d))
        out_shape.append(jax.ShapeDtypeStruct((n, d), BF16))
    return pl.pallas_call(
        functools.partial(_out_proj_kernel, n_a=len(a_list), route=route),
        grid=(n // tm,),
        in_specs=in_specs,
        out_specs=out_specs,
        out_shape=out_shape,
        scratch_shapes=[pltpu.VMEM((d, d), BF16)],
        compiler_params=_cparams(("arbitrary",)),
        name="out_proj_route" if route else "out_proj",
    )(*args)


def _rank_kernel(sel_ref, rank_ref, tot_ref, carry):
    @pl.when(pl.program_id(0) == 0)
    def _():
        carry[...] = jnp.zeros(carry.shape, F32)

    sel = sel_ref[...]
    tb = sel.shape[0]
    r = lax.broadcasted_iota(jnp.int32, (tb, tb), 0)
    c = lax.broadcasted_iota(jnp.int32, (tb, tb), 1)
    tri = jnp.where(r >= c, 1.0, 0.0).astype(BF16)
    incl = jnp.dot(tri, sel.astype(BF16), preferred_element_type=F32)
    rank_ref[...] = incl - sel + carry[...]
    total = carry[...] + incl[tb - 1:tb, :]
    carry[...] = total
    tot_ref[...] = total


def _rank(sel, *, tb=512):
    n = sel.shape[0]
    return pl.pallas_call(
        _rank_kernel,
        grid=(n // tb,),
        in_specs=[pl.BlockSpec((tb, LANES), lambda i: (i, 0))],
        out_specs=[pl.BlockSpec((tb, LANES), lambda i: (i, 0)), pl.BlockSpec((1, LANES), lambda i: (0, 0))],
        out_shape=[jax.ShapeDtypeStruct((n, LANES), F32), jax.ShapeDtypeStruct((1, LANES), F32)],
        scratch_shapes=[pltpu.VMEM((1, LANES), F32)],
        compiler_params=_cparams(("arbitrary",)),
        name="moe_rank",
    )(sel)


def _dest_kernel(rank_ref, sel_ref, comb_ref, off_ref, dest_ref, p_ref):
    lane = lax.broadcasted_iota(jnp.int32, rank_ref.shape, 1).astype(F32)
    sel = sel_ref[...] > 0.5
    pos = rank_ref[...] + off_ref[...]
    la = jnp.min(jnp.where(sel, lane, float(LANES)), axis=1, keepdims=True)
    lb = jnp.max(jnp.where(sel, lane, -1.0), axis=1, keepdims=True)
    pick = lambda l, v: jnp.sum(jnp.where(lane == l, v, 0.0), axis=1, keepdims=True)
    two = lambda a, b: jnp.where(lane == 0.0, a, jnp.where(lane == 1.0, b, 0.0))
    dest_ref[...] = two(pick(la, pos), pick(lb, pos)).astype(jnp.int32)
    p_ref[...] = two(pick(la, comb_ref[...]), pick(lb, comb_ref[...]))


def _dest(rank, sel, comb, off, *, tb=1024):
    n = rank.shape[0]
    spec = pl.BlockSpec((tb, LANES), lambda i: (i, 0))
    return pl.pallas_call(
        _dest_kernel,
        grid=(n // tb,),
        in_specs=[spec, spec, spec, pl.BlockSpec((1, LANES), lambda i: (0, 0))],
        out_specs=[spec, spec],
        out_shape=[jax.ShapeDtypeStruct((n, LANES), jnp.int32), jax.ShapeDtypeStruct((n, LANES), F32)],
        compiler_params=_cparams(("arbitrary",)),
        name="moe_dest",
    )(rank, sel, comb, off)


def _row_copy(src, src_row, dst, dst_row, sem):
    return pltpu.make_async_copy(src.at[pl.ds(src_row, 1), :], dst.at[pl.ds(dst_row, 1), :], sem)


def _dispatch_kernel(dest_ref, hp_ref, xs_in_ref, xs_ref, sem):
    del xs_in_ref
    tb = hp_ref.shape[0]
    for r in range(tb):
        for s in range(2):
            _row_copy(hp_ref, r, xs_ref, dest_ref[s, r], sem).start(priority=s)
    for s in range(2):
        pltpu.make_async_copy(hp_ref, xs_ref.at[pl.ds(0, tb), :], sem).wait()


def _dispatch(hp, dest3, rows, *, tb):
    n, w = hp.shape
    xs0 = jnp.zeros((rows, w), hp.dtype)
    return pl.pallas_call(
        _dispatch_kernel,
        grid=(n // tb,),
        in_specs=[pl.BlockSpec((None, 2, tb), lambda i: (i, 0, 0), memory_space=pltpu.SMEM),
                  pl.BlockSpec((tb, w), lambda i: (i, 0)),
                  pl.BlockSpec(memory_space=pl.ANY)],
        out_specs=pl.BlockSpec(memory_space=pl.ANY),
        out_shape=jax.ShapeDtypeStruct((rows, w), hp.dtype),
        scratch_shapes=[pltpu.SemaphoreType.DMA(())],
        input_output_aliases={2: 0},
        compiler_params=_cparams(("arbitrary",)),
        name="moe_dispatch",
    )(dest3, hp, xs0)


def _expert_changed(te_ref, t):
    return (t == 0) | (te_ref[t] != te_ref[jnp.maximum(t - 1, 0)])


def _moe_up_kernel(te_ref, nu_ref, xs_ref, w1_ref, w3_ref, o_ref, w1b, w3b):
    t = pl.program_id(1)

    @pl.when(_expert_changed(te_ref, t))
    def _():
        w1b[...] = w1_ref[...].astype(BF16)
        w3b[...] = w3_ref[...].astype(BF16)

    @pl.when(t < nu_ref[0])
    def _():
        x = jnp.concatenate(_unpack_halves(xs_ref[...]), axis=1)
        a1 = jnp.dot(x, w1b[...], preferred_element_type=F32)
        a3 = jnp.dot(x, w3b[...], preferred_element_type=F32)
        o_ref[...] = (_silu(a1) * a3).astype(o_ref.dtype)

    @pl.when(t >= nu_ref[0])
    def _():
        o_ref[...] = jnp.zeros(o_ref.shape, o_ref.dtype)


def _moe_up(xs, w1, w3, layer, te, nu, *, tg, tn):
    rows, half = xs.shape
    _, e, d, f = w1.shape
    wspec = pl.BlockSpec((None, None, d, tn), lambda j, t, te, nu: (layer, te[t], 0, j))
    return pl.pallas_call(
        _moe_up_kernel,
        grid_spec=pltpu.PrefetchScalarGridSpec(
            num_scalar_prefetch=2,
            grid=(f // tn, rows // tg),
            in_specs=[pl.BlockSpec((tg, half), lambda j, t, te, nu: (t, 0)), wspec, wspec],
            out_specs=pl.BlockSpec((tg, tn), lambda j, t, te, nu: (t, j)),
            scratch_shapes=[pltpu.VMEM((d, tn), BF16)] * 2),
        out_shape=jax.ShapeDtypeStruct((rows, f), BF16),
        compiler_params=_cparams(("arbitrary", "arbitrary")),
        name="moe_up",
    )(te, nu, xs, w1, w3)


def _moe_down_kernel(te_ref, nu_ref, a_ref, w2_ref, o_ref, w2b):
    t = pl.program_id(1)

    @pl.when(_expert_changed(te_ref, t))
    def _():
        w2b[...] = w2_ref[...].astype(BF16)

    @pl.when(t < nu_ref[0])
    def _():
        o_ref[...] = _pack_halves(jnp.dot(a_ref[...], w2b[...], preferred_element_type=F32))

    @pl.when(t >= nu_ref[0])
    def _():
        o_ref[...] = jnp.zeros(o_ref.shape, o_ref.dtype)


def _moe_down(a, w2, layer, te, nu, *, tg, tn):
    rows, f = a.shape
    d = w2.shape[-1]
    return pl.pallas_call(
        _moe_down_kernel,
        grid_spec=pltpu.PrefetchScalarGridSpec(
            num_scalar_prefetch=2,
            grid=(d // tn, rows // tg),
            in_specs=[pl.BlockSpec((tg, f), lambda j, t, te, nu: (t, 0)),
                      pl.BlockSpec((None, None, f, tn), lambda j, t, te, nu: (layer, te[t], 0, j))],
            out_specs=pl.BlockSpec((tg, tn // 2), lambda j, t, te, nu: (t, j)),
            scratch_shapes=[pltpu.VMEM((f, tn), BF16)]),
        out_shape=jax.ShapeDtypeStruct((rows, d // 2), jnp.uint32),
        compiler_params=_cparams(("arbitrary", "arbitrary")),
        name="moe_down",
    )(te, nu, a, w2)


def _combine_kernel(*refs, modulate, pack_w):
    if modulate:
        dest_ref, x_ref, g_ref, p_ref, ng_ref, sc_ref, sh_ref, ys_ref, o_ref, h_ref, buf, sem = refs
    else:
        dest_ref, x_ref, g_ref, p_ref, ng_ref, ys_ref, o_ref, h_ref, buf, sem = refs
    tb = x_ref.shape[0]
    for r in range(tb):
        for s in range(2):
            _row_copy(ys_ref, dest_ref[s, r], buf.at[s], r, sem).start(priority=s)
    for s in range(2):
        pltpu.make_async_copy(ys_ref.at[pl.ds(0, tb), :], buf.at[s], sem).wait()

    def expert_rows(s):
        w = buf[s]
        parts = []
        for c in range(w.shape[1] // pack_w):
            wc = w[:, c * pack_w:(c + 1) * pack_w]
            parts += [pltpu.bitcast(wc << 16, F32), pltpu.bitcast(wc & jnp.uint32(0xFFFF0000), F32)]
        return jnp.concatenate(parts, axis=1)

    p = p_ref[...]
    xn = x_ref[...] + g_ref[0] * (p[:, 0:1] * expert_rows(0) + p[:, 1:2] * expert_rows(1))
    o_ref[...] = xn
    if modulate:
        h_ref[...] = _mod_norm(xn, ng_ref[...], sc_ref[0], sh_ref[0]).astype(h_ref.dtype)
    else:
        h_ref[...] = _mod_norm(xn, ng_ref[...]).astype(h_ref.dtype)


def _combine(x, g, p, dest3, ys, norm_g, sc, sh, *, rows_per_batch, tb, h_dtype, pack_w):
    n, d = x.shape
    tpb = rows_per_batch // tb
    modulate = sc is not None
    row = pl.BlockSpec((tb, d), lambda i: (i, 0))
    per_batch = pl.BlockSpec((1, 1, d), lambda i: (i // tpb, 0, 0))
    in_specs = [pl.BlockSpec((None, 2, tb), lambda i: (i, 0, 0), memory_space=pltpu.SMEM), row, per_batch,
                pl.BlockSpec((tb, LANES), lambda i: (i, 0)), pl.BlockSpec((1, d), lambda i: (0, 0))]
    args = [dest3, x, g, p, norm_g.reshape(1, d)]
    if modulate:
        in_specs += [per_batch, per_batch]
        args += [sc, sh]
    return pl.pallas_call(
        functools.partial(_combine_kernel, modulate=modulate, pack_w=pack_w),
        grid=(n // tb,),
        in_specs=in_specs + [pl.BlockSpec(memory_space=pl.ANY)],
        out_specs=[row, row],
        out_shape=[jax.ShapeDtypeStruct((n, d), F32), jax.ShapeDtypeStruct((n, d), h_dtype)],
        scratch_shapes=[pltpu.VMEM((2, tb, d // 2), jnp.uint32), pltpu.SemaphoreType.DMA(())],
        compiler_params=_cparams(("arbitrary",)),
        name="moe_combine",
    )(*args, ys)


def _moe_ffn(x, hp, comb, sel, g, w1, w3, w2, layer, next_norm, *, rows_per_batch, h_dtype, tg=512, tb=256,
             tn_up=896, tn_down=1024):
    n, d = x.shape
    e = w1.shape[1]
    t_max = 2 * n // tg + e
    rank, tot = _rank(sel)
    cnt = tot[0, :e].astype(jnp.int32)
    tiles = (cnt + tg - 1) // tg
    tile_end = jnp.cumsum(tiles)
    n_used = tile_end[-1]
    off = jnp.zeros((1, LANES), F32).at[0, :e].set(((tile_end - tiles) * tg).astype(F32))
    te = jnp.sum(jnp.arange(t_max, dtype=jnp.int32)[:, None] >= tile_end[None, :], axis=1).astype(jnp.int32)
    te = jnp.minimum(te, te[jnp.maximum(n_used - 1, 0)])
    nu = n_used.reshape(1).astype(jnp.int32)
    dest, p = _dest(rank, sel, comb, off)
    dest3 = dest[:, :2].T.reshape(2, n // tb, tb).transpose(1, 0, 2)
    xs = _dispatch(hp, dest3, t_max * tg, tb=tb)
    a = _moe_up(xs, w1, w3, layer, te, nu, tg=tg, tn=tn_up)
    ys = _moe_down(a, w2, layer, te, nu, tg=tg, tn=tn_down)
    return _combine(x, g, p, dest3, ys, *next_norm, rows_per_batch=rows_per_batch, tb=tb, h_dtype=h_dtype,
                    pack_w=tn_down // 2)


def _pad_cols(w, width):
    return jnp.zeros((w.shape[0], width), w.dtype).at[:, :w.shape[1]].set(w)


def kernel(x, c, ada_w, ada_b, norm1_g, norm2_g, w_in, w_out, pool_w, pool_scale, q_norm_g, k_norm_g,
           m_conv_w, m_wq, m_wk, m_gate_b, m_norm_g, ffn_w1, ffn_w3, ffn_w2, router_w, router_b,
           moe_w1, moe_w3, moe_w2, final_norm_g):
    batch, seq, d = x.shape
    depth = ada_w.shape[0]
    n = batch * seq
    ng = 4 * N_M_HEADS
    d_main = w_in.shape[2] - ng
    xf = x.reshape(n, d)
    mod = _adaln_mod(c, ada_w, ada_b)
    cos, sin = _rope_tables(seq)
    dims = dict(batch=batch, seq=seq)

    mods = [[mod[l, :, i * d:(i + 1) * d].reshape(batch, 1, d) for i in range(6)] for l in range(depth)]
    h = None
    for l in range(depth):
        sh1, sc1, g1, sh2, sc2, g2 = mods[l]
        if h is None:
            h = _norm(xf, norm1_g[l], sc1, sh1, rows_per_batch=seq, out_dtype=BF16)
        z = _in_proj(h, w_in, l, cos, sin, q_norm_g[l], k_norm_g[l], seq=seq, ncols=d_main)
        gates = _mm([h], w_in, (l,), tm=1024, tn=LANES, ncols=LANES, col_blk0=d_main // LANES, valid_cols=ng)
        y_pool = _pool_mixer(z, pool_w[l], pool_scale[l], **dims)
        y_attn = _attention(z, q_col=pool_w.shape[1] * pool_w.shape[2], tq=4096, tk=256, **dims)
        qm, km = _mlstm_qk(z, m_conv_w[l], m_wq[l], m_wk[l], **dims)
        gates_t = gates[:, :ng].reshape(batch, seq, ng).transpose(0, 2, 1)
        hf, hb = _mlstm_scan(qm, km, z, gates, gates_t, m_gate_b[l], **dims)
        y_m = _mlstm_out(hf, hb, z, m_norm_g[l])
        mixed = [y_pool, y_attn, y_m]
        i = l // 2
        if l % 2 == 0:
            xf, h2 = _out_proj(mixed, w_out, l, xf, g1, norm2_g[l], sc2, sh2, rows_per_batch=seq)
            a = _swiglu_up(h2, ffn_w1, ffn_w3, (i,), tm=1024, tn=512)
            xf = _mm([a], ffn_w2, (i,), tm=512, tn=512, x=xf, g=g2, rows_per_batch=seq)
            h = None
        else:
            xf, hp, comb, sel = _out_proj(mixed, w_out, l, xf, g1, norm2_g[l], sc2, sh2,
                                          (router_w[i], router_b[i]), rows_per_batch=seq)
            last = l == depth - 1
            next_norm = (final_norm_g, None, None) if last else (norm1_g[l + 1], mods[l + 1][1], mods[l + 1][0])
            xf, h = _moe_ffn(xf, hp, comb, sel, g2, moe_w1, moe_w3, moe_w2, i, next_norm, rows_per_batch=seq,
                             h_dtype=F32 if last else BF16)
            if last:
                return h.reshape(batch, seq, d)

    return _norm(xf, final_norm_g, rows_per_batch=seq, out_dtype=F32).reshape(batch, seq, d)
```

```python
import functools
import math

import numpy as np
import jax
import jax.numpy as jnp
from jax import lax
from jax.experimental import pallas as pl
from jax.experimental.pallas import tpu as pltpu

F32 = jnp.float32
BF16 = jnp.bfloat16

EPS = 1e-6
HEAD_DIM = 128
GRID_W = 64
ROPE_THETA = 10000.0
POOL_WINDOWS = (2, 4, 8, 16)
M_CHUNK = 128
M_CONV_W = 5
N_M_HEADS = 4
N_Q_HEADS = 8
N_KV_HEADS = 2
Q_PER_KV = N_Q_HEADS // N_KV_HEADS
N_EXPERTS = 8
LANES = 128
HALO = 16
VMEM_LIMIT = 56 * 1024 * 1024
LOG2E = math.log2(math.e)


def _cparams(sem, flags=None):
    return pltpu.CompilerParams(dimension_semantics=sem, vmem_limit_bytes=VMEM_LIMIT, flags=flags)


def _silu(a):
    return a * jax.nn.sigmoid(a)


def _mod_kernel(c_ref, w_ref, b_ref, o_ref):
    ca = _silu(c_ref[...])
    o_ref[0] = jnp.dot(ca.astype(BF16), w_ref[0].astype(BF16), preferred_element_type=F32) + b_ref[0]


def _adaln_mod(c, ada_w, ada_b, tn=768):
    depth, d, n6 = ada_w.shape
    b = c.shape[0]
    cp = jnp.zeros((8, d), F32).at[:b].set(c)
    out = pl.pallas_call(
        _mod_kernel,
        grid=(depth, n6 // tn),
        in_specs=[
            pl.BlockSpec((8, d), lambda l, j: (0, 0)),
            pl.BlockSpec((1, d, tn), lambda l, j: (l, 0, j)),
            pl.BlockSpec((1, 1, tn), lambda l, j: (l, 0, j)),
        ],
        out_specs=pl.BlockSpec((1, 8, tn), lambda l, j: (l, 0, j)),
        out_shape=jax.ShapeDtypeStruct((depth, 8, n6), F32),
        compiler_params=_cparams(("arbitrary", "arbitrary")),
        name="adaln_mod",
    )(cp, ada_w, ada_b.reshape(depth, 1, n6))
    return out[:, :b]


def _norm_kernel(*refs, modulate):
    if modulate:
        x_ref, g_ref, sc_ref, sh_ref, o_ref = refs
    else:
        x_ref, g_ref, o_ref = refs
    x = x_ref[...]
    ms = jnp.mean(x * x, axis=-1, keepdims=True)
    y = x * lax.rsqrt(ms + EPS) * g_ref[...]
    if modulate:
        y = y * (1.0 + sc_ref[0]) + sh_ref[0]
    o_ref[...] = y.astype(o_ref.dtype)


def _norm(x, g, sc=None, sh=None, *, rows_per_batch, out_dtype, tm=1024):
    n, d = x.shape
    tpb = rows_per_batch // tm
    modulate = sc is not None
    in_specs = [pl.BlockSpec((tm, d), lambda i: (i, 0)), pl.BlockSpec((1, d), lambda i: (0, 0))]
    args = [x, g.reshape(1, d)]
    if modulate:
        in_specs += [pl.BlockSpec((1, 1, d), lambda i: (i // tpb, 0, 0))] * 2
        args += [sc, sh]
    return pl.pallas_call(
        functools.partial(_norm_kernel, modulate=modulate),
        grid=(n // tm,),
        in_specs=in_specs,
        out_specs=pl.BlockSpec((tm, d), lambda i: (i, 0)),
        out_shape=jax.ShapeDtypeStruct((n, d), out_dtype),
        compiler_params=_cparams(("arbitrary",)),
        name="rms_norm",
    )(*args)


def _mm_kernel(*refs, n_a, resid, valid_cols):
    a_refs, w_ref = refs[:n_a], refs[n_a]
    if resid:
        x_ref, g_ref, o_ref, wb = refs[n_a + 1:]
    else:
        o_ref, wb = refs[n_a + 1:]

    @pl.when(pl.program_id(1) == 0)
    def _():
        w = w_ref[...]
        if valid_cols is not None:
            w = jnp.where(lax.broadcasted_iota(jnp.int32, w.shape, 1) < valid_cols, w, 0.0)
        wb[...] = w.astype(BF16)

    acc, k0 = None, 0
    for a_ref in a_refs:
        kk = a_ref.shape[1]
        part = jnp.dot(a_ref[...], wb[k0:k0 + kk, :], preferred_element_type=F32)
        acc = part if acc is None else acc + part
        k0 += kk
    if resid:
        o_ref[...] = x_ref[...] + g_ref[0] * acc
    else:
        o_ref[...] = acc.astype(o_ref.dtype)


def _mm(a_list, w, w_idx, *, tm, tn, ncols=None, col_blk0=0, valid_cols=None, out_dtype=F32,
        x=None, g=None, rows_per_batch=None):
    m = a_list[0].shape[0]
    kdim = w.shape[-2]
    ncols = w.shape[-1] if ncols is None else ncols
    resid = x is not None
    lead = (None,) * len(w_idx)
    in_specs = [pl.BlockSpec((tm, a.shape[1]), lambda j, i: (i, 0)) for a in a_list]
    in_specs.append(pl.BlockSpec(lead + (kdim, tn), lambda j, i: tuple(w_idx) + (0, j + col_blk0)))
    args = list(a_list) + [w]
    if resid:
        tpb = rows_per_batch // tm
        in_specs += [pl.BlockSpec((tm, tn), lambda j, i: (i, j)),
                     pl.BlockSpec((1, 1, tn), lambda j, i: (i // tpb, 0, j))]
        args += [x, g]
    return pl.pallas_call(
        functools.partial(_mm_kernel, n_a=len(a_list), resid=resid, valid_cols=valid_cols),
        grid=(ncols // tn, m // tm),
        in_specs=in_specs,
        out_specs=pl.BlockSpec((tm, tn), lambda j, i: (i, j)),
        out_shape=jax.ShapeDtypeStruct((m, ncols), out_dtype),
        scratch_shapes=[pltpu.VMEM((kdim, tn), BF16)],
        compiler_params=_cparams(("arbitrary", "arbitrary")),
        name="matmul_resid" if resid else "matmul",
    )(*args)


def _head_norm_rope(x, g, cos, sin, ones, perm, scale):
    ss = jnp.dot((x * x).astype(BF16), ones, preferred_element_type=F32)
    y = x * lax.rsqrt(ss * (1.0 / HEAD_DIM) + EPS) * g
    rot = jnp.dot(y.astype(BF16), perm, preferred_element_type=F32)
    return (y * cos + rot * sin) * scale


def _win_kernel(h_ref, w_ref, cos_ref, sin_ref, qg_ref, kg_ref, o_ref, wb):
    j = pl.program_id(0)

    @pl.when(pl.program_id(1) == 0)
    def _():
        wb[...] = w_ref[...].astype(BF16)

    acc = jnp.dot(h_ref[...], wb[...], preferred_element_type=F32)

    def store_roped(g_ref, scale, n_rope):
        cos = cos_ref[...]
        sin = sin_ref[...]
        src = lax.broadcasted_iota(jnp.int32, (HEAD_DIM, HEAD_DIM), 0)
        dst = lax.broadcasted_iota(jnp.int32, (HEAD_DIM, HEAD_DIM), 1)
        quarter = HEAD_DIM // 4
        partner = jnp.where((dst % (2 * quarter)) < quarter, dst + quarter, dst - quarter)
        perm = jnp.where(src == partner, 1.0, 0.0).astype(BF16)
        ones = jnp.ones((HEAD_DIM, HEAD_DIM), BF16)
        for hh in range(acc.shape[1] // HEAD_DIM):
            sl = slice(hh * HEAD_DIM, (hh + 1) * HEAD_DIM)
            xh = acc[:, sl]
            if hh < n_rope:
                xh = _head_norm_rope(xh, g_ref[...], cos, sin, ones, perm, scale)
            o_ref[:, sl] = xh.astype(o_ref.dtype)

    @pl.when((j == 1) | (j == 2))
    def _():
        store_roped(qg_ref, LOG2E / math.sqrt(HEAD_DIM), 4)

    @pl.when(j == 3)
    def _():
        store_roped(kg_ref, 1.0, N_KV_HEADS)

    @pl.when((j == 0) | (j > 3))
    def _():
        o_ref[...] = acc.astype(o_ref.dtype)


def _in_proj(h, w_in, layer, cos, sin, qg, kg, *, seq, ncols, tm=2048, tn=512):
    n, d = h.shape
    tps = seq // tm
    return pl.pallas_call(
        _win_kernel,
        grid=(ncols // tn, n // tm),
        in_specs=[pl.BlockSpec((tm, d), lambda j, i: (i, 0)),
                  pl.BlockSpec((None, d, tn), lambda j, i: (layer, 0, j)),
                  pl.BlockSpec((tm, HEAD_DIM), lambda j, i: (i % tps, 0)),
                  pl.BlockSpec((tm, HEAD_DIM), lambda j, i: (i % tps, 0)),
                  pl.BlockSpec((1, HEAD_DIM), lambda j, i: (0, 0)),
                  pl.BlockSpec((1, HEAD_DIM), lambda j, i: (0, 0))],
        out_specs=pl.BlockSpec((tm, tn), lambda j, i: (i, j)),
        out_shape=jax.ShapeDtypeStruct((n, ncols), BF16),
        scratch_shapes=[pltpu.VMEM((d, tn), BF16)],
        compiler_params=_cparams(("arbitrary", "arbitrary")),
        name="in_proj",
    )(h, w_in, cos, sin, qg.reshape(1, HEAD_DIM), kg.reshape(1, HEAD_DIM))


def _up_kernel(h_ref, w1_ref, w3_ref, o_ref, w1b, w3b):
    @pl.when(pl.program_id(1) == 0)
    def _():
        w1b[...] = w1_ref[...].astype(BF16)
        w3b[...] = w3_ref[...].astype(BF16)

    h = h_ref[...]
    a1 = jnp.dot(h, w1b[...], preferred_element_type=F32)
    a3 = jnp.dot(h, w3b[...], preferred_element_type=F32)
    o_ref[...] = (_silu(a1) * a3).astype(o_ref.dtype)


def _swiglu_up(h, w1, w3, w_idx, *, tm, tn):
    n, d = h.shape
    f = w1.shape[-1]
    lead = (None,) * len(w_idx)
    wspec = pl.BlockSpec(lead + (d, tn), lambda j, i: tuple(w_idx) + (0, j))
    return pl.pallas_call(
        _up_kernel,
        grid=(f // tn, n // tm),
        in_specs=[pl.BlockSpec((tm, d), lambda j, i: (i, 0)), wspec, wspec],
        out_specs=pl.BlockSpec((tm, tn), lambda j, i: (i, j)),
        out_shape=jax.ShapeDtypeStruct((n, f), BF16),
        scratch_shapes=[pltpu.VMEM((d, tn), BF16)] * 2,
        compiler_params=_cparams(("arbitrary", "arbitrary")),
        name="swiglu_up",
    )(h, w1, w3)


def _halo_specs(tl, width, col_block, nl, batch):
    hb = tl // HALO
    nh = nl * hb
    last = batch * nh - 1
    cur = pl.BlockSpec((tl, width), lambda b, i: (b * nl + i, col_block))
    prev = pl.BlockSpec((HALO, width), lambda b, i: (jnp.maximum(b * nh + i * hb - 1, 0), col_block))
    nxt = pl.BlockSpec((HALO, width), lambda b, i: (jnp.minimum(b * nh + (i + 1) * hb, last), col_block))
    return prev, cur, nxt


def _with_halo(prev_ref, cur_ref, next_ref, nl):
    i = pl.program_id(1)
    prev = jnp.where(i == 0, 0.0, prev_ref[...].astype(F32))
    nxt = jnp.where(i == nl - 1, 0.0, next_ref[...].astype(F32))
    return jnp.concatenate([prev, cur_ref[...].astype(F32), nxt], axis=0)


def _pool_kernel(prev_ref, cur_ref, next_ref, w_ref, s_ref, o_ref, *, tl, nl, seq):
    u = _with_halo(prev_ref, cur_ref, next_ref, nl)
    rows = tl + 2 * HALO
    t = (pl.program_id(1) * tl + lax.broadcasted_iota(jnp.int32, (tl, LANES), 0)).astype(F32)
    for g, w in enumerate(POOL_WINDOWS):
        half = w // 2
        ug = u[:, g * LANES:(g + 1) * LANES]
        s = ug + pltpu.roll(ug, 1, 0)
        sh = 1
        while sh < half:
            s = pltpu.roll(s, sh, 0) + pltpu.roll(s, rows - sh, 0)
            sh *= 2
        cnt = jnp.minimum(t + (half - 1), seq - 1.0) - jnp.maximum(t - half, 0.0) + 1.0
        mean = s[HALO:HALO + tl] / cnt
        diff = mean - ug[HALO:HALO + tl]
        y = jnp.dot(diff.astype(BF16), w_ref[g], preferred_element_type=F32)
        o_ref[:, g * LANES:(g + 1) * LANES] = (y * s_ref[:, g * LANES:(g + 1) * LANES]).astype(o_ref.dtype)


def _pool_mixer(z, w_pool, scale, *, batch, seq, tl=1024):
    n = z.shape[0]
    nl = seq // tl
    width = len(POOL_WINDOWS) * LANES
    prev, cur, nxt = _halo_specs(tl, width, 0, nl, batch)
    return pl.pallas_call(
        functools.partial(_pool_kernel, tl=tl, nl=nl, seq=seq),
        grid=(batch, nl),
        in_specs=[prev, cur, nxt,
                  pl.BlockSpec((len(POOL_WINDOWS), LANES, LANES), lambda b, i: (0, 0, 0)),
                  pl.BlockSpec((1, width), lambda b, i: (0, 0))],
        out_specs=pl.BlockSpec((tl, width), lambda b, i: (b * nl + i, 0)),
        out_shape=jax.ShapeDtypeStruct((n, width), BF16),
        compiler_params=_cparams(("arbitrary", "arbitrary")),
        name="pool_mixer",
    )(z, z, z, w_pool.astype(BF16), scale.reshape(1, width))


def _rope_tables(seq):
    rows = seq // GRID_W
    row = jnp.broadcast_to(jnp.arange(rows, dtype=F32)[:, None], (rows, GRID_W)).reshape(seq)
    col = jnp.broadcast_to(jnp.arange(GRID_W, dtype=F32)[None, :], (rows, GRID_W)).reshape(seq)
    half = HEAD_DIM // 4
    inv = ROPE_THETA ** (-jnp.arange(half, dtype=F32) / half)
    ar = row[:, None] * inv[None, :]
    ac = col[:, None] * inv[None, :]
    cos = jnp.concatenate([jnp.cos(ar), jnp.cos(ar), jnp.cos(ac), jnp.cos(ac)], axis=-1)
    sin = jnp.concatenate([-jnp.sin(ar), jnp.sin(ar), -jnp.sin(ac), jnp.sin(ac)], axis=-1)
    return cos, sin


def _attn_kernel(q_ref, k_ref, v_ref, o_ref, m_scr, l_scr, acc_scr, *, nk, tk):
    ki = pl.program_id(3)

    @pl.when(ki == 0)
    def _():
        m_scr[...] = jnp.full(m_scr.shape, -jnp.inf, F32)
        l_scr[...] = jnp.zeros(l_scr.shape, F32)
        acc_scr[...] = jnp.zeros(acc_scr.shape, F32)

    k = k_ref[...]
    v = v_ref[...]
    for g in range(Q_PER_KV):
        q = q_ref[:, g * HEAD_DIM:(g + 1) * HEAD_DIM]
        s = lax.dot_general(q, k, (((1,), (1,)), ((), ())), preferred_element_type=F32)
        m_prev = m_scr[g]
        m_next = jnp.maximum(m_prev, jnp.max(s, axis=1, keepdims=True))
        p = jnp.exp2(s - jnp.concatenate([m_next] * (tk // LANES), axis=1))
        alpha = jnp.exp2(m_prev - m_next)
        l_scr[g] = alpha * l_scr[g] + jnp.sum(p, axis=1, keepdims=True)
        acc_scr[g] = alpha * acc_scr[g] + jnp.dot(p.astype(BF16), v, preferred_element_type=F32)
        m_scr[g] = m_next

    @pl.when(ki == nk - 1)
    def _():
        for g in range(Q_PER_KV):
            o_ref[:, g * HEAD_DIM:(g + 1) * HEAD_DIM] = (acc_scr[g] / l_scr[g]).astype(o_ref.dtype)


def _attention(z, *, batch, seq, q_col, tq=512, tk=512):
    n = z.shape[0]
    nq, nk = seq // tq, seq // tk
    qw = Q_PER_KV * HEAD_DIM
    qblk = q_col // qw
    kcol = q_col // HEAD_DIM + N_Q_HEADS
    vcol = kcol + N_KV_HEADS
    return pl.pallas_call(
        functools.partial(_attn_kernel, nk=nk, tk=tk),
        grid=(batch, N_KV_HEADS, nq, nk),
        in_specs=[pl.BlockSpec((tq, qw), lambda b, h, qi, ki: (b * nq + qi, qblk + h)),
                  pl.BlockSpec((tk, HEAD_DIM), lambda b, h, qi, ki: (b * nk + ki, kcol + h)),
                  pl.BlockSpec((tk, HEAD_DIM), lambda b, h, qi, ki: (b * nk + ki, vcol + h))],
        out_specs=pl.BlockSpec((tq, qw), lambda b, h, qi, ki: (b * nq + qi, h)),
        out_shape=jax.ShapeDtypeStruct((n, N_Q_HEADS * HEAD_DIM), BF16),
        scratch_shapes=[pltpu.VMEM((Q_PER_KV, tq, HEAD_DIM), F32)] * 3,
        compiler_params=_cparams(("arbitrary", "arbitrary", "arbitrary", "arbitrary")),
        name="flash_attention",
    )(z, z, z)


def _mconv_kernel(prev_ref, cur_ref, next_ref, cw_ref, wq_ref, wk_ref, q_ref, k_ref, *, tl, nl):
    u = _with_halo(prev_ref, cur_ref, next_ref, nl)
    rows = tl + 2 * HALO
    acc = None
    for kk in range(M_CONV_W):
        sh = (M_CONV_W // 2 - kk) % rows
        tap = (pltpu.roll(u, sh, 0) if sh else u) * cw_ref[kk:kk + 1, :]
        acc = tap if acc is None else acc + tap
    uc = _silu(acc[HALO:HALO + tl]).astype(BF16)
    for h in range(N_M_HEADS):
        sl = slice(h * HEAD_DIM, (h + 1) * HEAD_DIM)
        q_ref[:, sl] = jnp.dot(uc[:, sl], wq_ref[h], preferred_element_type=F32).astype(q_ref.dtype)
        kh = jnp.dot(uc[:, sl], wk_ref[h], preferred_element_type=F32) * (HEAD_DIM ** -0.5)
        k_ref[:, sl] = kh.astype(k_ref.dtype)


def _mlstm_qk(z, conv_w, wq, wk, *, batch, seq, tl=1024):
    n = z.shape[0]
    nl = seq // tl
    width = N_M_HEADS * HEAD_DIM
    prev, cur, nxt = _halo_specs(tl, width, 4, nl, batch)
    wspec = pl.BlockSpec((N_M_HEADS, HEAD_DIM, HEAD_DIM), lambda b, i: (0, 0, 0))
    ospec = pl.BlockSpec((tl, width), lambda b, i: (b * nl + i, 0))
    return pl.pallas_call(
        functools.partial(_mconv_kernel, tl=tl, nl=nl),
        grid=(batch, nl),
        in_specs=[prev, cur, nxt, pl.BlockSpec((M_CONV_W, width), lambda b, i: (0, 0)), wspec, wspec],
        out_specs=[ospec, ospec],
        out_shape=[jax.ShapeDtypeStruct((n, width), BF16)] * 2,
        compiler_params=_cparams(("arbitrary", "arbitrary")),
        name="mlstm_conv_qk",
    )(z, z, z, conv_w, wq.astype(BF16), wk.astype(BF16))


def _split3(x):
    x1 = x.astype(BF16)
    r = x - x1.astype(F32)
    x2 = r.astype(BF16)
    x3 = (r - x2.astype(F32)).astype(BF16)
    return x1, x2, x3


def _mscan_kernel(qf_ref, kf_ref, vf_ref, gf_ref, gtf_ref, qb_ref, kb_ref, vb_ref, gb_ref, gtb_ref,
                  bias_ref, biast_ref, hf_ref, hb_ref, s_scr, n_scr, m_scr):
    c = pl.program_id(0)

    @pl.when(c == 0)
    def _():
        s_scr[...] = jnp.zeros(s_scr.shape, F32)
        n_scr[...] = jnp.zeros(n_scr.shape, F32)
        m_scr[...] = jnp.zeros(m_scr.shape, F32)

    ch = M_CHUNK
    ti = lax.broadcasted_iota(jnp.int32, (ch, ch), 0)
    si = lax.broadcasted_iota(jnp.int32, (ch, ch), 1)
    nh = N_M_HEADS

    for b, rev in [(b, rev) for b in range(qf_ref.shape[0]) for rev in (False, True)]:
        q_ref, k_ref, v_ref, g_ref, gt_ref, h_ref = (
            (qb_ref, kb_ref, vb_ref, gb_ref, gtb_ref, hb_ref) if rev else
            (qf_ref, kf_ref, vf_ref, gf_ref, gtf_ref, hf_ref))
        causal = (si >= ti) if rev else (si <= ti)
        tri = jnp.where(causal, 1.0, 0.0).astype(BF16)
        tri_t = jnp.where((ti >= si) if rev else (ti <= si), 1.0, 0.0).astype(BF16)
        gates = g_ref[b] + bias_ref[...]
        gates_t = gt_ref[b] + biast_ref[...]
        lf = jax.nn.log_sigmoid(gates)
        lf_t = jax.nn.log_sigmoid(gates_t)
        cum = sum(jnp.dot(tri, part, preferred_element_type=F32) for part in _split3(lf))
        cum_t = sum(jnp.dot(part, tri_t, preferred_element_type=F32) for part in _split3(lf_t))
        last = 0 if rev else ch - 1
        for h in range(nh):
            idx = (2 * b + (1 if rev else 0)) * nh + h
            icol = (2 * nh if rev else 0) + h
            fcol = icol + nh
            sl = slice(h * HEAD_DIM, (h + 1) * HEAD_DIM)
            q = q_ref[b, :, sl]
            k = k_ref[b, :, sl]
            v = v_ref[b, :, sl]
            b_col = cum[:, fcol:fcol + 1]
            b_row = cum_t[fcol:fcol + 1, :]
            li_col = gates[:, icol:icol + 1]
            li_row = gates_t[icol:icol + 1, :]
            gtot = cum[last:last + 1, fcol:fcol + 1]
            m_prev = m_scr[idx][:, :1]
            n_prev = n_scr[idx]
            s_prev = s_scr[idx]

            dm = jnp.where(causal, b_col - b_row + li_row, -jnp.inf)
            m_inter = b_col + m_prev
            m_t = jnp.maximum(m_inter, jnp.max(dm, axis=1, keepdims=True))
            p = jnp.exp(dm - m_t)
            qk = lax.dot_general(q, k, (((1,), (1,)), ((), ())), preferred_element_type=F32)
            sc = qk * p
            w_inter = jnp.exp(m_inter - m_t)
            num = (jnp.dot(sc.astype(BF16), v, preferred_element_type=F32)
                   + w_inter * jnp.dot(q, s_prev.astype(BF16), preferred_element_type=F32))
            den = (jnp.sum(sc, axis=1, keepdims=True)
                   + w_inter * jnp.sum(q.astype(F32) * n_prev, axis=1, keepdims=True))
            h_ref[b, :, sl] = num / jnp.maximum(jnp.abs(den), jnp.exp(-m_t))

            w_st = gtot - b_col + li_col
            m_loc = jnp.max(w_st, axis=0, keepdims=True)
            ak = jnp.exp(w_st - m_loc) * k.astype(F32)
            s_c = lax.dot_general(ak.astype(BF16), v, (((0,), (0,)), ((), ())), preferred_element_type=F32)
            n_c = jnp.sum(ak, axis=0, keepdims=True)
            m_new = jnp.maximum(gtot + m_prev, m_loc)
            decay = jnp.exp(gtot + m_prev - m_new)
            add = jnp.exp(m_loc - m_new)
            s_scr[idx] = decay * s_prev + add * s_c
            n_scr[idx] = decay * n_prev + add * n_c
            m_scr[idx] = jnp.broadcast_to(m_new, (1, HEAD_DIM))


def _mlstm_scan(qm, km, z, gates, gates_t, gate_b, *, batch, seq):
    n = qm.shape[0]
    nc = seq // M_CHUNK
    width = N_M_HEADS * HEAD_DIM
    ng = 4 * N_M_HEADS
    chains = 2 * batch * N_M_HEADS
    fwd = lambda c: (0, c, 0)
    bwd = lambda c: (0, nc - 1 - c, 0)
    fwd_v = lambda c: (0, c, 5)
    bwd_v = lambda c: (0, nc - 1 - c, 5)
    fwd_t = lambda c: (0, 0, c)
    bwd_t = lambda c: (0, 0, nc - 1 - c)
    blk = (batch, M_CHUNK, width)
    gblk = (batch, M_CHUNK, LANES)
    tblk = (batch, ng, M_CHUNK)
    bias = jnp.zeros((1, LANES), F32).at[0, :ng].set(gate_b)
    bias_t = jnp.broadcast_to(gate_b[:, None], (ng, M_CHUNK))
    const = lambda c: (0, 0)
    in_specs = [
        pl.BlockSpec(blk, fwd), pl.BlockSpec(blk, fwd), pl.BlockSpec(blk, fwd_v),
        pl.BlockSpec(gblk, fwd), pl.BlockSpec(tblk, fwd_t),
        pl.BlockSpec(blk, bwd), pl.BlockSpec(blk, bwd), pl.BlockSpec(blk, bwd_v),
        pl.BlockSpec(gblk, bwd), pl.BlockSpec(tblk, bwd_t),
        pl.BlockSpec((1, LANES), const), pl.BlockSpec((ng, M_CHUNK), const),
    ]
    q3, k3, z3, g3 = (a.reshape(batch, seq, a.shape[1]) for a in (qm, km, z, gates))
    hf, hb = pl.pallas_call(
        _mscan_kernel,
        grid=(nc,),
        in_specs=in_specs,
        out_specs=[pl.BlockSpec(blk, fwd), pl.BlockSpec(blk, bwd)],
        out_shape=[jax.ShapeDtypeStruct((batch, seq, width), F32)] * 2,
        scratch_shapes=[pltpu.VMEM((chains, HEAD_DIM, HEAD_DIM), F32),
                        pltpu.VMEM((chains, 1, HEAD_DIM), F32),
                        pltpu.VMEM((chains, 1, HEAD_DIM), F32)],
        compiler_params=_cparams(("arbitrary",)),
        name="mlstm_scan",
    )(q3, k3, z3, g3, gates_t, q3, k3, z3, g3, gates_t, bias, bias_t)
    return hf.reshape(n, width), hb.reshape(n, width)


def _mout_kernel(hf_ref, hb_ref, o_ref, g_ref, y_ref):
    for h in range(N_M_HEADS):
        sl = slice(h * HEAD_DIM, (h + 1) * HEAD_DIM)
        x = hf_ref[:, sl] + hb_ref[:, sl]
        ms = jnp.mean(x * x, axis=-1, keepdims=True)
        hn = x * lax.rsqrt(ms + EPS) * g_ref[...]
        y_ref[:, sl] = (jax.nn.sigmoid(o_ref[:, sl].astype(F32)) * hn).astype(y_ref.dtype)


def _mlstm_out(hf, hb, z, norm_g, *, tm=1024):
    n, width = hf.shape
    spec = pl.BlockSpec((tm, width), lambda i: (i, 0))
    return pl.pallas_call(
        _mout_kernel,
        grid=(n // tm,),
        in_specs=[spec, spec, pl.BlockSpec((tm, width), lambda i: (i, 6)),
                  pl.BlockSpec((1, HEAD_DIM), lambda i: (0, 0))],
        out_specs=spec,
        out_shape=jax.ShapeDtypeStruct((n, width), BF16),
        compiler_params=_cparams(("arbitrary",)),
        name="mlstm_out",
    )(hf, hb, z, norm_g.reshape(1, HEAD_DIM))


def _top2(logit):
    lane = lax.broadcasted_iota(jnp.int32, logit.shape, 1).astype(F32)
    logit = jnp.where(lane < N_EXPERTS, logit, -jnp.inf)
    m1 = jnp.max(logit, axis=1, keepdims=True)
    i1 = jnp.min(jnp.where(logit == m1, lane, float(LANES)), axis=1, keepdims=True)
    rest = jnp.where(lane == i1, -jnp.inf, logit)
    m2 = jnp.max(rest, axis=1, keepdims=True)
    i2 = jnp.min(jnp.where(rest == m2, lane, float(LANES)), axis=1, keepdims=True)
    e = jnp.exp(m2 - m1)
    p1 = 1.0 / (1.0 + e)
    p2 = e / (1.0 + e)
    comb = jnp.where(lane == i1, p1, 0.0) + jnp.where(lane == i2, p2, 0.0)
    sel = jnp.where((lane == i1) | (lane == i2), 1.0, 0.0)
    return comb, sel


def _pack_halves(y):
    half = y.shape[1] // 2
    lo = pltpu.bitcast(y[:, :half].astype(BF16).astype(F32), jnp.uint32)
    hi = pltpu.bitcast(y[:, half:].astype(BF16).astype(F32), jnp.uint32)
    return hi | (lo >> 16)


def _unpack_halves(w):
    lo = pltpu.bitcast(w << 16, F32).astype(BF16)
    hi = pltpu.bitcast(w & jnp.uint32(0xFFFF0000), F32).astype(BF16)
    return lo, hi


def _mod_norm(x, g, sc=None, sh=None):
    ms = jnp.mean(x * x, axis=-1, keepdims=True)
    y = x * lax.rsqrt(ms + EPS) * g
    return y if sc is None else y * (1.0 + sc) + sh


def _out_proj_kernel(*refs, n_a, route):
    a_refs = refs[:n_a]
    w_ref, x_ref, g1_ref, ng_ref, sc_ref, sh_ref = refs[n_a:n_a + 6]
    if route:
        rw_ref, rb_ref, xo_ref, hp_ref, comb_ref, sel_ref, wb = refs[n_a + 6:]
    else:
        xo_ref, h_ref, wb = refs[n_a + 6:]

    @pl.when(pl.program_id(0) == 0)
    def _():
        wb[...] = w_ref[...].astype(BF16)

    acc, k0 = None, 0
    for a_ref in a_refs:
        kk = a_ref.shape[1]
        part = jnp.dot(a_ref[...], wb[k0:k0 + kk, :], preferred_element_type=F32)
        acc = part if acc is None else acc + part
        k0 += kk
    xn = x_ref[...] + g1_ref[0] * acc
    xo_ref[...] = xn
    y = _mod_norm(xn, ng_ref[...], sc_ref[0], sh_ref[0])
    if route:
        hp_ref[...] = _pack_halves(y)
        y_hi = y.astype(BF16)
        y_lo = (y - y_hi.astype(F32)).astype(BF16)
        logit = (jnp.dot(y_hi, rw_ref[0], preferred_element_type=F32)
                 + jnp.dot(y_lo, rw_ref[0], preferred_element_type=F32)
                 + jnp.dot(y_hi, rw_ref[1], preferred_element_type=F32)) + rb_ref[...]
        comb_ref[...], sel_ref[...] = _top2(logit)
    else:
        h_ref[...] = y.astype(h_ref.dtype)


def _out_proj(a_list, w_out, layer, x, g1, norm_g, sc, sh, router=None, *, rows_per_batch, tm=512):
    n, d = x.shape
    tpb = rows_per_batch // tm
    route = router is not None
    row = lambda w: pl.BlockSpec((tm, w), lambda i: (i, 0))
    per_batch = pl.BlockSpec((1, 1, d), lambda i: (i // tpb, 0, 0))
    const = lambda shape: pl.BlockSpec(shape, lambda i: (0,) * len(shape))
    in_specs = [row(a.shape[1]) for a in a_list]
    in_specs += [pl.BlockSpec((None, d, d), lambda i: (layer, 0, 0), pipeline_mode=pl.Buffered(1)), row(d), per_batch,
                 const((1, d)), per_batch, per_batch]
    args = list(a_list) + [w_out, x, g1, norm_g.reshape(1, d), sc, sh]
    out_specs = [row(d)]
    out_shape = [jax.ShapeDtypeStruct((n, d), F32)]
    if route:
        router_w, router_b = router
        rw = _pad_cols(router_w, LANES)
        rw_hi = rw.astype(BF16)
        rw_split = jnp.stack([rw_hi, (rw - rw_hi.astype(F32)).astype(BF16)])
        in_specs += [const((2, d, LANES)), const((1, LANES))]
        args += [rw_split, jnp.zeros((1, LANES), F32).at[0, :N_EXPERTS].set(router_b)]
        out_specs += [row(d // 2), row(LANES), row(LANES)]
        out_shape += [jax.ShapeDtypeStruct((n, d // 2), jnp.uint32), jax.ShapeDtypeStruct((n, LANES), F32),
                      jax.ShapeDtypeStruct((n, LANES), F32)]
    else:
        out_specs.append(row(d))
        out_shape.append(jax.ShapeDtypeStruct((n, d), BF16))
    return pl.pallas_call(
        functools.partial(_out_proj_kernel, n_a=len(a_list), route=route),
        grid=(n // tm,),
        in_specs=in_specs,
        out_specs=out_specs,
        out_shape=out_shape,
        scratch_shapes=[pltpu.VMEM((d, d), BF16)],
        compiler_params=_cparams(("arbitrary",)),
        name="out_proj_route" if route else "out_proj",
    )(*args)


def _rank_kernel(sel_ref, rank_ref, tot_ref, carry):
    @pl.when(pl.program_id(0) == 0)
    def _():
        carry[...] = jnp.zeros(carry.shape, F32)

    sel = sel_ref[...]
    tb = sel.shape[0]
    r = lax.broadcasted_iota(jnp.int32, (tb, tb), 0)
    c = lax.broadcasted_iota(jnp.int32, (tb, tb), 1)
    tri = jnp.where(r >= c, 1.0, 0.0).astype(BF16)
    incl = jnp.dot(tri, sel.astype(BF16), preferred_element_type=F32)
    rank_ref[...] = incl - sel + carry[...]
    total = carry[...] + incl[tb - 1:tb, :]
    carry[...] = total
    tot_ref[...] = total


def _rank(sel, *, tb=512):
    n = sel.shape[0]
    return pl.pallas_call(
        _rank_kernel,
        grid=(n // tb,),
        in_specs=[pl.BlockSpec((tb, LANES), lambda i: (i, 0))],
        out_specs=[pl.BlockSpec((tb, LANES), lambda i: (i, 0)), pl.BlockSpec((1, LANES), lambda i: (0, 0))],
        out_shape=[jax.ShapeDtypeStruct((n, LANES), F32), jax.ShapeDtypeStruct((1, LANES), F32)],
        scratch_shapes=[pltpu.VMEM((1, LANES), F32)],
        compiler_params=_cparams(("arbitrary",)),
        name="moe_rank",
    )(sel)


def _dest_kernel(rank_ref, sel_ref, comb_ref, off_ref, dest_ref, p_ref):
    lane = lax.broadcasted_iota(jnp.int32, rank_ref.shape, 1).astype(F32)
    sel = sel_ref[...] > 0.5
    pos = rank_ref[...] + off_ref[...]
    la = jnp.min(jnp.where(sel, lane, float(LANES)), axis=1, keepdims=True)
    lb = jnp.max(jnp.where(sel, lane, -1.0), axis=1, keepdims=True)
    pick = lambda l, v: jnp.sum(jnp.where(lane == l, v, 0.0), axis=1, keepdims=True)
    two = lambda a, b: jnp.where(lane == 0.0, a, jnp.where(lane == 1.0, b, 0.0))
    dest_ref[...] = two(pick(la, pos), pick(lb, pos)).astype(jnp.int32)
    p_ref[...] = two(pick(la, comb_ref[...]), pick(lb, comb_ref[...]))


def _dest(rank, sel, comb, off, *, tb=1024):
    n = rank.shape[0]
    spec = pl.BlockSpec((tb, LANES), lambda i: (i, 0))
    return pl.pallas_call(
        _dest_kernel,
        grid=(n // tb,),
        in_specs=[spec, spec, spec, pl.BlockSpec((1, LANES), lambda i: (0, 0))],
        out_specs=[spec, spec],
        out_shape=[jax.ShapeDtypeStruct((n, LANES), jnp.int32), jax.ShapeDtypeStruct((n, LANES), F32)],
        compiler_params=_cparams(("arbitrary",)),
        name="moe_dest",
    )(rank, sel, comb, off)


def _row_copy(src, src_row, dst, dst_row, sem):
    return pltpu.make_async_copy(src.at[pl.ds(src_row, 1), :], dst.at[pl.ds(dst_row, 1), :], sem)


def _dispatch_kernel(dest_ref, hp_ref, xs_in_ref, xs_ref, sem):
    del xs_in_ref
    tb = hp_ref.shape[0]
    for r in range(tb):
        for s in range(2):
            _row_copy(hp_ref, r, xs_ref, dest_ref[s, r], sem).start(priority=s)
    for s in range(2):
        pltpu.make_async_copy(hp_ref, xs_ref.at[pl.ds(0, tb), :], sem).wait()


def _dispatch(hp, dest3, rows, *, tb):
    n, w = hp.shape
    xs0 = jnp.zeros((rows, w), hp.dtype)
    return pl.pallas_call(
        _dispatch_kernel,
        grid=(n // tb,),
        in_specs=[pl.BlockSpec((None, 2, tb), lambda i: (i, 0, 0), memory_space=pltpu.SMEM),
                  pl.BlockSpec((tb, w), lambda i: (i, 0)),
                  pl.BlockSpec(memory_space=pl.ANY)],
        out_specs=pl.BlockSpec(memory_space=pl.ANY),
        out_shape=jax.ShapeDtypeStruct((rows, w), hp.dtype),
        scratch_shapes=[pltpu.SemaphoreType.DMA(())],
        input_output_aliases={2: 0},
        compiler_params=_cparams(("arbitrary",)),
        name="moe_dispatch",
    )(dest3, hp, xs0)


def _expert_changed(te_ref, t):
    return (t == 0) | (te_ref[t] != te_ref[jnp.maximum(t - 1, 0)])


def _moe_up_kernel(te_ref, nu_ref, xs_ref, w1_ref, w3_ref, o_ref, w1b, w3b):
    t = pl.program_id(1)

    @pl.when(_expert_changed(te_ref, t))
    def _():
        w1b[...] = w1_ref[...].astype(BF16)
        w3b[...] = w3_ref[...].astype(BF16)

    @pl.when(t < nu_ref[0])
    def _():
        x = jnp.concatenate(_unpack_halves(xs_ref[...]), axis=1)
        a1 = jnp.dot(x, w1b[...], preferred_element_type=F32)
        a3 = jnp.dot(x, w3b[...], preferred_element_type=F32)
        o_ref[...] = (_silu(a1) * a3).astype(o_ref.dtype)

    @pl.when(t >= nu_ref[0])
    def _():
        o_ref[...] = jnp.zeros(o_ref.shape, o_ref.dtype)


def _moe_up(xs, w1, w3, layer, te, nu, *, tg, tn):
    rows, half = xs.shape
    _, e, d, f = w1.shape
    wspec = pl.BlockSpec((None, None, d, tn), lambda j, t, te, nu: (layer, te[t], 0, j))
    return pl.pallas_call(
        _moe_up_kernel,
        grid_spec=pltpu.PrefetchScalarGridSpec(
            num_scalar_prefetch=2,
            grid=(f // tn, rows // tg),
            in_specs=[pl.BlockSpec((tg, half), lambda j, t, te, nu: (t, 0)), wspec, wspec],
            out_specs=pl.BlockSpec((tg, tn), lambda j, t, te, nu: (t, j)),
            scratch_shapes=[pltpu.VMEM((d, tn), BF16)] * 2),
        out_shape=jax.ShapeDtypeStruct((rows, f), BF16),
        compiler_params=_cparams(("arbitrary", "arbitrary")),
        name="moe_up",
    )(te, nu, xs, w1, w3)


def _moe_down_kernel(te_ref, nu_ref, a_ref, w2_ref, o_ref, w2b):
    t = pl.program_id(1)

    @pl.when(_expert_changed(te_ref, t))
    def _():
        w2b[...] = w2_ref[...].astype(BF16)

    @pl.when(t < nu_ref[0])
    def _():
        o_ref[...] = _pack_halves(jnp.dot(a_ref[...], w2b[...], preferred_element_type=F32))

    @pl.when(t >= nu_ref[0])
    def _():
        o_ref[...] = jnp.zeros(o_ref.shape, o_ref.dtype)


def _moe_down(a, w2, layer, te, nu, *, tg, tn):
    rows, f = a.shape
    d = w2.shape[-1]
    return pl.pallas_call(
        _moe_down_kernel,
        grid_spec=pltpu.PrefetchScalarGridSpec(
            num_scalar_prefetch=2,
            grid=(d // tn, rows // tg),
            in_specs=[pl.BlockSpec((tg, f), lambda j, t, te, nu: (t, 0)),
                      pl.BlockSpec((None, None, f, tn), lambda j, t, te, nu: (layer, te[t], 0, j))],
            out_specs=pl.BlockSpec((tg, tn // 2), lambda j, t, te, nu: (t, j)),
            scratch_shapes=[pltpu.VMEM((f, tn), BF16)]),
        out_shape=jax.ShapeDtypeStruct((rows, d // 2), jnp.uint32),
        compiler_params=_cparams(("arbitrary", "arbitrary")),
        name="moe_down",
    )(te, nu, a, w2)


def _combine_kernel(*refs, modulate, pack_w):
    if modulate:
        dest_ref, x_ref, g_ref, p_ref, ng_ref, sc_ref, sh_ref, ys_ref, o_ref, h_ref, buf, sem = refs
    else:
        dest_ref, x_ref, g_ref, p_ref, ng_ref, ys_ref, o_ref, h_ref, buf, sem = refs
    tb = x_ref.shape[0]
    for r in range(tb):
        for s in range(2):
            _row_copy(ys_ref, dest_ref[s, r], buf.at[s], r, sem).start(priority=s)
    for s in range(2):
        pltpu.make_async_copy(ys_ref.at[pl.ds(0, tb), :], buf.at[s], sem).wait()

    def expert_rows(s):
        w = buf[s]
        parts = []
        for c in range(w.shape[1] // pack_w):
            wc = w[:, c * pack_w:(c + 1) * pack_w]
            parts += [pltpu.bitcast(wc << 16, F32), pltpu.bitcast(wc & jnp.uint32(0xFFFF0000), F32)]
        return jnp.concatenate(parts, axis=1)

    p = p_ref[...]
    xn = x_ref[...] + g_ref[0] * (p[:, 0:1] * expert_rows(0) + p[:, 1:2] * expert_rows(1))
    o_ref[...] = xn
    if modulate:
        h_ref[...] = _mod_norm(xn, ng_ref[...], sc_ref[0], sh_ref[0]).astype(h_ref.dtype)
    else:
        h_ref[...] = _mod_norm(xn, ng_ref[...]).astype(h_ref.dtype)


def _combine(x, g, p, dest3, ys, norm_g, sc, sh, *, rows_per_batch, tb, h_dtype, pack_w):
    n, d = x.shape
    tpb = rows_per_batch // tb
    modulate = sc is not None
    row = pl.BlockSpec((tb, d), lambda i: (i, 0))
    per_batch = pl.BlockSpec((1, 1, d), lambda i: (i // tpb, 0, 0))
    in_specs = [pl.BlockSpec((None, 2, tb), lambda i: (i, 0, 0), memory_space=pltpu.SMEM), row, per_batch,
                pl.BlockSpec((tb, LANES), lambda i: (i, 0)), pl.BlockSpec((1, d), lambda i: (0, 0))]
    args = [dest3, x, g, p, norm_g.reshape(1, d)]
    if modulate:
        in_specs += [per_batch, per_batch]
        args += [sc, sh]
    return pl.pallas_call(
        functools.partial(_combine_kernel, modulate=modulate, pack_w=pack_w),
        grid=(n // tb,),
        in_specs=in_specs + [pl.BlockSpec(memory_space=pl.ANY)],
        out_specs=[row, row],
        out_shape=[jax.ShapeDtypeStruct((n, d), F32), jax.ShapeDtypeStruct((n, d), h_dtype)],
        scratch_shapes=[pltpu.VMEM((2, tb, d // 2), jnp.uint32), pltpu.SemaphoreType.DMA(())],
        compiler_params=_cparams(("arbitrary",)),
        name="moe_combine",
    )(*args, ys)


def _moe_ffn(x, hp, comb, sel, g, w1, w3, w2, layer, next_norm, *, rows_per_batch, h_dtype, tg=512, tb=256,
             tn_up=896, tn_down=1024):
    n, d = x.shape
    e = w1.shape[1]
    t_max = 2 * n // tg + e
    rank, tot = _rank(sel)
    cnt = tot[0, :e].astype(jnp.int32)
    tiles = (cnt + tg - 1) // tg
    tile_end = jnp.cumsum(tiles)
    n_used = tile_end[-1]
    off = jnp.zeros((1, LANES), F32).at[0, :e].set(((tile_end - tiles) * tg).astype(F32))
    te = jnp.sum(jnp.arange(t_max, dtype=jnp.int32)[:, None] >= tile_end[None, :], axis=1).astype(jnp.int32)
    te = jnp.minimum(te, te[jnp.maximum(n_used - 1, 0)])
    nu = n_used.reshape(1).astype(jnp.int32)
    dest, p = _dest(rank, sel, comb, off)
    dest3 = dest[:, :2].T.reshape(2, n // tb, tb).transpose(1, 0, 2)
    xs = _dispatch(hp, dest3, t_max * tg, tb=tb)
    a = _moe_up(xs, w1, w3, layer, te, nu, tg=tg, tn=tn_up)
    ys = _moe_down(a, w2, layer, te, nu, tg=tg, tn=tn_down)
    return _combine(x, g, p, dest3, ys, *next_norm, rows_per_batch=rows_per_batch, tb=tb, h_dtype=h_dtype,
                    pack_w=tn_down // 2)


def _pad_cols(w, width):
    return jnp.zeros((w.shape[0], width), w.dtype).at[:, :w.shape[1]].set(w)


def kernel(x, c, ada_w, ada_b, norm1_g, norm2_g, w_in, w_out, pool_w, pool_scale, q_norm_g, k_norm_g,
           m_conv_w, m_wq, m_wk, m_gate_b, m_norm_g, ffn_w1, ffn_w3, ffn_w2, router_w, router_b,
           moe_w1, moe_w3, moe_w2, final_norm_g):
    batch, seq, d = x.shape
    depth = ada_w.shape[0]
    n = batch * seq
    ng = 4 * N_M_HEADS
    d_main = w_in.shape[2] - ng
    xf = x.reshape(n, d)
    mod = _adaln_mod(c, ada_w, ada_b)
    cos, sin = _rope_tables(seq)
    dims = dict(batch=batch, seq=seq)

    mods = [[mod[l, :, i * d:(i + 1) * d].reshape(batch, 1, d) for i in range(6)] for l in range(depth)]
    h = None
    for l in range(depth):
        sh1, sc1, g1, sh2, sc2, g2 = mods[l]
        if h is None:
            h = _norm(xf, norm1_g[l], sc1, sh1, rows_per_batch=seq, out_dtype=BF16)
        z = _in_proj(h, w_in, l, cos, sin, q_norm_g[l], k_norm_g[l], seq=seq, ncols=d_main)
        gates = _mm([h], w_in, (l,), tm=1024, tn=LANES, ncols=LANES, col_blk0=d_main // LANES, valid_cols=ng)
        y_pool = _pool_mixer(z, pool_w[l], pool_scale[l], **dims)
        y_attn = _attention(z, q_col=pool_w.shape[1] * pool_w.shape[2], tq=4096, tk=256, **dims)
        qm, km = _mlstm_qk(z, m_conv_w[l], m_wq[l], m_wk[l], **dims)
        gates_t = gates[:, :ng].reshape(batch, seq, ng).transpose(0, 2, 1)
        hf, hb = _mlstm_scan(qm, km, z, gates, gates_t, m_gate_b[l], **dims)
        y_m = _mlstm_out(hf, hb, z, m_norm_g[l])
        mixed = [y_pool, y_attn, y_m]
        i = l // 2
        if l % 2 == 0:
            xf, h2 = _out_proj(mixed, w_out, l, xf, g1, norm2_g[l], sc2, sh2, rows_per_batch=seq)
            a = _swiglu_up(h2, ffn_w1, ffn_w3, (i,), tm=1024, tn=512)
            xf = _mm([a], ffn_w2, (i,), tm=512, tn=512, x=xf, g=g2, rows_per_batch=seq)
            h = None
        else:
            xf, hp, comb, sel = _out_proj(mixed, w_out, l, xf, g1, norm2_g[l], sc2, sh2,
                                          (router_w[i], router_b[i]), rows_per_batch=seq)
            last = l == depth - 1
            next_norm = (final_norm_g, None, None) if last else (norm1_g[l + 1], mods[l + 1][1], mods[l + 1][0])
            xf, h = _moe_ffn(xf, hp, comb, sel, g2, moe_w1, moe_w3, moe_w2, i, next_norm, rows_per_batch=seq,
                             h_dtype=F32 if last else BF16)
            if last:
                return h.reshape(batch, seq, d)

    return _norm(xf, final_norm_g, rows_per_batch=seq, out_dtype=F32).reshape(batch, seq, d)
```

```python
import functools
import math

import numpy as np
import jax
import jax.numpy as jnp
from jax import lax
from jax.experimental import pallas as pl
from jax.experimental.pallas import tpu as pltpu

F32 = jnp.float32
BF16 = jnp.bfloat16

EPS = 1e-6
HEAD_DIM = 128
GRID_W = 64
ROPE_THETA = 10000.0
POOL_WINDOWS = (2, 4, 8, 16)
M_CHUNK = 128
M_CONV_W = 5
N_M_HEADS = 4
N_Q_HEADS = 8
N_KV_HEADS = 2
Q_PER_KV = N_Q_HEADS // N_KV_HEADS
N_EXPERTS = 8
LANES = 128
HALO = 16
VMEM_LIMIT = 56 * 1024 * 1024
Z_BLOCK = 512
Z_POOL, Z_Q, Z_KV, Z_MU, Z_MV, Z_MO = 0, 1, 3, 4, 5, 6
LOG2E = math.log2(math.e)


def _cparams(sem, flags=None):
    return pltpu.CompilerParams(dimension_semantics=sem, vmem_limit_bytes=VMEM_LIMIT, flags=flags)


def _silu(a):
    return a * jax.nn.sigmoid(a)


def _mod_kernel(c_ref, w_ref, b_ref, o_ref):
    ca = _silu(c_ref[...])
    o_ref[0] = jnp.dot(ca.astype(BF16), w_ref[0].astype(BF16), preferred_element_type=F32) + b_ref[0]


def _adaln_mod(c, ada_w, ada_b, tn=768):
    depth, d, n6 = ada_w.shape
    b = c.shape[0]
    cp = jnp.zeros((8, d), F32).at[:b].set(c)
    out = pl.pallas_call(
        _mod_kernel,
        grid=(depth, n6 // tn),
        in_specs=[
            pl.BlockSpec((8, d), lambda l, j: (0, 0)),
            pl.BlockSpec((1, d, tn), lambda l, j: (l, 0, j)),
            pl.BlockSpec((1, 1, tn), lambda l, j: (l, 0, j)),
        ],
        out_specs=pl.BlockSpec((1, 8, tn), lambda l, j: (l, 0, j)),
        out_shape=jax.ShapeDtypeStruct((depth, 8, n6), F32),
        compiler_params=_cparams(("arbitrary", "arbitrary")),
        name="adaln_mod",
    )(cp, ada_w, ada_b.reshape(depth, 1, n6))
    return out[:, :b]


def _norm_kernel(*refs, modulate):
    if modulate:
        x_ref, g_ref, sc_ref, sh_ref, o_ref = refs
    else:
        x_ref, g_ref, o_ref = refs
    x = x_ref[...]
    ms = jnp.mean(x * x, axis=-1, keepdims=True)
    y = x * lax.rsqrt(ms + EPS) * g_ref[...]
    if modulate:
        y = y * (1.0 + sc_ref[0]) + sh_ref[0]
    o_ref[...] = y.astype(o_ref.dtype)


def _norm(x, g, sc=None, sh=None, *, rows_per_batch, out_dtype, tm=1024):
    n, d = x.shape
    tpb = rows_per_batch // tm
    modulate = sc is not None
    in_specs = [pl.BlockSpec((tm, d), lambda i: (i, 0)), pl.BlockSpec((1, d), lambda i: (0, 0))]
    args = [x, g.reshape(1, d)]
    if modulate:
        in_specs += [pl.BlockSpec((1, 1, d), lambda i: (i // tpb, 0, 0))] * 2
        args += [sc, sh]
    return pl.pallas_call(
        functools.partial(_norm_kernel, modulate=modulate),
        grid=(n // tm,),
        in_specs=in_specs,
        out_specs=pl.BlockSpec((tm, d), lambda i: (i, 0)),
        out_shape=jax.ShapeDtypeStruct((n, d), out_dtype),
        compiler_params=_cparams(("arbitrary",)),
        name="rms_norm",
    )(*args)


def _mm_kernel(*refs, n_a, resid, valid_cols):
    a_refs, w_ref = refs[:n_a], refs[n_a]
    if resid:
        x_ref, g_ref, o_ref, wb = refs[n_a + 1:]
    else:
        o_ref, wb = refs[n_a + 1:]

    @pl.when(pl.program_id(1) == 0)
    def _():
        w = w_ref[...]
        if valid_cols is not None:
            w = jnp.where(lax.broadcasted_iota(jnp.int32, w.shape, 1) < valid_cols, w, 0.0)
        wb[...] = w.astype(BF16)

    acc, k0 = None, 0
    for a_ref in a_refs:
        kk = a_ref.shape[1]
        part = jnp.dot(a_ref[...], wb[k0:k0 + kk, :], preferred_element_type=F32)
        acc = part if acc is None else acc + part
        k0 += kk
    if resid:
        o_ref[...] = x_ref[...] + g_ref[0] * acc
    else:
        o_ref[...] = acc.astype(o_ref.dtype)


def _mm(a_list, w, w_idx, *, tm, tn, ncols=None, col_blk0=0, valid_cols=None, out_dtype=F32,
        x=None, g=None, rows_per_batch=None):
    m = a_list[0].shape[0]
    kdim = w.shape[-2]
    ncols = w.shape[-1] if ncols is None else ncols
    resid = x is not None
    lead = (None,) * len(w_idx)
    in_specs = [pl.BlockSpec((tm, a.shape[1]), lambda j, i: (i, 0)) for a in a_list]
    in_specs.append(pl.BlockSpec(lead + (kdim, tn), lambda j, i: tuple(w_idx) + (0, j + col_blk0)))
    args = list(a_list) + [w]
    if resid:
        tpb = rows_per_batch // tm
        in_specs += [pl.BlockSpec((tm, tn), lambda j, i: (i, j)),
                     pl.BlockSpec((1, 1, tn), lambda j, i: (i // tpb, 0, j))]
        args += [x, g]
    return pl.pallas_call(
        functools.partial(_mm_kernel, n_a=len(a_list), resid=resid, valid_cols=valid_cols),
        grid=(ncols // tn, m // tm),
        in_specs=in_specs,
        out_specs=pl.BlockSpec((tm, tn), lambda j, i: (i, j)),
        out_shape=jax.ShapeDtypeStruct((m, ncols), out_dtype),
        scratch_shapes=[pltpu.VMEM((kdim, tn), BF16)],
        compiler_params=_cparams(("arbitrary", "arbitrary")),
        name="matmul_resid" if resid else "matmul",
    )(*args)


def _head_norm_rope(x, g, cos, sin, ones, perm, scale):
    ss = jnp.dot((x * x).astype(BF16), ones, preferred_element_type=F32)
    y = x * lax.rsqrt(ss * (1.0 / HEAD_DIM) + EPS) * g
    rot = jnp.dot(y.astype(BF16), perm, preferred_element_type=F32)
    return (y * cos + rot * sin) * scale


def _win_kernel(h_ref, w_ref, cos_ref, sin_ref, qg_ref, kg_ref, o_ref, wb):
    j = pl.program_id(0)

    @pl.when(pl.program_id(1) == 0)
    def _():
        wb[...] = w_ref[...].astype(BF16)

    acc = jnp.dot(h_ref[...], wb[...], preferred_element_type=F32)

    def store_roped(g_ref, scale, n_rope):
        cos = cos_ref[...]
        sin = sin_ref[...]
        src = lax.broadcasted_iota(jnp.int32, (HEAD_DIM, HEAD_DIM), 0)
        dst = lax.broadcasted_iota(jnp.int32, (HEAD_DIM, HEAD_DIM), 1)
        quarter = HEAD_DIM // 4
        partner = jnp.where((dst % (2 * quarter)) < quarter, dst + quarter, dst - quarter)
        perm = jnp.where(src == partner, 1.0, 0.0).astype(BF16)
        ones = jnp.ones((HEAD_DIM, HEAD_DIM), BF16)
        for hh in range(acc.shape[1] // HEAD_DIM):
            sl = slice(hh * HEAD_DIM, (hh + 1) * HEAD_DIM)
            xh = acc[:, sl]
            if hh < n_rope:
                xh = _head_norm_rope(xh, g_ref[...], cos, sin, ones, perm, scale)
            o_ref[:, sl] = xh.astype(o_ref.dtype)

    @pl.when((j >= Z_Q) & (j < Z_KV))
    def _():
        store_roped(qg_ref, LOG2E / math.sqrt(HEAD_DIM), Z_BLOCK // HEAD_DIM)

    @pl.when(j == Z_KV)
    def _():
        store_roped(kg_ref, 1.0, N_KV_HEADS)

    @pl.when((j < Z_Q) | (j > Z_KV))
    def _():
        o_ref[...] = acc.astype(o_ref.dtype)


def _in_proj(h, w_in, layer, cos, sin, qg, kg, *, seq, ncols, tm=2048, tn=Z_BLOCK):
    n, d = h.shape
    tps = seq // tm
    return pl.pallas_call(
        _win_kernel,
        grid=(ncols // tn, n // tm),
        in_specs=[pl.BlockSpec((tm, d), lambda j, i: (i, 0)),
                  pl.BlockSpec((None, d, tn), lambda j, i: (layer, 0, j)),
                  pl.BlockSpec((tm, HEAD_DIM), lambda j, i: (i % tps, 0)),
                  pl.BlockSpec((tm, HEAD_DIM), lambda j, i: (i % tps, 0)),
                  pl.BlockSpec((1, HEAD_DIM), lambda j, i: (0, 0)),
                  pl.BlockSpec((1, HEAD_DIM), lambda j, i: (0, 0))],
        out_specs=pl.BlockSpec((tm, tn), lambda j, i: (i, j)),
        out_shape=jax.ShapeDtypeStruct((n, ncols), BF16),
        scratch_shapes=[pltpu.VMEM((d, tn), BF16)],
        compiler_params=_cparams(("arbitrary", "arbitrary")),
        name="in_proj",
    )(h, w_in, cos, sin, qg.reshape(1, HEAD_DIM), kg.reshape(1, HEAD_DIM))


def _up_kernel(h_ref, w1_ref, w3_ref, o_ref, w1b, w3b):
    @pl.when(pl.program_id(1) == 0)
    def _():
        w1b[...] = w1_ref[...].astype(BF16)
        w3b[...] = w3_ref[...].astype(BF16)

    h = h_ref[...]
    a1 = jnp.dot(h, w1b[...], preferred_element_type=F32)
    a3 = jnp.dot(h, w3b[...], preferred_element_type=F32)
    o_ref[...] = (_silu(a1) * a3).astype(o_ref.dtype)


def _swiglu_up(h, w1, w3, w_idx, *, tm, tn):
    n, d = h.shape
    f = w1.shape[-1]
    lead = (None,) * len(w_idx)
    wspec = pl.BlockSpec(lead + (d, tn), lambda j, i: tuple(w_idx) + (0, j))
    return pl.pallas_call(
        _up_kernel,
        grid=(f // tn, n // tm),
        in_specs=[pl.BlockSpec((tm, d), lambda j, i: (i, 0)), wspec, wspec],
        out_specs=pl.BlockSpec((tm, tn), lambda j, i: (i, j)),
        out_shape=jax.ShapeDtypeStruct((n, f), BF16),
        scratch_shapes=[pltpu.VMEM((d, tn), BF16)] * 2,
        compiler_params=_cparams(("arbitrary", "arbitrary")),
        name="swiglu_up",
    )(h, w1, w3)


def _halo_specs(tl, width, col_block, nl, batch):
    hb = tl // HALO
    nh = nl * hb
    last = batch * nh - 1
    cur = pl.BlockSpec((tl, width), lambda b, i: (b * nl + i, col_block))
    prev = pl.BlockSpec((HALO, width), lambda b, i: (jnp.maximum(b * nh + i * hb - 1, 0), col_block))
    nxt = pl.BlockSpec((HALO, width), lambda b, i: (jnp.minimum(b * nh + (i + 1) * hb, last), col_block))
    return prev, cur, nxt


def _with_halo(prev_ref, cur_ref, next_ref, nl):
    i = pl.program_id(1)
    prev = jnp.where(i == 0, 0.0, prev_ref[...].astype(F32))
    nxt = jnp.where(i == nl - 1, 0.0, next_ref[...].astype(F32))
    return jnp.concatenate([prev, cur_ref[...].astype(F32), nxt], axis=0)


def _pool_kernel(prev_ref, cur_ref, next_ref, w_ref, s_ref, o_ref, *, tl, nl, seq):
    u = _with_halo(prev_ref, cur_ref, next_ref, nl)
    rows = tl + 2 * HALO
    t = (pl.program_id(1) * tl + lax.broadcasted_iota(jnp.int32, (tl, LANES), 0)).astype(F32)
    for g, w in enumerate(POOL_WINDOWS):
        half = w // 2
        ug = u[:, g * LANES:(g + 1) * LANES]
        s = ug + pltpu.roll(ug, 1, 0)
        sh = 1
        while sh < half:
            s = pltpu.roll(s, sh, 0) + pltpu.roll(s, rows - sh, 0)
            sh *= 2
        cnt = jnp.minimum(t + (half - 1), seq - 1.0) - jnp.maximum(t - half, 0.0) + 1.0
        mean = s[HALO:HALO + tl] / cnt
        diff = mean - ug[HALO:HALO + tl]
        y = jnp.dot(diff.astype(BF16), w_ref[g], preferred_element_type=F32)
        o_ref[:, g * LANES:(g + 1) * LANES] = (y * s_ref[:, g * LANES:(g + 1) * LANES]).astype(o_ref.dtype)


def _pool_mixer(z, w_pool, scale, *, batch, seq, tl=1024):
    n = z.shape[0]
    nl = seq // tl
    width = len(POOL_WINDOWS) * LANES
    prev, cur, nxt = _halo_specs(tl, width, Z_POOL, nl, batch)
    return pl.pallas_call(
        functools.partial(_pool_kernel, tl=tl, nl=nl, seq=seq),
        grid=(batch, nl),
        in_specs=[prev, cur, nxt,
                  pl.BlockSpec((len(POOL_WINDOWS), LANES, LANES), lambda b, i: (0, 0, 0)),
                  pl.BlockSpec((1, width), lambda b, i: (0, 0))],
        out_specs=pl.BlockSpec((tl, width), lambda b, i: (b * nl + i, 0)),
        out_shape=jax.ShapeDtypeStruct((n, width), BF16),
        compiler_params=_cparams(("arbitrary", "arbitrary")),
        name="pool_mixer",
    )(z, z, z, w_pool.astype(BF16), scale.reshape(1, width))


def _rope_tables(seq):
    rows = seq // GRID_W
    row = jnp.broadcast_to(jnp.arange(rows, dtype=F32)[:, None], (rows, GRID_W)).reshape(seq)
    col = jnp.broadcast_to(jnp.arange(GRID_W, dtype=F32)[None, :], (rows, GRID_W)).reshape(seq)
    half = HEAD_DIM // 4
    inv = ROPE_THETA ** (-jnp.arange(half, dtype=F32) / half)
    ar = row[:, None] * inv[None, :]
    ac = col[:, None] * inv[None, :]
    cos = jnp.concatenate([jnp.cos(ar), jnp.cos(ar), jnp.cos(ac), jnp.cos(ac)], axis=-1)
    sin = jnp.concatenate([-jnp.sin(ar), jnp.sin(ar), -jnp.sin(ac), jnp.sin(ac)], axis=-1)
    return cos, sin


def _attn_kernel(q_ref, k_ref, v_ref, o_ref, m_scr, l_scr, acc_scr, *, nk, tk):
    ki = pl.program_id(3)

    @pl.when(ki == 0)
    def _():
        m_scr[...] = jnp.full(m_scr.shape, -jnp.inf, F32)
        l_scr[...] = jnp.zeros(l_scr.shape, F32)
        acc_scr[...] = jnp.zeros(acc_scr.shape, F32)

    k = k_ref[...]
    v = v_ref[...]
    for g in range(Q_PER_KV):
        q = q_ref[:, g * HEAD_DIM:(g + 1) * HEAD_DIM]
        s = lax.dot_general(q, k, (((1,), (1,)), ((), ())), preferred_element_type=F32)
        m_prev = m_scr[g]
        m_next = jnp.maximum(m_prev, jnp.max(s, axis=1, keepdims=True))
        p = jnp.exp2(s - jnp.concatenate([m_next] * (tk // LANES), axis=1))
        alpha = jnp.exp2(m_prev - m_next)
        l_scr[g] = alpha * l_scr[g] + jnp.sum(p, axis=1, keepdims=True)
        acc_scr[g] = alpha * acc_scr[g] + jnp.dot(p.astype(BF16), v, preferred_element_type=F32)
        m_scr[g] = m_next

    @pl.when(ki == nk - 1)
    def _():
        for g in range(Q_PER_KV):
            o_ref[:, g * HEAD_DIM:(g + 1) * HEAD_DIM] = (acc_scr[g] / l_scr[g]).astype(o_ref.dtype)


def _attention(z, *, batch, seq, q_col, tq=512, tk=512):
    n = z.shape[0]
    nq, nk = seq // tq, seq // tk
    qw = Q_PER_KV * HEAD_DIM
    qblk = q_col // qw
    kcol = q_col // HEAD_DIM + N_Q_HEADS
    vcol = kcol + N_KV_HEADS
    return pl.pallas_call(
        functools.partial(_attn_kernel, nk=nk, tk=tk),
        grid=(batch, N_KV_HEADS, nq, nk),
        in_specs=[pl.BlockSpec((tq, qw), lambda b, h, qi, ki: (b * nq + qi, qblk + h)),
                  pl.BlockSpec((tk, HEAD_DIM), lambda b, h, qi, ki: (b * nk + ki, kcol + h)),
                  pl.BlockSpec((tk, HEAD_DIM), lambda b, h, qi, ki: (b * nk + ki, vcol + h))],
        out_specs=pl.BlockSpec((tq, qw), lambda b, h, qi, ki: (b * nq + qi, h)),
        out_shape=jax.ShapeDtypeStruct((n, N_Q_HEADS * HEAD_DIM), BF16),
        scratch_shapes=[pltpu.VMEM((Q_PER_KV, tq, HEAD_DIM), F32)] * 3,
        compiler_params=_cparams(("arbitrary", "arbitrary", "arbitrary", "arbitrary")),
        name="flash_attention",
    )(z, z, z)


def _mconv_kernel(prev_ref, cur_ref, next_ref, cw_ref, wq_ref, wk_ref, q_ref, k_ref, *, tl, nl):
    u = _with_halo(prev_ref, cur_ref, next_ref, nl)
    rows = tl + 2 * HALO
    acc = None
    for kk in range(M_CONV_W):
        sh = (M_CONV_W // 2 - kk) % rows
        tap = (pltpu.roll(u, sh, 0) if sh else u) * cw_ref[kk:kk + 1, :]
        acc = tap if acc is None else acc + tap
    uc = _silu(acc[HALO:HALO + tl]).astype(BF16)
    for h in range(N_M_HEADS):
        sl = slice(h * HEAD_DIM, (h + 1) * HEAD_DIM)
        q_ref[:, sl] = jnp.dot(uc[:, sl], wq_ref[h], preferred_element_type=F32).astype(q_ref.dtype)
        kh = jnp.dot(uc[:, sl], wk_ref[h], preferred_element_type=F32) * (HEAD_DIM ** -0.5)
        k_ref[:, sl] = kh.astype(k_ref.dtype)


def _mlstm_qk(z, conv_w, wq, wk, *, batch, seq, tl=1024):
    n = z.shape[0]
    nl = seq // tl
    width = N_M_HEADS * HEAD_DIM
    prev, cur, nxt = _halo_specs(tl, width, Z_MU, nl, batch)
    wspec = pl.BlockSpec((N_M_HEADS, HEAD_DIM, HEAD_DIM), lambda b, i: (0, 0, 0))
    ospec = pl.BlockSpec((tl, width), lambda b, i: (b * nl + i, 0))
    return pl.pallas_call(
        functools.partial(_mconv_kernel, tl=tl, nl=nl),
        grid=(batch, nl),
        in_specs=[prev, cur, nxt, pl.BlockSpec((M_CONV_W, width), lambda b, i: (0, 0)), wspec, wspec],
        out_specs=[ospec, ospec],
        out_shape=[jax.ShapeDtypeStruct((n, width), BF16)] * 2,
        compiler_params=_cparams(("arbitrary", "arbitrary")),
        name="mlstm_conv_qk",
    )(z, z, z, conv_w, wq.astype(BF16), wk.astype(BF16))


def _split3(x):
    x1 = x.astype(BF16)
    r = x - x1.astype(F32)
    x2 = r.astype(BF16)
    x3 = (r - x2.astype(F32)).astype(BF16)
    return x1, x2, x3


def _mscan_kernel(qf_ref, kf_ref, vf_ref, gf_ref, gtf_ref, qb_ref, kb_ref, vb_ref, gb_ref, gtb_ref,
                  bias_ref, biast_ref, hf_ref, hb_ref, s_scr, n_scr, m_scr):
    c = pl.program_id(0)

    @pl.when(c == 0)
    def _():
        s_scr[...] = jnp.zeros(s_scr.shape, F32)
        n_scr[...] = jnp.zeros(n_scr.shape, F32)
        m_scr[...] = jnp.zeros(m_scr.shape, F32)

    ch = M_CHUNK
    ti = lax.broadcasted_iota(jnp.int32, (ch, ch), 0)
    si = lax.broadcasted_iota(jnp.int32, (ch, ch), 1)
    nh = N_M_HEADS

    for b, rev in [(b, rev) for b in range(qf_ref.shape[0]) for rev in (False, True)]:
        q_ref, k_ref, v_ref, g_ref, gt_ref, h_ref = (
            (qb_ref, kb_ref, vb_ref, gb_ref, gtb_ref, hb_ref) if rev else
            (qf_ref, kf_ref, vf_ref, gf_ref, gtf_ref, hf_ref))
        causal = (si >= ti) if rev else (si <= ti)
        tri = jnp.where(causal, 1.0, 0.0).astype(BF16)
        tri_t = jnp.where((ti >= si) if rev else (ti <= si), 1.0, 0.0).astype(BF16)
        gates = g_ref[b] + bias_ref[...]
        gates_t = gt_ref[b] + biast_ref[...]
        lf = jax.nn.log_sigmoid(gates)
        lf_t = jax.nn.log_sigmoid(gates_t)
        cum = sum(jnp.dot(tri, part, preferred_element_type=F32) for part in _split3(lf))
        cum_t = sum(jnp.dot(part, tri_t, preferred_element_type=F32) for part in _split3(lf_t))
        last = 0 if rev else ch - 1
        for h in range(nh):
            idx = (2 * b + (1 if rev else 0)) * nh + h
            icol = (2 * nh if rev else 0) + h
            fcol = icol + nh
            sl = slice(h * HEAD_DIM, (h + 1) * HEAD_DIM)
            q = q_ref[b, :, sl]
            k = k_ref[b, :, sl]
            v = v_ref[b, :, sl]
            b_col = cum[:, fcol:fcol + 1]
            b_row = cum_t[fcol:fcol + 1, :]
            li_col = gates[:, icol:icol + 1]
            li_row = gates_t[icol:icol + 1, :]
            gtot = cum[last:last + 1, fcol:fcol + 1]
            m_prev = m_scr[idx][:, :1]
            n_prev = n_scr[idx]
            s_prev = s_scr[idx]

            dm = jnp.where(causal, b_col - b_row + li_row, -jnp.inf)
            m_inter = b_col + m_prev
            m_t = jnp.maximum(m_inter, jnp.max(dm, axis=1, keepdims=True))
            p = jnp.exp(dm - m_t)
            qk = lax.dot_general(q, k, (((1,), (1,)), ((), ())), preferred_element_type=F32)
            sc = qk * p
            w_inter = jnp.exp(m_inter - m_t)
            num = (jnp.dot(sc.astype(BF16), v, preferred_element_type=F32)
                   + w_inter * jnp.dot(q, s_prev.astype(BF16), preferred_element_type=F32))
            den = (jnp.sum(sc, axis=1, keepdims=True)
                   + w_inter * jnp.sum(q.astype(F32) * n_prev, axis=1, keepdims=True))
            h_ref[b, :, sl] = num / jnp.maximum(jnp.abs(den), jnp.exp(-m_t))

            w_st = gtot - b_col + li_col
            m_loc = jnp.max(w_st, axis=0, keepdims=True)
            ak = jnp.exp(w_st - m_loc) * k.astype(F32)
            s_c = lax.dot_general(ak.astype(BF16), v, (((0,), (0,)), ((), ())), preferred_element_type=F32)
            n_c = jnp.sum(ak, axis=0, keepdims=True)
            m_new = jnp.maximum(gtot + m_prev, m_loc)
            decay = jnp.exp(gtot + m_prev - m_new)
            add = jnp.exp(m_loc - m_new)
            s_scr[idx] = decay * s_prev + add * s_c
            n_scr[idx] = decay * n_prev + add * n_c
            m_scr[idx] = jnp.broadcast_to(m_new, (1, HEAD_DIM))


def _mlstm_scan(qm, km, z, gates, gates_t, gate_b, *, batch, seq):
    n = qm.shape[0]
    nc = seq // M_CHUNK
    width = N_M_HEADS * HEAD_DIM
    ng = 4 * N_M_HEADS
    chains = 2 * batch * N_M_HEADS
    fwd = lambda c: (0, c, 0)
    bwd = lambda c: (0, nc - 1 - c, 0)
    fwd_v = lambda c: (0, c, Z_MV)
    bwd_v = lambda c: (0, nc - 1 - c, Z_MV)
    fwd_t = lambda c: (0, 0, c)
    bwd_t = lambda c: (0, 0, nc - 1 - c)
    blk = (batch, M_CHUNK, width)
    gblk = (batch, M_CHUNK, LANES)
    tblk = (batch, ng, M_CHUNK)
    bias = jnp.zeros((1, LANES), F32).at[0, :ng].set(gate_b)
    bias_t = jnp.broadcast_to(gate_b[:, None], (ng, M_CHUNK))
    const = lambda c: (0, 0)
    in_specs = [
        pl.BlockSpec(blk, fwd), pl.BlockSpec(blk, fwd), pl.BlockSpec(blk, fwd_v),
        pl.BlockSpec(gblk, fwd), pl.BlockSpec(tblk, fwd_t),
        pl.BlockSpec(blk, bwd), pl.BlockSpec(blk, bwd), pl.BlockSpec(blk, bwd_v),
        pl.BlockSpec(gblk, bwd), pl.BlockSpec(tblk, bwd_t),
        pl.BlockSpec((1, LANES), const), pl.BlockSpec((ng, M_CHUNK), const),
    ]
    q3, k3, z3, g3 = (a.reshape(batch, seq, a.shape[1]) for a in (qm, km, z, gates))
    hf, hb = pl.pallas_call(
        _mscan_kernel,
        grid=(nc,),
        in_specs=in_specs,
        out_specs=[pl.BlockSpec(blk, fwd), pl.BlockSpec(blk, bwd)],
        out_shape=[jax.ShapeDtypeStruct((batch, seq, width), F32)] * 2,
        scratch_shapes=[pltpu.VMEM((chains, HEAD_DIM, HEAD_DIM), F32),
                        pltpu.VMEM((chains, 1, HEAD_DIM), F32),
                        pltpu.VMEM((chains, 1, HEAD_DIM), F32)],
        compiler_params=_cparams(("arbitrary",)),
        name="mlstm_scan",
    )(q3, k3, z3, g3, gates_t, q3, k3, z3, g3, gates_t, bias, bias_t)
    return hf.reshape(n, width), hb.reshape(n, width)


def _mout_kernel(hf_ref, hb_ref, o_ref, g_ref, y_ref):
    for h in range(N_M_HEADS):
        sl = slice(h * HEAD_DIM, (h + 1) * HEAD_DIM)
        x = hf_ref[:, sl] + hb_ref[:, sl]
        ms = jnp.mean(x * x, axis=-1, keepdims=True)
        hn = x * lax.rsqrt(ms + EPS) * g_ref[...]
        y_ref[:, sl] = (jax.nn.sigmoid(o_ref[:, sl].astype(F32)) * hn).astype(y_ref.dtype)


def _mlstm_out(hf, hb, z, norm_g, *, tm=1024):
    n, width = hf.shape
    spec = pl.BlockSpec((tm, width), lambda i: (i, 0))
    return pl.pallas_call(
        _mout_kernel,
        grid=(n // tm,),
        in_specs=[spec, spec, pl.BlockSpec((tm, width), lambda i: (i, Z_MO)),
                  pl.BlockSpec((1, HEAD_DIM), lambda i: (0, 0))],
        out_specs=spec,
        out_shape=jax.ShapeDtypeStruct((n, width), BF16),
        compiler_params=_cparams(("arbitrary",)),
        name="mlstm_out",
    )(hf, hb, z, norm_g.reshape(1, HEAD_DIM))


def _top2(logit):
    lane = lax.broadcasted_iota(jnp.int32, logit.shape, 1).astype(F32)
    logit = jnp.where(lane < N_EXPERTS, logit, -jnp.inf)
    m1 = jnp.max(logit, axis=1, keepdims=True)
    i1 = jnp.min(jnp.where(logit == m1, lane, float(LANES)), axis=1, keepdims=True)
    rest = jnp.where(lane == i1, -jnp.inf, logit)
    m2 = jnp.max(rest, axis=1, keepdims=True)
    i2 = jnp.min(jnp.where(rest == m2, lane, float(LANES)), axis=1, keepdims=True)
    e = jnp.exp(m2 - m1)
    p1 = 1.0 / (1.0 + e)
    p2 = e / (1.0 + e)
    comb = jnp.where(lane == i1, p1, 0.0) + jnp.where(lane == i2, p2, 0.0)
    sel = jnp.where((lane == i1) | (lane == i2), 1.0, 0.0)
    return comb, sel


def _pack_halves(y):
    half = y.shape[1] // 2
    lo = pltpu.bitcast(y[:, :half].astype(BF16).astype(F32), jnp.uint32)
    hi = pltpu.bitcast(y[:, half:].astype(BF16).astype(F32), jnp.uint32)
    return hi | (lo >> 16)


def _unpack_halves(w):
    lo = pltpu.bitcast(w << 16, F32).astype(BF16)
    hi = pltpu.bitcast(w & jnp.uint32(0xFFFF0000), F32).astype(BF16)
    return lo, hi


def _mod_norm(x, g, sc=None, sh=None):
    ms = jnp.mean(x * x, axis=-1, keepdims=True)
    y = x * lax.rsqrt(ms + EPS) * g
    return y if sc is None else y * (1.0 + sc) + sh


def _out_proj_kernel(*refs, n_a, route):
    a_refs = refs[:n_a]
    w_ref, x_ref, g1_ref, ng_ref, sc_ref, sh_ref = refs[n_a:n_a + 6]
    if route:
        rw_ref, rb_ref, xo_ref, hp_ref, comb_ref, sel_ref, wb = refs[n_a + 6:]
    else:
        xo_ref, h_ref, wb = refs[n_a + 6:]

    @pl.when(pl.program_id(0) == 0)
    def _():
        wb[...] = w_ref[...].astype(BF16)

    acc, k0 = None, 0
    for a_ref in a_refs:
        kk = a_ref.shape[1]
        part = jnp.dot(a_ref[...], wb[k0:k0 + kk, :], preferred_element_type=F32)
        acc = part if acc is None else acc + part
        k0 += kk
    xn = x_ref[...] + g1_ref[0] * acc
    xo_ref[...] = xn
    y = _mod_norm(xn, ng_ref[...], sc_ref[0], sh_ref[0])
    if route:
        hp_ref[...] = _pack_halves(y)
        y_hi = y.astype(BF16)
        y_lo = (y - y_hi.astype(F32)).astype(BF16)
        logit = (jnp.dot(y_hi, rw_ref[0], preferred_element_type=F32)
                 + jnp.dot(y_lo, rw_ref[0], preferred_element_type=F32)
                 + jnp.dot(y_hi, rw_ref[1], preferred_element_type=F32)) + rb_ref[...]
        comb_ref[...], sel_ref[...] = _top2(logit)
    else:
        h_ref[...] = y.astype(h_ref.dtype)


def _out_proj(a_list, w_out, layer, x, g1, norm_g, sc, sh, router=None, *, rows_per_batch, tm=512):
    n, d = x.shape
    tpb = rows_per_batch // tm
    route = router is not None
    row = lambda w: pl.BlockSpec((tm, w), lambda i: (i, 0))
    per_batch = pl.BlockSpec((1, 1, d), lambda i: (i // tpb, 0, 0))
    const = lambda shape: pl.BlockSpec(shape, lambda i: (0,) * len(shape))
    in_specs = [row(a.shape[1]) for a in a_list]
    in_specs += [pl.BlockSpec((None, d, d), lambda i: (layer, 0, 0), pipeline_mode=pl.Buffered(1)), row(d), per_batch,
                 const((1, d)), per_batch, per_batch]
    args = list(a_list) + [w_out, x, g1, norm_g.reshape(1, d), sc, sh]
    out_specs = [row(d)]
    out_shape = [jax.ShapeDtypeStruct((n, d), F32)]
    if route:
        router_w, router_b = router
        rw = _pad_cols(router_w, LANES)
        rw_hi = rw.astype(BF16)
        rw_split = jnp.stack([rw_hi, (rw - rw_hi.astype(F32)).astype(BF16)])
        in_specs += [const((2, d, LANES)), const((1, LANES))]
        args += [rw_split, jnp.zeros((1, LANES), F32).at[0, :N_EXPERTS].set(router_b)]
        out_specs += [row(d // 2), row(LANES), row(LANES)]
        out_shape += [jax.ShapeDtypeStruct((n, d // 2), jnp.uint32), jax.ShapeDtypeStruct((n, LANES), F32),
                      jax.ShapeDtypeStruct((n, LANES), F32)]
    else:
        out_specs.append(row(d))
        out_shape.append(jax.ShapeDtypeStruct((n, d), BF16))
    return pl.pallas_call(
        functools.partial(_out_proj_kernel, n_a=len(a_list), route=route),
        grid=(n // tm,),
        in_specs=in_specs,
        out_specs=out_specs,
        out_shape=out_shape,
        scratch_shapes=[pltpu.VMEM((d, d), BF16)],
        compiler_params=_cparams(("arbitrary",)),
        name="out_proj_route" if route else "out_proj",
    )(*args)


def _rank_kernel(sel_ref, rank_ref, tot_ref, carry):
    @pl.when(pl.program_id(0) == 0)
    def _():
        carry[...] = jnp.zeros(carry.shape, F32)

    sel = sel_ref[...]
    tb = sel.shape[0]
    r = lax.broadcasted_iota(jnp.int32, (tb, tb), 0)
    c = lax.broadcasted_iota(jnp.int32, (tb, tb), 1)
    tri = jnp.where(r >= c, 1.0, 0.0).astype(BF16)
    incl = jnp.dot(tri, sel.astype(BF16), preferred_element_type=F32)
    rank_ref[...] = incl - sel + carry[...]
    total = carry[...] + incl[tb - 1:tb, :]
    carry[...] = total
    tot_ref[...] = total


def _rank(sel, *, tb=512):
    n = sel.shape[0]
    return pl.pallas_call(
        _rank_kernel,
        grid=(n // tb,),
        in_specs=[pl.BlockSpec((tb, LANES), lambda i: (i, 0))],
        out_specs=[pl.BlockSpec((tb, LANES), lambda i: (i, 0)), pl.BlockSpec((1, LANES), lambda i: (0, 0))],
        out_shape=[jax.ShapeDtypeStruct((n, LANES), F32), jax.ShapeDtypeStruct((1, LANES), F32)],
        scratch_shapes=[pltpu.VMEM((1, LANES), F32)],
        compiler_params=_cparams(("arbitrary",)),
        name="moe_rank",
    )(sel)


def _dest_kernel(rank_ref, sel_ref, comb_ref, off_ref, dest_ref, p_ref):
    lane = lax.broadcasted_iota(jnp.int32, rank_ref.shape, 1).astype(F32)
    sel = sel_ref[...] > 0.5
    pos = rank_ref[...] + off_ref[...]
    la = jnp.min(jnp.where(sel, lane, float(LANES)), axis=1, keepdims=True)
    lb = jnp.max(jnp.where(sel, lane, -1.0), axis=1, keepdims=True)
    pick = lambda l, v: jnp.sum(jnp.where(lane == l, v, 0.0), axis=1, keepdims=True)
    two = lambda a, b: jnp.where(lane == 0.0, a, jnp.where(lane == 1.0, b, 0.0))
    dest_ref[...] = two(pick(la, pos), pick(lb, pos)).astype(jnp.int32)
    p_ref[...] = two(pick(la, comb_ref[...]), pick(lb, comb_ref[...]))


def _dest(rank, sel, comb, off, *, tb=1024):
    n = rank.shape[0]
    spec = pl.BlockSpec((tb, LANES), lambda i: (i, 0))
    return pl.pallas_call(
        _dest_kernel,
        grid=(n // tb,),
        in_specs=[spec, spec, spec, pl.BlockSpec((1, LANES), lambda i: (0, 0))],
        out_specs=[spec, spec],
        out_shape=[jax.ShapeDtypeStruct((n, LANES), jnp.int32), jax.ShapeDtypeStruct((n, LANES), F32)],
        compiler_params=_cparams(("arbitrary",)),
        name="moe_dest",
    )(rank, sel, comb, off)


def _row_copy(src, src_row, dst, dst_row, sem):
    return pltpu.make_async_copy(src.at[pl.ds(src_row, 1), :], dst.at[pl.ds(dst_row, 1), :], sem)


def _dispatch_kernel(dest_ref, hp_ref, xs_in_ref, xs_ref, sem):
    del xs_in_ref
    tb = hp_ref.shape[0]
    for r in range(tb):
        for s in range(2):
            _row_copy(hp_ref, r, xs_ref, dest_ref[s, r], sem).start(priority=s)
    for s in range(2):
        pltpu.make_async_copy(hp_ref, xs_ref.at[pl.ds(0, tb), :], sem).wait()


def _dispatch(hp, dest3, rows, *, tb):
    n, w = hp.shape
    xs0 = jnp.zeros((rows, w), hp.dtype)
    return pl.pallas_call(
        _dispatch_kernel,
        grid=(n // tb,),
        in_specs=[pl.BlockSpec((None, 2, tb), lambda i: (i, 0, 0), memory_space=pltpu.SMEM),
                  pl.BlockSpec((tb, w), lambda i: (i, 0)),
                  pl.BlockSpec(memory_space=pl.ANY)],
        out_specs=pl.BlockSpec(memory_space=pl.ANY),
        out_shape=jax.ShapeDtypeStruct((rows, w), hp.dtype),
        scratch_shapes=[pltpu.SemaphoreType.DMA(())],
        input_output_aliases={2: 0},
        compiler_params=_cparams(("arbitrary",)),
        name="moe_dispatch",
    )(dest3, hp, xs0)


def _expert_changed(te_ref, t):
    return (t == 0) | (te_ref[t] != te_ref[jnp.maximum(t - 1, 0)])


def _moe_up_kernel(te_ref, nu_ref, xs_ref, w1_ref, w3_ref, o_ref, w1b, w3b):
    t = pl.program_id(1)

    @pl.when(_expert_changed(te_ref, t))
    def _():
        w1b[...] = w1_ref[...].astype(BF16)
        w3b[...] = w3_ref[...].astype(BF16)

    @pl.when(t < nu_ref[0])
    def _():
        x = jnp.concatenate(_unpack_halves(xs_ref[...]), axis=1)
        a1 = jnp.dot(x, w1b[...], preferred_element_type=F32)
        a3 = jnp.dot(x, w3b[...], preferred_element_type=F32)
        o_ref[...] = (_silu(a1) * a3).astype(o_ref.dtype)

    @pl.when(t >= nu_ref[0])
    def _():
        o_ref[...] = jnp.zeros(o_ref.shape, o_ref.dtype)


def _moe_up(xs, w1, w3, layer, te, nu, *, tg, tn):
    rows, half = xs.shape
    _, e, d, f = w1.shape
    wspec = pl.BlockSpec((None, None, d, tn), lambda j, t, te, nu: (layer, te[t], 0, j))
    return pl.pallas_call(
        _moe_up_kernel,
        grid_spec=pltpu.PrefetchScalarGridSpec(
            num_scalar_prefetch=2,
            grid=(f // tn, rows // tg),
            in_specs=[pl.BlockSpec((tg, half), lambda j, t, te, nu: (t, 0)), wspec, wspec],
            out_specs=pl.BlockSpec((tg, tn), lambda j, t, te, nu: (t, j)),
            scratch_shapes=[pltpu.VMEM((d, tn), BF16)] * 2),
        out_shape=jax.ShapeDtypeStruct((rows, f), BF16),
        compiler_params=_cparams(("arbitrary", "arbitrary")),
        name="moe_up",
    )(te, nu, xs, w1, w3)


def _moe_down_kernel(te_ref, nu_ref, a_ref, w2_ref, o_ref, w2b):
    t = pl.program_id(1)

    @pl.when(_expert_changed(te_ref, t))
    def _():
        w2b[...] = w2_ref[...].astype(BF16)

    @pl.when(t < nu_ref[0])
    def _():
        o_ref[...] = _pack_halves(jnp.dot(a_ref[...], w2b[...], preferred_element_type=F32))

    @pl.when(t >= nu_ref[0])
    def _():
        o_ref[...] = jnp.zeros(o_ref.shape, o_ref.dtype)


def _moe_down(a, w2, layer, te, nu, *, tg, tn):
    rows, f = a.shape
    d = w2.shape[-1]
    return pl.pallas_call(
        _moe_down_kernel,
        grid_spec=pltpu.PrefetchScalarGridSpec(
            num_scalar_prefetch=2,
            grid=(d // tn, rows // tg),
            in_specs=[pl.BlockSpec((tg, f), lambda j, t, te, nu: (t, 0)),
                      pl.BlockSpec((None, None, f, tn), lambda j, t, te, nu: (layer, te[t], 0, j))],
            out_specs=pl.BlockSpec((tg, tn // 2), lambda j, t, te, nu: (t, j)),
            scratch_shapes=[pltpu.VMEM((f, tn), BF16)]),
        out_shape=jax.ShapeDtypeStruct((rows, d // 2), jnp.uint32),
        compiler_params=_cparams(("arbitrary", "arbitrary")),
        name="moe_down",
    )(te, nu, a, w2)


def _combine_kernel(*refs, modulate, pack_w):
    if modulate:
        dest_ref, x_ref, g_ref, p_ref, ng_ref, sc_ref, sh_ref, ys_ref, o_ref, h_ref, buf, sem = refs
    else:
        dest_ref, x_ref, g_ref, p_ref, ng_ref, ys_ref, o_ref, h_ref, buf, sem = refs
    tb = x_ref.shape[0]
    for r in range(tb):
        for s in range(2):
            _row_copy(ys_ref, dest_ref[s, r], buf.at[s], r, sem).start(priority=s)
    for s in range(2):
        pltpu.make_async_copy(ys_ref.at[pl.ds(0, tb), :], buf.at[s], sem).wait()

    def expert_rows(s):
        w = buf[s]
        parts = []
        for c in range(w.shape[1] // pack_w):
            wc = w[:, c * pack_w:(c + 1) * pack_w]
            parts += [pltpu.bitcast(wc << 16, F32), pltpu.bitcast(wc & jnp.uint32(0xFFFF0000), F32)]
        return jnp.concatenate(parts, axis=1)

    p = p_ref[...]
    xn = x_ref[...] + g_ref[0] * (p[:, 0:1] * expert_rows(0) + p[:, 1:2] * expert_rows(1))
    o_ref[...] = xn
    if modulate:
        h_ref[...] = _mod_norm(xn, ng_ref[...], sc_ref[0], sh_ref[0]).astype(h_ref.dtype)
    else:
        h_ref[...] = _mod_norm(xn, ng_ref[...]).astype(h_ref.dtype)


def _combine(x, g, p, dest3, ys, norm_g, sc, sh, *, rows_per_batch, tb, h_dtype, pack_w):
    n, d = x.shape
    tpb = rows_per_batch // tb
    modulate = sc is not None
    row = pl.BlockSpec((tb, d), lambda i: (i, 0))
    per_batch = pl.BlockSpec((1, 1, d), lambda i: (i // tpb, 0, 0))
    in_specs = [pl.BlockSpec((None, 2, tb), lambda i: (i, 0, 0), memory_space=pltpu.SMEM), row, per_batch,
                pl.BlockSpec((tb, LANES), lambda i: (i, 0)), pl.BlockSpec((1, d), lambda i: (0, 0))]
    args = [dest3, x, g, p, norm_g.reshape(1, d)]
    if modulate:
        in_specs += [per_batch, per_batch]
        args += [sc, sh]
    return pl.pallas_call(
        functools.partial(_combine_kernel, modulate=modulate, pack_w=pack_w),
        grid=(n // tb,),
        in_specs=in_specs + [pl.BlockSpec(memory_space=pl.ANY)],
        out_specs=[row, row],
        out_shape=[jax.ShapeDtypeStruct((n, d), F32), jax.ShapeDtypeStruct((n, d), h_dtype)],
        scratch_shapes=[pltpu.VMEM((2, tb, d // 2), jnp.uint32), pltpu.SemaphoreType.DMA(())],
        compiler_params=_cparams(("arbitrary",)),
        name="moe_combine",
    )(*args, ys)


def _moe_ffn(x, hp, comb, sel, g, w1, w3, w2, layer, next_norm, *, rows_per_batch, h_dtype, tg=512, tb=256,
             tn_up=896, tn_down=1024):
    n, d = x.shape
    e = w1.shape[1]
    t_max = 2 * n // tg + e
    rank, tot = _rank(sel)
    cnt = tot[0, :e].astype(jnp.int32)
    tiles = (cnt + tg - 1) // tg
    tile_end = jnp.cumsum(tiles)
    n_used = tile_end[-1]
    off = jnp.zeros((1, LANES), F32).at[0, :e].set(((tile_end - tiles) * tg).astype(F32))
    te = jnp.sum(jnp.arange(t_max, dtype=jnp.int32)[:, None] >= tile_end[None, :], axis=1).astype(jnp.int32)
    te = jnp.minimum(te, te[jnp.maximum(n_used - 1, 0)])
    nu = n_used.reshape(1).astype(jnp.int32)
    dest, p = _dest(rank, sel, comb, off)
    dest3 = dest[:, :2].T.reshape(2, n // tb, tb).transpose(1, 0, 2)
    xs = _dispatch(hp, dest3, t_max * tg, tb=tb)
    a = _moe_up(xs, w1, w3, layer, te, nu, tg=tg, tn=tn_up)
    ys = _moe_down(a, w2, layer, te, nu, tg=tg, tn=tn_down)
    return _combine(x, g, p, dest3, ys, *next_norm, rows_per_batch=rows_per_batch, tb=tb, h_dtype=h_dtype,
                    pack_w=tn_down // 2)


def _pad_cols(w, width):
    return jnp.zeros((w.shape[0], width), w.dtype).at[:, :w.shape[1]].set(w)


def kernel(x, c, ada_w, ada_b, norm1_g, norm2_g, w_in, w_out, pool_w, pool_scale, q_norm_g, k_norm_g,
           m_conv_w, m_wq, m_wk, m_gate_b, m_norm_g, ffn_w1, ffn_w3, ffn_w2, router_w, router_b,
           moe_w1, moe_w3, moe_w2, final_norm_g):
    batch, seq, d = x.shape
    depth = ada_w.shape[0]
    n = batch * seq
    ng = 4 * N_M_HEADS
    d_main = w_in.shape[2] - ng
    assert d_main == (Z_MO + 1) * Z_BLOCK and d_main % LANES == 0, w_in.shape
    xf = x.reshape(n, d)
    mod = _adaln_mod(c, ada_w, ada_b)
    cos, sin = _rope_tables(seq)
    dims = dict(batch=batch, seq=seq)

    mods = [[mod[l, :, i * d:(i + 1) * d].reshape(batch, 1, d) for i in range(6)] for l in range(depth)]
    h = None
    for l in range(depth):
        sh1, sc1, g1, sh2, sc2, g2 = mods[l]
        if h is None:
            h = _norm(xf, norm1_g[l], sc1, sh1, rows_per_batch=seq, out_dtype=BF16)
        z = _in_proj(h, w_in, l, cos, sin, q_norm_g[l], k_norm_g[l], seq=seq, ncols=d_main)
        gates = _mm([h], w_in, (l,), tm=1024, tn=LANES, ncols=LANES, col_blk0=d_main // LANES, valid_cols=ng)
        y_pool = _pool_mixer(z, pool_w[l], pool_scale[l], **dims)
        y_attn = _attention(z, q_col=Z_Q * Z_BLOCK, tq=4096, tk=256, **dims)
        qm, km = _mlstm_qk(z, m_conv_w[l], m_wq[l], m_wk[l], **dims)
        gates_t = gates[:, :ng].reshape(batch, seq, ng).transpose(0, 2, 1)
        hf, hb = _mlstm_scan(qm, km, z, gates, gates_t, m_gate_b[l], **dims)
        y_m = _mlstm_out(hf, hb, z, m_norm_g[l])
        mixed = [y_pool, y_attn, y_m]
        i = l // 2
        if l % 2 == 0:
            xf, h2 = _out_proj(mixed, w_out, l, xf, g1, norm2_g[l], sc2, sh2, rows_per_batch=seq)
            a = _swiglu_up(h2, ffn_w1, ffn_w3, (i,), tm=1024, tn=512)
            xf = _mm([a], ffn_w2, (i,), tm=512, tn=512, x=xf, g=g2, rows_per_batch=seq)
            h = None
        else:
            xf, hp, comb, sel = _out_proj(mixed, w_out, l, xf, g1, norm2_g[l], sc2, sh2,
                                          (router_w[i], router_b[i]), rows_per_batch=seq)
            last = l == depth - 1
            next_norm = (final_norm_g, None, None) if last else (norm1_g[l + 1], mods[l + 1][1], mods[l + 1][0])
            xf, h = _moe_ffn(xf, hp, comb, sel, g2, moe_w1, moe_w3, moe_w2, i, next_norm, rows_per_batch=seq,
                             h_dtype=F32 if last else BF16)
            if last:
                return h.reshape(batch, seq, d)

    return _norm(xf, final_norm_g, rows_per_batch=seq, out_dtype=F32).reshape(batch, seq, d)
```
